```python
import jax, jax.numpy as jnp
from jax import lax
import numpy as np

D_MODEL = 1024
BATCH = 32
SEQ = 256
DEPTH = 4
DEC_BATCH = 4
DEC_SEQ = 2048
PAST_LEN = 256

GRID_W = 64
N_MIXERS = 3
N_ATTN_LAYERS = (DEPTH + 2) // 3
N_POOL_LAYERS = (DEPTH + 1) // 3
N_RET_LAYERS = DEPTH // 3

HEAD_DIM = 128
N_HEADS = D_MODEL // HEAD_DIM
N_KV_HEADS = 2
Q_BLOCK = 128
ROPE_THETA = 10000.0

POOL_WINDOWS = (2, 4, 8, 16)
POOL_GROUP = D_MODEL // 4

RET_HEADS = 8
RET_DK = D_MODEL // RET_HEADS
RET_DV = 2 * RET_DK
RET_CHUNK = 128

N_GROUPS = 4
EXPERTS_PER_GROUP = 4
N_EXPERTS = N_GROUPS * EXPERTS_PER_GROUP
D_EXPERT = 256
TOP_K_IN_GROUP = 2

EPS = 1e-6

kernel_name = 'hybrid_diffusion_trunk_step'


def _rmsnorm(x, g):
    xf = x.astype(jnp.float32)
    y = xf * lax.rsqrt(jnp.mean(xf * xf, axis=-1, keepdims=True) + EPS)
    return (y * g.astype(jnp.float32)).astype(x.dtype)


def _adaln(cond, w, b):
    m = jax.nn.silu(cond) @ w + b
    return jnp.split(m, 6, axis=-1)


def _modulate(h, shift, scale):
    return h * (1 + scale) + shift


def _rotate(x, ang):
    f = ang.shape[-1]
    cos = jnp.cos(ang)[None, :, None, :].astype(x.dtype)
    sin = jnp.sin(ang)[None, :, None, :].astype(x.dtype)
    x1, x2 = x[..., :f], x[..., f:]
    return jnp.concatenate([x1 * cos - x2 * sin, x1 * sin + x2 * cos], axis=-1)


def _axial_rope(x):
    L = x.shape[1]
    rows = L // GRID_W
    t_row = jnp.broadcast_to(jnp.arange(rows)[:, None], (rows, GRID_W)).reshape(-1)
    t_col = jnp.broadcast_to(jnp.arange(GRID_W)[None, :], (rows, GRID_W)).reshape(-1)
    nf = HEAD_DIM // 4
    inv = ROPE_THETA ** (-jnp.arange(nf, dtype=jnp.float32) / nf)
    ang_r = t_row.astype(jnp.float32)[:, None] * inv[None, :]
    ang_c = t_col.astype(jnp.float32)[:, None] * inv[None, :]
    half = HEAD_DIM // 2
    return jnp.concatenate([_rotate(x[..., :half], ang_r), _rotate(x[..., half:], ang_c)], axis=-1)


def _attend(q, k, v):
    B, Lq, H, Dh = q.shape
    kv = k.shape[2]
    g = H // kv
    nb = Lq // Q_BLOCK
    qb = q.reshape(B, nb, Q_BLOCK, kv, g, Dh).transpose(1, 0, 2, 3, 4, 5)
    scale = Dh ** -0.5

    def one_block(qblk):
        s = jnp.einsum('bqkgd,bskd->bkgqs', qblk, k).astype(jnp.float32) * scale
        p = jax.nn.softmax(s, axis=-1).astype(v.dtype)
        return jnp.einsum('bkgqs,bskd->bqkgd', p, v)

    o = lax.map(one_block, qb)
    return o.transpose(1, 0, 2, 3, 4, 5).reshape(B, Lq, H * Dh)


def _attn_project(h, w_qkv, q_norm, k_norm):
    B, L, _ = h.shape
    p = h @ w_qkv
    nq = N_HEADS * HEAD_DIM
    nk = N_KV_HEADS * HEAD_DIM
    q = _rmsnorm(p[..., :nq].reshape(B, L, N_HEADS, HEAD_DIM), q_norm)
    k = _rmsnorm(p[..., nq:nq + nk].reshape(B, L, N_KV_HEADS, HEAD_DIM), k_norm)
    v = p[..., nq + nk:].reshape(B, L, N_KV_HEADS, HEAD_DIM)
    return q, k, v


def _attn_context(h, w_qkv, q_norm, k_norm, w_o):
    q, k, v = _attn_project(h, w_qkv, q_norm, k_norm)
    return _attend(q, k, v) @ w_o, k, v


def _attn_latent(h, k_ctx, v_ctx, w_qkv, q_norm, k_norm, w_o):
    q, k, v = _attn_project(h, w_qkv, q_norm, k_norm)
    keys = jnp.concatenate([k_ctx.astype(k.dtype), _axial_rope(k)], axis=1)
    vals = jnp.concatenate([v_ctx.astype(v.dtype), v], axis=1)
    return _attend(_axial_rope(q), keys, vals) @ w_o


def _pool_mixer(h, w, scale):
    B, L, D = h.shape
    hf = h.astype(jnp.float32)
    cs = jnp.concatenate([jnp.zeros((B, 1, D), jnp.float32), jnp.cumsum(hf, axis=1)], axis=1)
    t = jnp.arange(L)
    outs = []
    for gi, win in enumerate(POOL_WINDOWS):
        lo, hi = gi * POOL_GROUP, (gi + 1) * POOL_GROUP
        start = jnp.clip(t - win // 2, 0, L)
        end = jnp.clip(t + win - win // 2, 0, L)
        csg = cs[..., lo:hi]
        mean = (jnp.take(csg, end, axis=1) - jnp.take(csg, start, axis=1)) / (end - start).astype(jnp.float32)[None, :, None]
        outs.append((mean - hf[..., lo:hi]).astype(h.dtype) @ w[gi])
    return jnp.concatenate(outs, axis=-1) * scale


def _retention_chunkwise(q, k, v, log_g, S0):
    B, L, H, DK = q.shape
    DV = v.shape[-1]
    C = RET_CHUNK
    n = L // C
    idx = jnp.arange(C, dtype=jnp.float32)
    diff = idx[:, None] - idx[None, :]
    intra = jnp.where(diff[None] >= 0, jnp.exp(jnp.maximum(diff, 0.0)[None] * log_g[:, None, None]), 0.0)
    q_dec = jnp.exp((idx + 1.0)[:, None] * log_g[None, :])
    k_dec = jnp.exp((C - 1.0 - idx)[:, None] * log_g[None, :])
    c_dec = jnp.exp(C * log_g)

    def chunks(a):
        return a.astype(jnp.float32).reshape(B, n, C, H, a.shape[-1]).transpose(1, 0, 2, 3, 4)

    def step(S, xs):
        qc, kc, vc = xs
        s = jnp.einsum('bihd,bjhd->bhij', qc, kc) * intra
        o = jnp.einsum('bhij,bjhv->bihv', s, vc) + jnp.einsum('bihd,bhdv->bihv', qc, S) * q_dec[None, :, :, None]
        S = S * c_dec[None, :, None, None] + jnp.einsum('bjhd,bjhv->bhdv', kc * k_dec[None, :, :, None], vc)
        return S, o

    S, o = lax.scan(step, S0.astype(jnp.float32), (chunks(q), chunks(k), chunks(v)))
    return o.transpose(1, 0, 2, 3, 4).reshape(B, L, H, DV), S


def _retention_mixer(h, S0, w_in, decay_logit, norm_g, w_out):
    B, L, _ = h.shape
    p = h @ w_in
    nq = RET_HEADS * RET_DK
    nv = RET_HEADS * RET_DV
    q = p[..., :nq].reshape(B, L, RET_HEADS, RET_DK)
    k = p[..., nq:2 * nq].reshape(B, L, RET_HEADS, RET_DK) * (RET_DK ** -0.5)
    v = p[..., 2 * nq:2 * nq + nv].reshape(B, L, RET_HEADS, RET_DV)
    gate = p[..., 2 * nq + nv:]
    log_g = jax.nn.log_sigmoid(decay_logit.astype(jnp.float32))
    o_f, S_f = _retention_chunkwise(q, k, v, log_g[0], S0[:, 0])
    o_b, S_b = _retention_chunkwise(q[:, ::-1], k[:, ::-1], v[:, ::-1], log_g[1], S0[:, 1])
    o = o_f + o_b[:, ::-1]
    mu = jnp.mean(o, axis=-1, keepdims=True)
    var = jnp.mean(jnp.square(o - mu), axis=-1, keepdims=True)
    o = ((o - mu) * lax.rsqrt(var + EPS)).reshape(B, L, nv) * norm_g.astype(jnp.float32)
    y = (o.astype(h.dtype) * jax.nn.silu(gate)) @ w_out
    return y, jnp.stack([S_f, S_b], axis=1)


def _moe(h, w_rg, b_rg, w_re, b_re, w_gate, w_up, w_down):
    B, L, D = h.shape
    t = h.reshape(B * L, D)
    pg = jax.nn.softmax((t @ w_rg + b_rg).astype(jnp.float32), axis=-1)
    p_top, g_idx = lax.top_k(pg, 1)
    le = (t @ w_re + b_re).astype(jnp.float32).reshape(-1, N_GROUPS, EXPERTS_PER_GROUP)
    le_sel = jnp.einsum('tg,tge->te', jax.nn.one_hot(g_idx[:, 0], N_GROUPS, dtype=jnp.float32), le)
    pe = jax.nn.softmax(le_sel, axis=-1)
    pe_top, e_idx = lax.top_k(pe, TOP_K_IN_GROUP)
    w = p_top * pe_top / jnp.sum(pe_top, axis=-1, keepdims=True)
    expert_id = g_idx * EXPERTS_PER_GROUP + e_idx
    gates = jnp.sum(jax.nn.one_hot(expert_id, N_EXPERTS, dtype=jnp.float32) * w[..., None], axis=1)
    hg = jnp.einsum('td,edf->tef', t, w_gate)
    hu = jnp.einsum('td,edf->tef', t, w_up)
    act = jax.nn.silu(hg) * hu * gates[:, :, None].astype(t.dtype)
    y = jnp.einsum('tef,efd->td', act, w_down)
    return y.reshape(B, L, D)


def setup_inputs(seed: int = 0) -> dict:
    key = jax.random.key(seed)
    ks = list(jax.random.split(key, 32))
    D = D_MODEL

    def nrm(i, shape, s):
        return jax.random.normal(ks[i], shape, jnp.float32) * s

    expo = 5.0 + jnp.arange(RET_HEADS, dtype=jnp.float32)
    base_logit = jnp.log(2.0 ** expo - 1.0)
    return {
        'x_prompt': nrm(0, (BATCH, SEQ, D), 1.0),
        'x_sample': nrm(1, (DEC_BATCH, DEC_SEQ, D), 1.0),
        'cache_k': nrm(2, (DEC_BATCH, N_ATTN_LAYERS, PAST_LEN, N_KV_HEADS, HEAD_DIM), 1.0),
        'cache_v': nrm(3, (DEC_BATCH, N_ATTN_LAYERS, PAST_LEN, N_KV_HEADS, HEAD_DIM), 1.0),
        'state_ret': nrm(4, (DEC_BATCH, N_RET_LAYERS, 2, RET_HEADS, RET_DK, RET_DV), 0.5),
        'c': nrm(5, (DEC_BATCH, D), 1.0),
        'c_ctx': nrm(6, (D,), 1.0),
        'norm1': 1.0 + nrm(7, (DEPTH, D), 0.05),
        'norm2': 1.0 + nrm(8, (DEPTH, D), 0.05),
        'w_ada': nrm(9, (DEPTH, D, 6 * D), 0.5 * D ** -0.5),
        'b_ada': nrm(10, (DEPTH, 6 * D), 0.05),
        'attn_w_qkv': nrm(11, (N_ATTN_LAYERS, D, (N_HEADS + 2 * N_KV_HEADS) * HEAD_DIM), D ** -0.5),
        'attn_q_norm': 1.0 + nrm(12, (N_ATTN_LAYERS, HEAD_DIM), 0.05),
        'attn_k_norm': 1.0 + nrm(13, (N_ATTN_LAYERS, HEAD_DIM), 0.05),
        'attn_w_o': nrm(14, (N_ATTN_LAYERS, N_HEADS * HEAD_DIM, D), (N_HEADS * HEAD_DIM) ** -0.5),
        'pool_w': nrm(15, (N_POOL_LAYERS, 4, POOL_GROUP, POOL_GROUP), POOL_GROUP ** -0.5),
        'pool_scale': 1.0 + nrm(16, (N_POOL_LAYERS, D), 0.05),
        'ret_w_in': nrm(17, (N_RET_LAYERS, D, 2 * RET_HEADS * RET_DK + 2 * RET_HEADS * RET_DV), D ** -0.5),
        'ret_decay_logit': base_logit[None, None, :] + nrm(18, (N_RET_LAYERS, 2, RET_HEADS), 0.1),
        'ret_norm': 1.0 + nrm(19, (N_RET_LAYERS, RET_HEADS * RET_DV), 0.05),
        'ret_w_out': nrm(20, (N_RET_LAYERS, RET_HEADS * RET_DV, D), (RET_HEADS * RET_DV) ** -0.5),
        'moe_w_router_g': nrm(21, (DEPTH, D, N_GROUPS), D ** -0.5),
        'moe_b_router_g': nrm(22, (DEPTH, N_GROUPS), 0.01),
        'moe_w_router_e': nrm(23, (DEPTH, D, N_EXPERTS), D ** -0.5),
        'moe_b_router_e': nrm(24, (DEPTH, N_EXPERTS), 0.01),
        'moe_w_gate': nrm(25, (DEPTH, N_EXPERTS, D, D_EXPERT), D ** -0.5),
        'moe_w_up': nrm(26, (DEPTH, N_EXPERTS, D, D_EXPERT), D ** -0.5),
        'moe_w_down': nrm(27, (DEPTH, N_EXPERTS, D_EXPERT, D), D_EXPERT ** -0.5),
        'norm_f': 1.0 + nrm(28, (D,), 0.05),
    }


def reference(x_prompt, x_sample, cache_k, cache_v, state_ret, c, c_ctx,
              norm1, norm2, w_ada, b_ada,
              attn_w_qkv, attn_q_norm, attn_k_norm, attn_w_o,
              pool_w, pool_scale,
              ret_w_in, ret_decay_logit, ret_norm, ret_w_out,
              moe_w_router_g, moe_b_router_g, moe_w_router_e, moe_b_router_e,
              moe_w_gate, moe_w_up, moe_w_down, norm_f):
    xp, xs = x_prompt, x_sample
    new_k, new_v, new_s = [], [], []
    for i in range(DEPTH):
        kind, j = i % N_MIXERS, i // N_MIXERS
        mp = _adaln(c_ctx[None, None, :], w_ada[i], b_ada[i])
        ms = _adaln(c[:, None, :], w_ada[i], b_ada[i])
        hp = _modulate(_rmsnorm(xp, norm1[i]), mp[0], mp[1])
        hs = _modulate(_rmsnorm(xs, norm1[i]), ms[0], ms[1])
        if kind == 0:
            yp, kc, vc = _attn_context(hp, attn_w_qkv[j], attn_q_norm[j], attn_k_norm[j], attn_w_o[j])
            ys = _attn_latent(hs, cache_k[:, j], cache_v[:, j], attn_w_qkv[j], attn_q_norm[j], attn_k_norm[j], attn_w_o[j])
            new_k.append(kc)
            new_v.append(vc)
        elif kind == 1:
            yp = _pool_mixer(hp, pool_w[j], pool_scale[j])
            ys = _pool_mixer(hs, pool_w[j], pool_scale[j])
        else:
            s0 = jnp.zeros((xp.shape[0], 2, RET_HEADS, RET_DK, RET_DV), jnp.float32)
            yp, sc = _retention_mixer(hp, s0, ret_w_in[j], ret_decay_logit[j], ret_norm[j], ret_w_out[j])
            ys, _ = _retention_mixer(hs, state_ret[:, j], ret_w_in[j], ret_decay_logit[j], ret_norm[j], ret_w_out[j])
            new_s.append(sc.astype(xp.dtype))
        xp = xp + mp[2] * yp
        xs = xs + ms[2] * ys
        hp = _modulate(_rmsnorm(xp, norm2[i]), mp[3], mp[4])
        hs = _modulate(_rmsnorm(xs, norm2[i]), ms[3], ms[4])
        moe_args = (moe_w_router_g[i], moe_b_router_g[i], moe_w_router_e[i], moe_b_router_e[i],
                    moe_w_gate[i], moe_w_up[i], moe_w_down[i])
        xp = xp + mp[5] * _moe(hp, *moe_args)
        xs = xs + ms[5] * _moe(hs, *moe_args)
    y_prompt = _rmsnorm(xp, norm_f)
    y_sample = _rmsnorm(xs, norm_f)
    new_cache_k = jnp.stack(new_k, axis=1)
    new_cache_v = jnp.stack(new_v, axis=1)
    new_state_ret = jnp.stack(new_s, axis=1)
    return (y_prompt, y_sample, new_cache_k, new_cache_v, new_state_ret)
```

```python
import functools

import jax
import jax.numpy as jnp
import numpy as np
from jax import lax
from jax.experimental import pallas as pl
from jax.experimental.pallas import tpu as pltpu

F32 = jnp.float32
BF16 = jnp.bfloat16

D = 1024
NB_P, SEQ_P = 32, 256
NB_S, SEQ_S = 4, 2048
T_P = NB_P * SEQ_P
T_S = NB_S * SEQ_S
T = T_P + T_S
DEPTH = 4
GRID_W = 64
HEAD_DIM = 128
N_HEADS = 8
N_KV = 2
KV_GROUP = N_HEADS // N_KV
PAST = 256
ROPE_THETA = 10000.0
POOL_WINDOWS = (2, 4, 8, 16)
POOL_GROUP = D // 4
POOL_HALO = 8
RET_HEADS = 8
RET_DK = 128
RET_DV = 256
RET_CHUNK = 128
N_GROUPS = 4
EPG = 4
N_EXPERTS = 16
D_EXPERT = 256
N_PAIRS = 6
N_CLASSES = N_GROUPS * N_PAIRS
EPS = 1e-6
LOG2E = 1.4426950408889634
NEG = -1e30
N_SEG = 8

VMEM_LIMIT_BYTES = 52 * 1024 * 1024

TM = 512
TQ = 128
TP = 256
TMM = 128
TC = 256
P_PAD = T + N_CLASSES * TMM
NT_E = P_PAD // TMM


def _params(sem):
  return pltpu.CompilerParams(dimension_semantics=sem, vmem_limit_bytes=VMEM_LIMIT_BYTES)


def _seg(i, tm):
  npt = T_P // tm
  return jnp.where(i < npt, 0, (i - npt) // (SEQ_S // tm) + 1)


def _norm_mod(x, g, shift, scale):
  r = lax.rsqrt(jnp.mean(x * x, axis=-1, keepdims=True) + EPS)
  return ((x * r) * g) * (1.0 + scale) + shift


def _silu(x):
  return x * (1.0 / (1.0 + jnp.exp(-x)))


def _adaln_kernel(c_ref, w_ref, b_ref, o_ref):
  s = _silu(c_ref[...]).astype(BF16)
  o_ref[...] = jnp.dot(s, w_ref[...].astype(BF16), preferred_element_type=F32) + b_ref[...]


def _adaln(cond8, w_ada, b_ada):
  tn = 1536
  return pl.pallas_call(
      _adaln_kernel,
      grid=(DEPTH, 6 * D // tn),
      in_specs=[
          pl.BlockSpec((N_SEG, D), lambda l, j: (0, 0)),
          pl.BlockSpec((None, D, tn), lambda l, j: (l, 0, j)),
          pl.BlockSpec((None, 1, tn), lambda l, j: (l, 0, j)),
      ],
      out_specs=pl.BlockSpec((None, N_SEG, tn), lambda l, j: (l, 0, j)),
      out_shape=jax.ShapeDtypeStruct((DEPTH, N_SEG, 6 * D), F32),
      compiler_params=_params(("arbitrary", "arbitrary")),
      name="adaln",
  )(cond8, w_ada, b_ada.reshape(DEPTH, 1, 6 * D))


def _nm_matmul_kernel(x_ref, g_ref, sh_ref, sc_ref, w_ref, o_ref, *, n_chunk):
  h = _norm_mod(x_ref[...], g_ref[...], sh_ref[...], sc_ref[...]).astype(BF16)
  n = w_ref.shape[1]
  for c in range(0, n, n_chunk):
    o_ref[:, c:c + n_chunk] = jnp.dot(
        h, w_ref[:, c:c + n_chunk], preferred_element_type=F32).astype(o_ref.dtype)


def _nm_matmul(x, g, shift, scale, w, tm, name):
  n = w.shape[1]
  return pl.pallas_call(
      functools.partial(_nm_matmul_kernel, n_chunk=512),
      grid=(T // tm,),
      in_specs=[
          pl.BlockSpec((tm, D), lambda i: (i, 0)),
          pl.BlockSpec((1, D), lambda i: (0, 0)),
          pl.BlockSpec((None, 1, D), lambda i: (_seg(i, tm), 0, 0)),
          pl.BlockSpec((None, 1, D), lambda i: (_seg(i, tm), 0, 0)),
          pl.BlockSpec((D, n), lambda i: (0, 0)),
      ],
      out_specs=pl.BlockSpec((tm, n), lambda i: (i, 0)),
      out_shape=jax.ShapeDtypeStruct((T, n), BF16),
      compiler_params=_params(("arbitrary",)),
      name=name,
  )(x, g, shift, scale, w)


def _mm_res_kernel(a_ref, w_ref, x_ref, gate_ref, o_ref):
  y = jnp.dot(a_ref[...], w_ref[...], preferred_element_type=F32)
  o_ref[...] = x_ref[...] + gate_ref[...] * y


def _mm_res(a, w, x, gate, name):
  k = a.shape[1]
  return pl.pallas_call(
      _mm_res_kernel,
      grid=(T // TM,),
      in_specs=[
          pl.BlockSpec((TM, k), lambda i: (i, 0)),
          pl.BlockSpec((k, D), lambda i: (0, 0)),
          pl.BlockSpec((TM, D), lambda i: (i, 0)),
          pl.BlockSpec((None, 1, D), lambda i: (_seg(i, TM), 0, 0)),
      ],
      out_specs=pl.BlockSpec((TM, D), lambda i: (i, 0)),
      out_shape=jax.ShapeDtypeStruct((T, D), F32),
      compiler_params=_params(("arbitrary",)),
      name=name,
  )(a, w, x, gate)


def _rope(x, c, a, b):
  return x * c + pltpu.roll(x, 96, 1) * a + pltpu.roll(x, 32, 1) * b


def _head_norm(x, w):
  return (x * lax.rsqrt(jnp.mean(x * x, axis=-1, keepdims=True) + EPS)) * w


def _attn_kernel(*refs, latent, tq):
  it = iter(refs)
  q_ref, kn_ref, vn_ref = next(it), next(it), next(it)
  if latent:
    ck_ref, cv_ref = next(it), next(it)
  qw_ref, kw_ref = next(it), next(it)
  if latent:
    cq_ref, aq_ref, bq_ref, ckk_ref, akk_ref, bkk_ref = [next(it) for _ in range(6)]
  o_ref = next(it)
  if not latent:
    ko_ref, vo_ref = next(it), next(it)
  k_s, v_s = next(it), next(it)
  n_cache = PAST if latent else 0

  @pl.when(pl.program_id(2) == 0)
  def _():
    k = _head_norm(kn_ref[...].astype(F32), kw_ref[...])
    if latent:
      k = _rope(k, ckk_ref[...], akk_ref[...], bkk_ref[...])
      k_s[0:PAST, :] = ck_ref[...].astype(BF16)
      v_s[0:PAST, :] = cv_ref[...].astype(BF16)
    else:
      ko_ref[...] = k
      vo_ref[...] = vn_ref[...].astype(F32)
    k_s[n_cache:, :] = k.astype(BF16)
    v_s[n_cache:, :] = vn_ref[...]

  qa = q_ref[...].astype(F32)
  qs = []
  for h in range(KV_GROUP):
    qh = _head_norm(qa[:, h * HEAD_DIM:(h + 1) * HEAD_DIM], qw_ref[...])
    if latent:
      qh = _rope(qh, cq_ref[...], aq_ref[...], bq_ref[...])
    qs.append(qh.astype(BF16))
  qc = jnp.concatenate(qs, axis=0)
  s = lax.dot_general(qc, k_s[...], (((1,), (1,)), ((), ())),
                      preferred_element_type=F32)
  m = jnp.max(s, axis=-1, keepdims=True)
  p = jnp.exp2((s - m) * (HEAD_DIM ** -0.5 * LOG2E))
  l = jnp.sum(p, axis=-1, keepdims=True)
  o = jnp.dot(p.astype(BF16), v_s[...], preferred_element_type=F32) / l
  for h in range(KV_GROUP):
    o_ref[:, h * HEAD_DIM:(h + 1) * HEAD_DIM] = o[h * tq:(h + 1) * tq].astype(BF16)


def _rope_tables():
  rows = SEQ_S // GRID_W
  t_row = jnp.broadcast_to(jnp.arange(rows)[:, None], (rows, GRID_W)).reshape(-1)
  t_col = jnp.broadcast_to(jnp.arange(GRID_W)[None, :], (rows, GRID_W)).reshape(-1)
  nf = HEAD_DIM // 4
  inv = ROPE_THETA ** (-jnp.arange(nf, dtype=F32) / nf)
  ang_r = t_row.astype(F32)[:, None] * inv[None, :]
  ang_c = t_col.astype(F32)[:, None] * inv[None, :]
  cr, sr, cc, sc = jnp.cos(ang_r), jnp.sin(ang_r), jnp.cos(ang_c), jnp.sin(ang_c)
  z = jnp.zeros_like(sr)
  c = jnp.concatenate([cr, cr, cc, cc], axis=1)
  a = jnp.concatenate([-sr, z, -sc, z], axis=1)
  b = jnp.concatenate([z, sr, z, sc], axis=1)
  return c, a, b


def _attn_prompt(qkv, q_norm, k_norm):
  kcol = N_HEADS * HEAD_DIM // HEAD_DIM
  vcol = kcol + N_KV
  out_shapes = (
      jax.ShapeDtypeStruct((T_P, N_HEADS * HEAD_DIM), BF16),
      jax.ShapeDtypeStruct((NB_P, SEQ_P, N_KV * HEAD_DIM), F32),
      jax.ShapeDtypeStruct((NB_P, SEQ_P, N_KV * HEAD_DIM), F32),
  )
  return pl.pallas_call(
      functools.partial(_attn_kernel, latent=False, tq=SEQ_P),
      grid=(NB_P, N_KV, 1),
      in_specs=[
          pl.BlockSpec((SEQ_P, KV_GROUP * HEAD_DIM), lambda b, h, i: (b, h)),
          pl.BlockSpec((SEQ_P, HEAD_DIM), lambda b, h, i: (b, kcol + h)),
          pl.BlockSpec((SEQ_P, HEAD_DIM), lambda b, h, i: (b, vcol + h)),
          pl.BlockSpec((1, HEAD_DIM), lambda b, h, i: (0, 0)),
          pl.BlockSpec((1, HEAD_DIM), lambda b, h, i: (0, 0)),
      ],
      out_specs=(
          pl.BlockSpec((SEQ_P, KV_GROUP * HEAD_DIM), lambda b, h, i: (b, h)),
          pl.BlockSpec((None, SEQ_P, HEAD_DIM), lambda b, h, i: (b, 0, h)),
          pl.BlockSpec((None, SEQ_P, HEAD_DIM), lambda b, h, i: (b, 0, h)),
      ),
      out_shape=out_shapes,
      scratch_shapes=[pltpu.VMEM((SEQ_P, HEAD_DIM), BF16), pltpu.VMEM((SEQ_P, HEAD_DIM), BF16)],
      compiler_params=_params(("arbitrary", "arbitrary", "arbitrary")),
      name="attn_prompt",
  )(qkv, qkv, qkv, q_norm, k_norm)


def _attn_latent(qkv, cache_k, cache_v, q_norm, k_norm, rope):
  kcol = N_HEADS
  vcol = kcol + N_KV
  nq = SEQ_S // TQ
  row0 = T_P // TQ
  seq0 = T_P // SEQ_S
  lk = PAST + SEQ_S
  c, a, b = rope
  tab_q = pl.BlockSpec((TQ, HEAD_DIM), lambda bb, h, i: (i, 0))
  tab_k = pl.BlockSpec((SEQ_S, HEAD_DIM), lambda bb, h, i: (0, 0))
  return pl.pallas_call(
      functools.partial(_attn_kernel, latent=True, tq=TQ),
      grid=(NB_S, N_KV, nq),
      in_specs=[
          pl.BlockSpec((TQ, KV_GROUP * HEAD_DIM), lambda bb, h, i: (row0 + bb * nq + i, h)),
          pl.BlockSpec((SEQ_S, HEAD_DIM), lambda bb, h, i: (seq0 + bb, kcol + h)),
          pl.BlockSpec((SEQ_S, HEAD_DIM), lambda bb, h, i: (seq0 + bb, vcol + h)),
          pl.BlockSpec((None, PAST, HEAD_DIM), lambda bb, h, i: (bb, 0, h)),
          pl.BlockSpec((None, PAST, HEAD_DIM), lambda bb, h, i: (bb, 0, h)),
          pl.BlockSpec((1, HEAD_DIM), lambda bb, h, i: (0, 0)),
          pl.BlockSpec((1, HEAD_DIM), lambda bb, h, i: (0, 0)),
          tab_q, tab_q, tab_q, tab_k, tab_k, tab_k,
      ],
      out_specs=pl.BlockSpec((TQ, KV_GROUP * HEAD_DIM), lambda bb, h, i: (bb * nq + i, h)),
      out_shape=jax.ShapeDtypeStruct((T_S, N_HEADS * HEAD_DIM), BF16),
      scratch_shapes=[pltpu.VMEM((lk, HEAD_DIM), BF16), pltpu.VMEM((lk, HEAD_DIM), BF16)],
      compiler_params=_params(("arbitrary", "arbitrary", "arbitrary")),
      name="attn_latent",
  )(qkv, qkv, qkv, cache_k, cache_v, q_norm, k_norm, c, a, b, c, a, b)


def _pool_kernel(x_ref, xp_ref, xn_ref, g_ref, sh_ref, sc_ref, gate_ref, w_ref, ps_ref, o_ref):
  t = pl.program_id(0)
  npt = T_P // TP
  tiles_s = SEQ_S // TP
  is_p = t < npt
  pos = jnp.where(is_p, 0, (t - npt) % tiles_s)
  ntile = jnp.where(is_p, SEQ_P // TP, tiles_s)
  seq_len = ntile * TP
  keep_prev = jnp.where(pos == 0, 0.0, 1.0)
  keep_next = jnp.where(pos == ntile - 1, 0.0, 1.0)

  g, sh, sc = g_ref[...], sh_ref[...], sc_ref[...]
  x = x_ref[...]
  h = _norm_mod(x, g, sh, sc)
  hp = _norm_mod(xp_ref[...], g, sh, sc) * keep_prev
  hn = _norm_mod(xn_ref[...], g, sh, sc) * keep_next
  ext = jnp.concatenate([hp, h, hn], axis=0)
  n_ext = TP + 2 * POOL_HALO
  tseq = pos * TP + lax.broadcasted_iota(jnp.int32, (TP, POOL_GROUP), 0)

  outs = []
  for gi, win in enumerate(POOL_WINDOWS):
    lo, hi = gi * POOL_GROUP, (gi + 1) * POOL_GROUP
    acc = ext[:, lo:hi]
    span = 1
    while span < win:
      acc = acc + pltpu.roll(acc, n_ext - span, 0)
      span *= 2
    start = POOL_HALO - win // 2
    if start:
      acc = pltpu.roll(acc, n_ext - start, 0)
    ssum = acc[0:TP]
    cnt = (jnp.minimum(tseq + win // 2, seq_len) - jnp.maximum(tseq - win // 2, 0)).astype(F32)
    dlt = (ssum / cnt - h[:, lo:hi]).astype(BF16)
    outs.append(jnp.dot(dlt, w_ref[gi], preferred_element_type=F32))
  y = jnp.concatenate(outs, axis=1) * ps_ref[...]
  o_ref[...] = x + gate_ref[...] * y


def _pool_layer(x, g, shift, scale, gate, w, pscale):
  hb = TP // POOL_HALO
  last = T // POOL_HALO - 1
  seg = lambda i: _seg(i, TP)
  return pl.pallas_call(
      _pool_kernel,
      grid=(T // TP,),
      in_specs=[
          pl.BlockSpec((TP, D), lambda i: (i, 0)),
          pl.BlockSpec((POOL_HALO, D), lambda i: (jnp.maximum(i * hb - 1, 0), 0)),
          pl.BlockSpec((POOL_HALO, D), lambda i: (jnp.minimum((i + 1) * hb, last), 0)),
          pl.BlockSpec((1, D), lambda i: (0, 0)),
          pl.BlockSpec((None, 1, D), lambda i: (seg(i), 0, 0)),
          pl.BlockSpec((None, 1, D), lambda i: (seg(i), 0, 0)),
          pl.BlockSpec((None, 1, D), lambda i: (seg(i), 0, 0)),
          pl.BlockSpec((4, POOL_GROUP, POOL_GROUP), lambda i: (0, 0, 0)),
          pl.BlockSpec((1, D), lambda i: (0, 0)),
      ],
      out_specs=pl.BlockSpec((TP, D), lambda i: (i, 0)),
      out_shape=jax.ShapeDtypeStruct((T, D), F32),
      compiler_params=_params(("arbitrary",)),
      name="pool",
  )(x, x, x, g, shift, scale, gate, w, pscale)


def _ret_kernel(*refs, seq_len, has_s0):
  it = iter(refs)
  q_ref, k_ref, v_ref, gate_ref, dl_ref, ng_ref = [next(it) for _ in range(6)]
  s0_ref = next(it) if has_s0 else None
  y_ref = next(it)
  so_ref = None if has_s0 else next(it)
  o_s, st_s = next(it), next(it)
  c = RET_CHUNK
  n = seq_len // c
  kscale = RET_DK ** -0.5

  dl = dl_ref[...]
  lg = -jnp.log1p(jnp.exp(-dl))
  ri = lax.broadcasted_iota(jnp.int32, (c, c), 0).astype(F32)
  ci = lax.broadcasted_iota(jnp.int32, (c, c), 1).astype(F32)
  ri2 = lax.broadcasted_iota(jnp.int32, (c, RET_DV), 0).astype(F32)

  def decays(lgd, backward):
    lg2 = jnp.concatenate([lgd, lgd], axis=1)
    if backward:
      intra = jnp.where(ri <= ci, jnp.exp((ci - ri) * lgd), 0.0) * kscale
      qd = jnp.exp((c - ri2) * lg2)
      kd = jnp.exp(ri * lgd) * kscale
    else:
      intra = jnp.where(ri >= ci, jnp.exp((ri - ci) * lgd), 0.0) * kscale
      qd = jnp.exp((ri2 + 1.0) * lg2)
      kd = jnp.exp((c - 1.0 - ri) * lgd) * kscale
    return intra, qd, kd, jnp.exp(c * lg2)

  def chunk(r0, dec):
    intra, qd, kd, cdec = dec
    qc = q_ref[pl.ds(r0, c), :]
    kc = k_ref[pl.ds(r0, c), :]
    vc = v_ref[pl.ds(r0, c), :]
    s = lax.dot_general(qc, kc, (((1,), (1,)), ((), ())), preferred_element_type=F32) * intra
    st = st_s[...]
    o = (jnp.dot(s.astype(BF16), vc, preferred_element_type=F32)
         + jnp.dot(qc, st.astype(BF16), preferred_element_type=F32) * qd)
    kdec = (kc.astype(F32) * kd).astype(BF16)
    st_s[...] = st * cdec + lax.dot_general(
        kdec, vc, (((0,), (0,)), ((), ())), preferred_element_type=F32)
    return o

  st_s[...] = s0_ref[0] if has_s0 else jnp.zeros((RET_DK, RET_DV), F32)

  dec_f = decays(lg[0], False)

  def fwd(j, carry):
    r0 = pl.multiple_of(j * c, c)
    o_s[pl.ds(r0, c), :] = chunk(r0, dec_f)
    return carry
  lax.fori_loop(0, n, fwd, 0)
  if not has_s0:
    so_ref[0] = st_s[...]

  st_s[...] = s0_ref[1] if has_s0 else jnp.zeros((RET_DK, RET_DV), F32)

  dec_b = decays(lg[1], True)

  def bwd(j, carry):
    r0 = pl.multiple_of((n - 1 - j) * c, c)
    o = o_s[pl.ds(r0, c), :] + chunk(r0, dec_b)
    mu = jnp.mean(o, axis=-1, keepdims=True)
    d = o - mu
    var = jnp.mean(d * d, axis=-1, keepdims=True)
    on = (d * lax.rsqrt(var + EPS)) * ng_ref[...]
    y_ref[pl.ds(r0, c), :] = (on * _silu(gate_ref[pl.ds(r0, c), :].astype(F32))).astype(BF16)
    return carry
  lax.fori_loop(0, n, bwd, 0)
  if not has_s0:
    so_ref[1] = st_s[...]


def _retention(p, dl, ng, state0, prompt):
  seq_len = SEQ_P if prompt else SEQ_S
  nb = NB_P if prompt else NB_S
  row0 = 0 if prompt else T_P // SEQ_S
  kcol = RET_HEADS
  vcol = (2 * RET_HEADS * RET_DK) // RET_DV
  gcol = vcol + RET_HEADS
  in_specs = [
      pl.BlockSpec((seq_len, RET_DK), lambda b, h: (row0 + b, h)),
      pl.BlockSpec((seq_len, RET_DK), lambda b, h: (row0 + b, kcol + h)),
      pl.BlockSpec((seq_len, RET_DV), lambda b, h: (row0 + b, vcol + h)),
      pl.BlockSpec((seq_len, RET_DV), lambda b, h: (row0 + b, gcol + h)),
      pl.BlockSpec((None, 2, 1, RET_DK), lambda b, h: (h, 0, 0, 0)),
      pl.BlockSpec((1, RET_DV), lambda b, h: (0, h)),
  ]
  args = [p, p, p, p, dl, ng]
  y_spec = pl.BlockSpec((seq_len, RET_DV), lambda b, h: (b, h))
  y_shape = jax.ShapeDtypeStruct((nb * seq_len, RET_HEADS * RET_DV), BF16)
  if prompt:
    out_specs = (y_spec, pl.BlockSpec((None, 2, None, RET_DK, RET_DV), lambda b, h: (b, 0, h, 0, 0)))
    out_shape = (y_shape, jax.ShapeDtypeStruct((NB_P, 2, RET_HEADS, RET_DK, RET_DV), F32))
  else:
    in_specs.append(pl.BlockSpec((None, 2, None, RET_DK, RET_DV), lambda b, h: (b, 0, h, 0, 0)))
    args.append(state0)
    out_specs = y_spec
    out_shape = y_shape
  return pl.pallas_call(
      functools.partial(_ret_kernel, seq_len=seq_len, has_s0=not prompt),
      grid=(nb, RET_HEADS),
      in_specs=in_specs,
      out_specs=out_specs,
      out_shape=out_shape,
      scratch_shapes=[pltpu.VMEM((seq_len, RET_DV), F32), pltpu.VMEM((RET_DK, RET_DV), F32)],
      compiler_params=_params(("arbitrary", "arbitrary")),
      name="ret_prompt" if prompt else "ret_latent",
  )(*args)


def _router_kernel(x_ref, g_ref, sh_ref, sc_ref, wr_ref, br_ref, tri_ref,
                   h_ref, info_ref, cnt_ref, carry_s):
  @pl.when(pl.program_id(0) == 0)
  def _():
    carry_s[...] = jnp.zeros_like(carry_s)

  h = _norm_mod(x_ref[...], g_ref[...], sh_ref[...], sc_ref[...])
  h_ref[...] = h
  logits = jnp.dot(h, wr_ref[...], preferred_element_type=F32,
                   precision=lax.Precision.HIGHEST) + br_ref[...]
  lane = lax.broadcasted_iota(jnp.int32, logits.shape, 1)
  big = jnp.int32(127)

  def first_max(v):
    m = jnp.max(v, axis=-1, keepdims=True)
    idx = jnp.min(jnp.where(v == m, lane, big), axis=-1, keepdims=True)
    return m, idx

  lgm = jnp.where(lane < N_GROUPS, logits, NEG)
  gmax, gidx = first_max(lgm)
  gsum = jnp.sum(jnp.where(lane < N_GROUPS, jnp.exp(lgm - gmax), 0.0), axis=-1, keepdims=True)
  p_top = 1.0 / gsum
  lo = N_GROUPS + EPG * gidx
  le = jnp.where((lane >= lo) & (lane < lo + EPG), logits, NEG)
  m1, i1 = first_max(le)
  m2, i2 = first_max(jnp.where(lane == i1, NEG, le))
  e21 = jnp.exp(m2 - m1)
  w1 = p_top / (1.0 + e21)
  w2 = p_top * e21 / (1.0 + e21)
  first_low = i1 < i2
  e_lo = jnp.minimum(i1, i2) - N_GROUPS
  e_hi = jnp.maximum(i1, i2) - N_GROUPS
  w_lo = jnp.where(first_low, w1, w2)
  w_hi = jnp.where(first_low, w2, w1)
  a = e_lo - EPG * gidx
  b = e_hi - EPG * gidx
  pair_base = jnp.where(a == 0, 0, jnp.where(a == 1, 3, 5))
  cls = N_PAIRS * gidx + pair_base + (b - a - 1)

  onehot = jnp.where(lane == cls, 1.0, 0.0)
  before = jnp.dot(tri_ref[...], onehot.astype(BF16), preferred_element_type=F32) + carry_s[...]
  rank = jnp.sum(jnp.where(lane == cls, before, 0.0), axis=-1, keepdims=True)
  carry_s[...] = carry_s[...] + jnp.sum(onehot, axis=0, keepdims=True)
  cnt_ref[...] = carry_s[...]

  info = jnp.where(lane == 0, cls.astype(F32),
         jnp.where(lane == 1, rank,
         jnp.where(lane == 2, w_lo,
         jnp.where(lane == 3, w_hi, 0.0))))
  info_ref[...] = info


def _router(x, g, shift, scale, wr, br, tri):
  return pl.pallas_call(
      _router_kernel,
      grid=(T // TM,),
      in_specs=[
          pl.BlockSpec((TM, D), lambda i: (i, 0)),
          pl.BlockSpec((1, D), lambda i: (0, 0)),
          pl.BlockSpec((None, 1, D), lambda i: (_seg(i, TM), 0, 0)),
          pl.BlockSpec((None, 1, D), lambda i: (_seg(i, TM), 0, 0)),
          pl.BlockSpec((D, 128), lambda i: (0, 0)),
          pl.BlockSpec((1, 128), lambda i: (0, 0)),
          pl.BlockSpec((TM, TM), lambda i: (0, 0)),
      ],
      out_specs=(
          pl.BlockSpec((TM, D), lambda i: (i, 0)),
          pl.BlockSpec((TM, 128), lambda i: (i, 0)),
          pl.BlockSpec((1, 128), lambda i: (0, 0)),
      ),
      out_shape=(
          jax.ShapeDtypeStruct((T, D), F32),
          jax.ShapeDtypeStruct((T, 128), F32),
          jax.ShapeDtypeStruct((1, 128), F32),
      ),
      scratch_shapes=[pltpu.VMEM((1, 128), F32)],
      compiler_params=_params(("arbitrary",)),
      name="router",
  )(x, g, shift, scale, wr, br, tri)


def _row_copy(src_hbm, row, buf, slot, r, sem):
  return pltpu.make_async_copy(
      src_hbm.at[pl.ds(row, 1), :], buf.at[slot, pl.ds(r, 1), :], sem.at[slot])


def _gather_start(idx_ref, base, src_hbm, buf, slot, sem, rows):
  def body(r, carry):
    _row_copy(src_hbm, idx_ref[base + r], buf, slot, r, sem).start()
    return carry
  lax.fori_loop(0, rows, body, 0, unroll=8)


def _gather_wait(idx_ref, base, src_hbm, buf, slot, sem, rows):
  def body(r, carry):
    _row_copy(src_hbm, idx_ref[base + r], buf, slot, r, sem).wait()
    return carry
  lax.fori_loop(0, rows, body, 0, unroll=8)


def _expert_kernel(elo_ref, ehi_ref, nv_ref, src_ref,
                   h_hbm, wl_ref, wh_ref, wg_lo, wg_hi, wu_lo, wu_hi, wd_lo, wd_hi,
                   y_ref, hbuf, sem):
  del elo_ref, ehi_ref
  i = pl.program_id(0)
  nv = nv_ref[0]
  slot = i % 2

  @pl.when(i == 0)
  def _():
    _gather_start(src_ref, 0, h_hbm, hbuf, 0, sem, TMM)

  @pl.when(i + 1 < nv)
  def _():
    _gather_start(src_ref, (i + 1) * TMM, h_hbm, hbuf, 1 - slot, sem, TMM)

  @pl.when(i < nv)
  def _():
    _gather_wait(src_ref, i * TMM, h_hbm, hbuf, slot, sem, TMM)
    hb = hbuf[slot].astype(BF16)
    wl = jnp.concatenate([wl_ref[...], wl_ref[...]], axis=1)
    wh = jnp.concatenate([wh_ref[...], wh_ref[...]], axis=1)
    dot = functools.partial(jnp.dot, preferred_element_type=F32)
    a_lo = (_silu(dot(hb, wg_lo[...])) * dot(hb, wu_lo[...])) * wl
    a_hi = (_silu(dot(hb, wg_hi[...])) * dot(hb, wu_hi[...])) * wh
    y_ref[...] = dot(a_lo.astype(BF16), wd_lo[...]) + dot(a_hi.astype(BF16), wd_hi[...])

  @pl.when(i >= nv)
  def _():
    y_ref[...] = jnp.zeros_like(y_ref)


def _experts(tile_elo, tile_ehi, n_valid, src, h, wl_rows, wh_rows, w_gate, w_up, w_down):
  up_spec_lo = pl.BlockSpec((None, D, D_EXPERT), lambda i, elo, ehi, nv, src: (elo[i], 0, 0))
  up_spec_hi = pl.BlockSpec((None, D, D_EXPERT), lambda i, elo, ehi, nv, src: (ehi[i], 0, 0))
  dn_spec_lo = pl.BlockSpec((None, D_EXPERT, D), lambda i, elo, ehi, nv, src: (elo[i], 0, 0))
  dn_spec_hi = pl.BlockSpec((None, D_EXPERT, D), lambda i, elo, ehi, nv, src: (ehi[i], 0, 0))
  row_spec = pl.BlockSpec((TMM, 128), lambda i, elo, ehi, nv, src: (i, 0))
  grid_spec = pltpu.PrefetchScalarGridSpec(
      num_scalar_prefetch=4,
      grid=(NT_E,),
      in_specs=[
          pl.BlockSpec(memory_space=pl.ANY),
          row_spec, row_spec,
          up_spec_lo, up_spec_hi, up_spec_lo, up_spec_hi, dn_spec_lo, dn_spec_hi,
      ],
      out_specs=pl.BlockSpec((TMM, D), lambda i, elo, ehi, nv, src: (i, 0)),
      scratch_shapes=[pltpu.VMEM((2, TMM, D), F32), pltpu.SemaphoreType.DMA((2,))],
  )
  return pl.pallas_call(
      _expert_kernel,
      grid_spec=grid_spec,
      out_shape=jax.ShapeDtypeStruct((P_PAD, D), F32),
      compiler_params=_params(("arbitrary",)),
      name="experts",
  )(tile_elo, tile_ehi, n_valid, src, h, wl_rows, wh_rows,
    w_gate, w_gate, w_up, w_up, w_down, w_down)


def _combine_kernel(slot_ref, x_ref, gate_ref, nf_ref, y_hbm, o_ref, ybuf, sem, *, final):
  i = pl.program_id(0)
  n = pl.num_programs(0)
  slot = i % 2

  @pl.when(i == 0)
  def _():
    _gather_start(slot_ref, 0, y_hbm, ybuf, 0, sem, TC)

  @pl.when(i + 1 < n)
  def _():
    _gather_start(slot_ref, (i + 1) * TC, y_hbm, ybuf, 1 - slot, sem, TC)

  _gather_wait(slot_ref, i * TC, y_hbm, ybuf, slot, sem, TC)
  x = x_ref[...] + gate_ref[...] * ybuf[slot]
  if final:
    x = (x * lax.rsqrt(jnp.mean(x * x, axis=-1, keepdims=True) + EPS)) * nf_ref[...]
  o_ref[...] = x


def _combine(slot, x, gate, norm_f, y_sorted, final):
  grid_spec = pltpu.PrefetchScalarGridSpec(
      num_scalar_prefetch=1,
      grid=(T // TC,),
      in_specs=[
          pl.BlockSpec((TC, D), lambda i, s: (i, 0)),
          pl.BlockSpec((None, 1, D), lambda i, s: (_seg(i, TC), 0, 0)),
          pl.BlockSpec((1, D), lambda i, s: (0, 0)),
          pl.BlockSpec(memory_space=pl.ANY),
      ],
      out_specs=pl.BlockSpec((TC, D), lambda i, s: (i, 0)),
      scratch_shapes=[pltpu.VMEM((2, TC, D), F32), pltpu.SemaphoreType.DMA((2,))],
  )
  return pl.pallas_call(
      functools.partial(_combine_kernel, final=final),
      grid_spec=grid_spec,
      out_shape=jax.ShapeDtypeStruct((T, D), F32),
      compiler_params=_params(("arbitrary",)),
      name="combine_final" if final else "combine",
  )(slot, x, gate, norm_f, y_sorted)


def _class_experts():
  lo, hi = [], []
  for g in range(N_GROUPS):
    for a in range(EPG):
      for b in range(a + 1, EPG):
        lo.append(g * EPG + a)
        hi.append(g * EPG + b)
  return np.asarray(lo, np.int32), np.asarray(hi, np.int32)


def _moe_layer(x, g, shift, scale, gate, wr, br, tri, w_gate, w_up, w_down, norm_f, final):
  h, info, cnt = _router(x, g, shift, scale, wr, br, tri)

  cls = info[:, 0].astype(jnp.int32)
  rank = info[:, 1].astype(jnp.int32)
  counts = cnt[0, :N_CLASSES].astype(jnp.int32)
  tiles = (counts + TMM - 1) // TMM
  tile_end = jnp.cumsum(tiles)
  offs = (tile_end - tiles) * TMM
  slot = offs[cls] + rank
  src = jnp.zeros((P_PAD,), jnp.int32).at[slot].set(jnp.arange(T, dtype=jnp.int32))
  n_valid = tile_end[-1]
  tile_ids = jnp.minimum(jnp.arange(NT_E, dtype=jnp.int32), n_valid - 1)
  tile_cls = jnp.minimum(jnp.searchsorted(tile_end, tile_ids, side="right"), N_CLASSES - 1)
  cls_lo, cls_hi = _class_experts()
  tile_elo = jnp.asarray(cls_lo)[tile_cls]
  tile_ehi = jnp.asarray(cls_hi)[tile_cls]
  w_rows = jnp.zeros((P_PAD, 2), F32).at[slot].set(info[:, 2:4])
  wl_rows = jnp.broadcast_to(w_rows[:, 0:1], (P_PAD, 128))
  wh_rows = jnp.broadcast_to(w_rows[:, 1:2], (P_PAD, 128))

  y_sorted = _experts(tile_elo, tile_ehi, n_valid.reshape(1), src, h, wl_rows, wh_rows,
                      w_gate, w_up, w_down)
  return _combine(slot, x, gate, norm_f, y_sorted, final)


def kernel(x_prompt, x_sample, cache_k, cache_v, state_ret, c, c_ctx, norm1, norm2, w_ada, b_ada, attn_w_qkv, attn_q_norm, attn_k_norm, attn_w_o, pool_w, pool_scale, ret_w_in, ret_decay_logit, ret_norm, ret_w_out, moe_w_router_g, moe_b_router_g, moe_w_router_e, moe_b_router_e, moe_w_gate, moe_w_up, moe_w_down, norm_f):
  x = jnp.concatenate([x_prompt.reshape(T_P, D), x_sample.reshape(T_S, D)], axis=0)
  cond8 = jnp.concatenate([c_ctx[None, :], c, jnp.zeros((N_SEG - 1 - NB_S, D), F32)], axis=0)
  mods = _adaln(cond8, w_ada, b_ada)
  mods = mods.reshape(DEPTH, N_SEG, 6, 1, D).transpose(0, 2, 1, 3, 4)
  rope = _rope_tables()
  tri = jnp.tril(jnp.ones((TM, TM), BF16), -1)
  pad_r = 128 - N_GROUPS - N_EXPERTS
  norm_f2 = norm_f.reshape(1, D)

  new_k, new_v, new_s = [], [], []
  for i in range(DEPTH):
    kind, j = i % 3, i // 3
    m = mods[i]
    g1 = norm1[i].reshape(1, D)
    g2 = norm2[i].reshape(1, D)
    if kind == 0:
      qkv = _nm_matmul(x, g1, m[0], m[1], attn_w_qkv[j].astype(BF16), TM, "qkv_proj")
      qn = attn_q_norm[j].reshape(1, HEAD_DIM)
      kn = attn_k_norm[j].reshape(1, HEAD_DIM)
      o_p, kc, vc = _attn_prompt(qkv, qn, kn)
      ck = cache_k[:, j].reshape(NB_S, PAST, N_KV * HEAD_DIM)
      cv = cache_v[:, j].reshape(NB_S, PAST, N_KV * HEAD_DIM)
      o_s = _attn_latent(qkv, ck, cv, qn, kn, rope)
      new_k.append(kc.reshape(NB_P, SEQ_P, N_KV, HEAD_DIM))
      new_v.append(vc.reshape(NB_P, SEQ_P, N_KV, HEAD_DIM))
      x = _mm_res(jnp.concatenate([o_p, o_s], axis=0), attn_w_o[j].astype(BF16), x, m[2], "attn_out")
    elif kind == 1:
      x = _pool_layer(x, g1, m[0], m[1], m[2], pool_w[j].astype(BF16), pool_scale[j].reshape(1, D))
    else:
      p = _nm_matmul(x, g1, m[0], m[1], ret_w_in[j].astype(BF16), 256, "ret_proj")
      dl = jnp.broadcast_to(ret_decay_logit[j].T[:, :, None, None], (RET_HEADS, 2, 1, RET_DK))
      ng = ret_norm[j].reshape(1, RET_HEADS * RET_DV)
      y_p, s_new = _retention(p, dl, ng, None, True)
      y_s = _retention(p, dl, ng, state_ret[:, j], False)
      new_s.append(s_new)
      x = _mm_res(jnp.concatenate([y_p, y_s], axis=0), ret_w_out[j].astype(BF16), x, m[2], "ret_out")
    wr = jnp.concatenate([moe_w_router_g[i], moe_w_router_e[i], jnp.zeros((D, pad_r), F32)], axis=1)
    br = jnp.concatenate([moe_b_router_g[i], moe_b_router_e[i], jnp.zeros((pad_r,), F32)]).reshape(1, 128)
    x = _moe_layer(x, g2, m[3], m[4], m[5], wr, br, tri,
                   moe_w_gate[i].astype(BF16), moe_w_up[i].astype(BF16), moe_w_down[i].astype(BF16),
                   norm_f2, i == DEPTH - 1)

  y_prompt = x[:T_P].reshape(NB_P, SEQ_P, D)
  y_sample = x[T_P:].reshape(NB_S, SEQ_S, D)
  new_cache_k = jnp.stack(new_k, axis=1)
  new_cache_v = jnp.stack(new_v, axis=1)
  new_state_ret = jnp.stack(new_s, axis=1)
  return (y_prompt, y_sample, new_cache_k, new_cache_v, new_state_ret)
```

```python
import functools

import jax
import jax.numpy as jnp
import numpy as np
from jax import lax
from jax.experimental import pallas as pl
from jax.experimental.pallas import tpu as pltpu

F32 = jnp.float32
BF16 = jnp.bfloat16

D = 1024
NB_P, SEQ_P = 32, 256
NB_S, SEQ_S = 4, 2048
T_P = NB_P * SEQ_P
T_S = NB_S * SEQ_S
T = T_P + T_S
DEPTH = 4
GRID_W = 64
HEAD_DIM = 128
N_HEADS = 8
N_KV = 2
KV_GROUP = N_HEADS // N_KV
PAST = 256
ROPE_THETA = 10000.0
POOL_WINDOWS = (2, 4, 8, 16)
POOL_GROUP = D // 4
POOL_HALO = 8
RET_HEADS = 8
RET_DK = 128
RET_DV = 256
RET_CHUNK = 128
N_GROUPS = 4
EPG = 4
N_EXPERTS = 16
D_EXPERT = 256
N_PAIRS = 6
N_CLASSES = N_GROUPS * N_PAIRS
EPS = 1e-6
LOG2E = 1.4426950408889634
NEG = -1e30
N_SEG = 8

VMEM_LIMIT_BYTES = 52 * 1024 * 1024

TM = 512
TQ = 128
TP = 256
TMM = 256
TS = 256
TC = 256
ROW_SUB = 8
PACK_SUB = D // 2 // 128
INFO_SUB = PACK_SUB
P_PAD = T + N_CLASSES * TMM
NT_E = P_PAD // TMM


def _params(sem):
  return pltpu.CompilerParams(dimension_semantics=sem, vmem_limit_bytes=VMEM_LIMIT_BYTES)


def _seg(i, tm):
  npt = T_P // tm
  return jnp.where(i < npt, 0, (i - npt) // (SEQ_S // tm) + 1)


def _norm_mod(x, g, shift, scale):
  r = lax.rsqrt(jnp.mean(x * x, axis=-1, keepdims=True) + EPS)
  return ((x * r) * g) * (1.0 + scale) + shift


def _silu(x):
  return x * (1.0 / (1.0 + jnp.exp(-x)))


def _adaln_kernel(c_ref, w_ref, b_ref, o_ref):
  s = _silu(c_ref[...]).astype(BF16)
  o_ref[...] = jnp.dot(s, w_ref[...].astype(BF16), preferred_element_type=F32) + b_ref[...]


def _adaln(cond8, w_ada, b_ada):
  tn = 1536
  return pl.pallas_call(
      _adaln_kernel,
      grid=(DEPTH, 6 * D // tn),
      in_specs=[
          pl.BlockSpec((N_SEG, D), lambda l, j: (0, 0)),
          pl.BlockSpec((None, D, tn), lambda l, j: (l, 0, j)),
          pl.BlockSpec((None, 1, tn), lambda l, j: (l, 0, j)),
      ],
      out_specs=pl.BlockSpec((None, N_SEG, tn), lambda l, j: (l, 0, j)),
      out_shape=jax.ShapeDtypeStruct((DEPTH, N_SEG, 6 * D), F32),
      compiler_params=_params(("arbitrary", "arbitrary")),
      name="adaln",
  )(cond8, w_ada, b_ada.reshape(DEPTH, 1, 6 * D))


def _nm_matmul_kernel(x_ref, g_ref, sh_ref, sc_ref, w_ref, o_ref, *, n_chunk):
  h = _norm_mod(x_ref[...], g_ref[...], sh_ref[...], sc_ref[...]).astype(BF16)
  n = w_ref.shape[1]
  for c in range(0, n, n_chunk):
    o_ref[:, c:c + n_chunk] = jnp.dot(
        h, w_ref[:, c:c + n_chunk], preferred_element_type=F32).astype(o_ref.dtype)


def _nm_matmul(x, g, shift, scale, w, tm, name):
  n = w.shape[1]
  return pl.pallas_call(
      functools.partial(_nm_matmul_kernel, n_chunk=512),
      grid=(T // tm,),
      in_specs=[
          pl.BlockSpec((tm, D), lambda i: (i, 0)),
          pl.BlockSpec((1, D), lambda i: (0, 0)),
          pl.BlockSpec((None, 1, D), lambda i: (_seg(i, tm), 0, 0)),
          pl.BlockSpec((None, 1, D), lambda i: (_seg(i, tm), 0, 0)),
          pl.BlockSpec((D, n), lambda i: (0, 0)),
      ],
      out_specs=pl.BlockSpec((tm, n), lambda i: (i, 0)),
      out_shape=jax.ShapeDtypeStruct((T, n), BF16),
      compiler_params=_params(("arbitrary",)),
      name=name,
  )(x, g, shift, scale, w)


def _mm_res_kernel(a_ref, w_ref, x_ref, gate_ref, o_ref):
  y = jnp.dot(a_ref[...], w_ref[...], preferred_element_type=F32)
  o_ref[...] = x_ref[...] + gate_ref[...] * y


def _mm_res(a, w, x, gate, name):
  k = a.shape[1]
  return pl.pallas_call(
      _mm_res_kernel,
      grid=(T // TM,),
      in_specs=[
          pl.BlockSpec((TM, k), lambda i: (i, 0)),
          pl.BlockSpec((k, D), lambda i: (0, 0)),
          pl.BlockSpec((TM, D), lambda i: (i, 0)),
          pl.BlockSpec((None, 1, D), lambda i: (_seg(i, TM), 0, 0)),
      ],
      out_specs=pl.BlockSpec((TM, D), lambda i: (i, 0)),
      out_shape=jax.ShapeDtypeStruct((T, D), F32),
      compiler_params=_params(("arbitrary",)),
      name=name,
  )(a, w, x, gate)


def _rope(x, c, a, b):
  return x * c + pltpu.roll(x, 96, 1) * a + pltpu.roll(x, 32, 1) * b


def _head_norm(x, w):
  return (x * lax.rsqrt(jnp.mean(x * x, axis=-1, keepdims=True) + EPS)) * w


def _attn_kernel(*refs, latent, tq):
  it = iter(refs)
  q_ref, kn_ref, vn_ref = next(it), next(it), next(it)
  if latent:
    ck_ref, cv_ref = next(it), next(it)
  qw_ref, kw_ref = next(it), next(it)
  if latent:
    cq_ref, aq_ref, bq_ref, ckk_ref, akk_ref, bkk_ref = [next(it) for _ in range(6)]
  o_ref = next(it)
  if not latent:
    ko_ref, vo_ref = next(it), next(it)
  k_s, v_s = next(it), next(it)
  n_cache = PAST if latent else 0

  @pl.when(pl.program_id(2) == 0)
  def _():
    k = _head_norm(kn_ref[...].astype(F32), kw_ref[...])
    if latent:
      k = _rope(k, ckk_ref[...], akk_ref[...], bkk_ref[...])
      k_s[0:PAST, :] = ck_ref[...].astype(BF16)
      v_s[0:PAST, :] = cv_ref[...].astype(BF16)
    else:
      ko_ref[...] = k
      vo_ref[...] = vn_ref[...].astype(F32)
    k_s[n_cache:, :] = k.astype(BF16)
    v_s[n_cache:, :] = vn_ref[...]

  qa = q_ref[...].astype(F32)
  qs = []
  for h in range(KV_GROUP):
    qh = _head_norm(qa[:, h * HEAD_DIM:(h + 1) * HEAD_DIM], qw_ref[...])
    if latent:
      qh = _rope(qh, cq_ref[...], aq_ref[...], bq_ref[...])
    qs.append(qh.astype(BF16))
  qc = jnp.concatenate(qs, axis=0)
  s = lax.dot_general(qc, k_s[...], (((1,), (1,)), ((), ())),
                      preferred_element_type=F32)
  m = jnp.max(s, axis=-1, keepdims=True)
  p = jnp.exp2((s - m) * (HEAD_DIM ** -0.5 * LOG2E))
  l = jnp.sum(p, axis=-1, keepdims=True)
  o = jnp.dot(p.astype(BF16), v_s[...], preferred_element_type=F32) / l
  for h in range(KV_GROUP):
    o_ref[:, h * HEAD_DIM:(h + 1) * HEAD_DIM] = o[h * tq:(h + 1) * tq].astype(BF16)


def _rope_tables():
  rows = SEQ_S // GRID_W
  t_row = jnp.broadcast_to(jnp.arange(rows)[:, None], (rows, GRID_W)).reshape(-1)
  t_col = jnp.broadcast_to(jnp.arange(GRID_W)[None, :], (rows, GRID_W)).reshape(-1)
  nf = HEAD_DIM // 4
  inv = ROPE_THETA ** (-jnp.arange(nf, dtype=F32) / nf)
  ang_r = t_row.astype(F32)[:, None] * inv[None, :]
  ang_c = t_col.astype(F32)[:, None] * inv[None, :]
  cr, sr, cc, sc = jnp.cos(ang_r), jnp.sin(ang_r), jnp.cos(ang_c), jnp.sin(ang_c)
  z = jnp.zeros_like(sr)
  c = jnp.concatenate([cr, cr, cc, cc], axis=1)
  a = jnp.concatenate([-sr, z, -sc, z], axis=1)
  b = jnp.concatenate([z, sr, z, sc], axis=1)
  return c, a, b


def _attn_prompt(qkv, q_norm, k_norm):
  kcol = N_HEADS * HEAD_DIM // HEAD_DIM
  vcol = kcol + N_KV
  out_shapes = (
      jax.ShapeDtypeStruct((T_P, N_HEADS * HEAD_DIM), BF16),
      jax.ShapeDtypeStruct((NB_P, SEQ_P, N_KV * HEAD_DIM), F32),
      jax.ShapeDtypeStruct((NB_P, SEQ_P, N_KV * HEAD_DIM), F32),
  )
  return pl.pallas_call(
      functools.partial(_attn_kernel, latent=False, tq=SEQ_P),
      grid=(NB_P, N_KV, 1),
      in_specs=[
          pl.BlockSpec((SEQ_P, KV_GROUP * HEAD_DIM), lambda b, h, i: (b, h)),
          pl.BlockSpec((SEQ_P, HEAD_DIM), lambda b, h, i: (b, kcol + h)),
          pl.BlockSpec((SEQ_P, HEAD_DIM), lambda b, h, i: (b, vcol + h)),
          pl.BlockSpec((1, HEAD_DIM), lambda b, h, i: (0, 0)),
          pl.BlockSpec((1, HEAD_DIM), lambda b, h, i: (0, 0)),
      ],
      out_specs=(
          pl.BlockSpec((SEQ_P, KV_GROUP * HEAD_DIM), lambda b, h, i: (b, h)),
          pl.BlockSpec((None, SEQ_P, HEAD_DIM), lambda b, h, i: (b, 0, h)),
          pl.BlockSpec((None, SEQ_P, HEAD_DIM), lambda b, h, i: (b, 0, h)),
      ),
      out_shape=out_shapes,
      scratch_shapes=[pltpu.VMEM((SEQ_P, HEAD_DIM), BF16), pltpu.VMEM((SEQ_P, HEAD_DIM), BF16)],
      compiler_params=_params(("arbitrary", "arbitrary", "arbitrary")),
      name="attn_prompt",
  )(qkv, qkv, qkv, q_norm, k_norm)


def _attn_latent(qkv, cache_k, cache_v, q_norm, k_norm, rope):
  kcol = N_HEADS
  vcol = kcol + N_KV
  nq = SEQ_S // TQ
  row0 = T_P // TQ
  seq0 = T_P // SEQ_S
  lk = PAST + SEQ_S
  c, a, b = rope
  tab_q = pl.BlockSpec((TQ, HEAD_DIM), lambda bb, h, i: (i, 0))
  tab_k = pl.BlockSpec((SEQ_S, HEAD_DIM), lambda bb, h, i: (0, 0))
  return pl.pallas_call(
      functools.partial(_attn_kernel, latent=True, tq=TQ),
      grid=(NB_S, N_KV, nq),
      in_specs=[
          pl.BlockSpec((TQ, KV_GROUP * HEAD_DIM), lambda bb, h, i: (row0 + bb * nq + i, h)),
          pl.BlockSpec((SEQ_S, HEAD_DIM), lambda bb, h, i: (seq0 + bb, kcol + h)),
          pl.BlockSpec((SEQ_S, HEAD_DIM), lambda bb, h, i: (seq0 + bb, vcol + h)),
          pl.BlockSpec((None, PAST, HEAD_DIM), lambda bb, h, i: (bb, 0, h)),
          pl.BlockSpec((None, PAST, HEAD_DIM), lambda bb, h, i: (bb, 0, h)),
          pl.BlockSpec((1, HEAD_DIM), lambda bb, h, i: (0, 0)),
          pl.BlockSpec((1, HEAD_DIM), lambda bb, h, i: (0, 0)),
          tab_q, tab_q, tab_q, tab_k, tab_k, tab_k,
      ],
      out_specs=pl.BlockSpec((TQ, KV_GROUP * HEAD_DIM), lambda bb, h, i: (bb * nq + i, h)),
      out_shape=jax.ShapeDtypeStruct((T_S, N_HEADS * HEAD_DIM), BF16),
      scratch_shapes=[pltpu.VMEM((lk, HEAD_DIM), BF16), pltpu.VMEM((lk, HEAD_DIM), BF16)],
      compiler_params=_params(("arbitrary", "arbitrary", "arbitrary")),
      name="attn_latent",
  )(qkv, qkv, qkv, cache_k, cache_v, q_norm, k_norm, c, a, b, c, a, b)


def _pool_kernel(x_ref, xp_ref, xn_ref, g_ref, sh_ref, sc_ref, gate_ref, w_ref, ps_ref, o_ref):
  t = pl.program_id(0)
  npt = T_P // TP
  tiles_s = SEQ_S // TP
  is_p = t < npt
  pos = jnp.where(is_p, 0, (t - npt) % tiles_s)
  ntile = jnp.where(is_p, SEQ_P // TP, tiles_s)
  seq_len = ntile * TP
  keep_prev = jnp.where(pos == 0, 0.0, 1.0)
  keep_next = jnp.where(pos == ntile - 1, 0.0, 1.0)

  g, sh, sc = g_ref[...], sh_ref[...], sc_ref[...]
  x = x_ref[...]
  h = _norm_mod(x, g, sh, sc)
  hp = _norm_mod(xp_ref[...], g, sh, sc) * keep_prev
  hn = _norm_mod(xn_ref[...], g, sh, sc) * keep_next
  ext = jnp.concatenate([hp, h, hn], axis=0)
  n_ext = TP + 2 * POOL_HALO
  tseq = pos * TP + lax.broadcasted_iota(jnp.int32, (TP, POOL_GROUP), 0)

  outs = []
  for gi, win in enumerate(POOL_WINDOWS):
    lo, hi = gi * POOL_GROUP, (gi + 1) * POOL_GROUP
    acc = ext[:, lo:hi]
    span = 1
    while span < win:
      acc = acc + pltpu.roll(acc, n_ext - span, 0)
      span *= 2
    start = POOL_HALO - win // 2
    if start:
      acc = pltpu.roll(acc, n_ext - start, 0)
    ssum = acc[0:TP]
    cnt = (jnp.minimum(tseq + win // 2, seq_len) - jnp.maximum(tseq - win // 2, 0)).astype(F32)
    dlt = (ssum / cnt - h[:, lo:hi]).astype(BF16)
    outs.append(jnp.dot(dlt, w_ref[gi], preferred_element_type=F32))
  y = jnp.concatenate(outs, axis=1) * ps_ref[...]
  o_ref[...] = x + gate_ref[...] * y


def _pool_layer(x, g, shift, scale, gate, w, pscale):
  hb = TP // POOL_HALO
  last = T // POOL_HALO - 1
  seg = lambda i: _seg(i, TP)
  return pl.pallas_call(
      _pool_kernel,
      grid=(T // TP,),
      in_specs=[
          pl.BlockSpec((TP, D), lambda i: (i, 0)),
          pl.BlockSpec((POOL_HALO, D), lambda i: (jnp.maximum(i * hb - 1, 0), 0)),
          pl.BlockSpec((POOL_HALO, D), lambda i: (jnp.minimum((i + 1) * hb, last), 0)),
          pl.BlockSpec((1, D), lambda i: (0, 0)),
          pl.BlockSpec((None, 1, D), lambda i: (seg(i), 0, 0)),
          pl.BlockSpec((None, 1, D), lambda i: (seg(i), 0, 0)),
          pl.BlockSpec((None, 1, D), lambda i: (seg(i), 0, 0)),
          pl.BlockSpec((4, POOL_GROUP, POOL_GROUP), lambda i: (0, 0, 0)),
          pl.BlockSpec((1, D), lambda i: (0, 0)),
      ],
      out_specs=pl.BlockSpec((TP, D), lambda i: (i, 0)),
      out_shape=jax.ShapeDtypeStruct((T, D), F32),
      compiler_params=_params(("arbitrary",)),
      name="pool",
  )(x, x, x, g, shift, scale, gate, w, pscale)


def _ret_kernel(*refs, seq_len, has_s0):
  it = iter(refs)
  q_ref, k_ref, v_ref, gate_ref, dl_ref, ng_ref = [next(it) for _ in range(6)]
  s0_ref = next(it) if has_s0 else None
  y_ref = next(it)
  so_ref = None if has_s0 else next(it)
  o_s, st_s = next(it), next(it)
  c = RET_CHUNK
  n = seq_len // c
  kscale = RET_DK ** -0.5

  dl = dl_ref[...]
  lg = -jnp.log1p(jnp.exp(-dl))
  ri = lax.broadcasted_iota(jnp.int32, (c, c), 0).astype(F32)
  ci = lax.broadcasted_iota(jnp.int32, (c, c), 1).astype(F32)
  ri2 = lax.broadcasted_iota(jnp.int32, (c, RET_DV), 0).astype(F32)

  def decays(lgd, backward):
    lg2 = jnp.concatenate([lgd, lgd], axis=1)
    if backward:
      intra = jnp.where(ri <= ci, jnp.exp((ci - ri) * lgd), 0.0) * kscale
      qd = jnp.exp((c - ri2) * lg2)
      kd = jnp.exp(ri * lgd) * kscale
    else:
      intra = jnp.where(ri >= ci, jnp.exp((ri - ci) * lgd), 0.0) * kscale
      qd = jnp.exp((ri2 + 1.0) * lg2)
      kd = jnp.exp((c - 1.0 - ri) * lgd) * kscale
    return intra, qd, kd, jnp.exp(c * lg2)

  def chunk(r0, dec):
    intra, qd, kd, cdec = dec
    qc = q_ref[pl.ds(r0, c), :]
    kc = k_ref[pl.ds(r0, c), :]
    vc = v_ref[pl.ds(r0, c), :]
    s = lax.dot_general(qc, kc, (((1,), (1,)), ((), ())), preferred_element_type=F32) * intra
    st = st_s[...]
    o = (jnp.dot(s.astype(BF16), vc, preferred_element_type=F32)
         + jnp.dot(qc, st.astype(BF16), preferred_element_type=F32) * qd)
    kdec = (kc.astype(F32) * kd).astype(BF16)
    st_s[...] = st * cdec + lax.dot_general(
        kdec, vc, (((0,), (0,)), ((), ())), preferred_element_type=F32)
    return o

  st_s[...] = s0_ref[0] if has_s0 else jnp.zeros((RET_DK, RET_DV), F32)

  dec_f = decays(lg[0], False)

  def fwd(j, carry):
    r0 = pl.multiple_of(j * c, c)
    o_s[pl.ds(r0, c), :] = chunk(r0, dec_f)
    return carry
  lax.fori_loop(0, n, fwd, 0)
  if not has_s0:
    so_ref[0] = st_s[...]

  st_s[...] = s0_ref[1] if has_s0 else jnp.zeros((RET_DK, RET_DV), F32)

  dec_b = decays(lg[1], True)

  def bwd(j, carry):
    r0 = pl.multiple_of((n - 1 - j) * c, c)
    o = o_s[pl.ds(r0, c), :] + chunk(r0, dec_b)
    mu = jnp.mean(o, axis=-1, keepdims=True)
    d = o - mu
    var = jnp.mean(d * d, axis=-1, keepdims=True)
    on = (d * lax.rsqrt(var + EPS)) * ng_ref[...]
    y_ref[pl.ds(r0, c), :] = (on * _silu(gate_ref[pl.ds(r0, c), :].astype(F32))).astype(BF16)
    return carry
  lax.fori_loop(0, n, bwd, 0)
  if not has_s0:
    so_ref[1] = st_s[...]


def _retention(p, dl, ng, state0, prompt):
  seq_len = SEQ_P if prompt else SEQ_S
  nb = NB_P if prompt else NB_S
  row0 = 0 if prompt else T_P // SEQ_S
  kcol = RET_HEADS
  vcol = (2 * RET_HEADS * RET_DK) // RET_DV
  gcol = vcol + RET_HEADS
  in_specs = [
      pl.BlockSpec((seq_len, RET_DK), lambda b, h: (row0 + b, h)),
      pl.BlockSpec((seq_len, RET_DK), lambda b, h: (row0 + b, kcol + h)),
      pl.BlockSpec((seq_len, RET_DV), lambda b, h: (row0 + b, vcol + h)),
      pl.BlockSpec((seq_len, RET_DV), lambda b, h: (row0 + b, gcol + h)),
      pl.BlockSpec((None, 2, 1, RET_DK), lambda b, h: (h, 0, 0, 0)),
      pl.BlockSpec((1, RET_DV), lambda b, h: (0, h)),
  ]
  args = [p, p, p, p, dl, ng]
  y_spec = pl.BlockSpec((seq_len, RET_DV), lambda b, h: (b, h))
  y_shape = jax.ShapeDtypeStruct((nb * seq_len, RET_HEADS * RET_DV), BF16)
  if prompt:
    out_specs = (y_spec, pl.BlockSpec((None, 2, None, RET_DK, RET_DV), lambda b, h: (b, 0, h, 0, 0)))
    out_shape = (y_shape, jax.ShapeDtypeStruct((NB_P, 2, RET_HEADS, RET_DK, RET_DV), F32))
  else:
    in_specs.append(pl.BlockSpec((None, 2, None, RET_DK, RET_DV), lambda b, h: (b, 0, h, 0, 0)))
    args.append(state0)
    out_specs = y_spec
    out_shape = y_shape
  return pl.pallas_call(
      functools.partial(_ret_kernel, seq_len=seq_len, has_s0=not prompt),
      grid=(nb, RET_HEADS),
      in_specs=in_specs,
      out_specs=out_specs,
      out_shape=out_shape,
      scratch_shapes=[pltpu.VMEM((seq_len, RET_DV), F32), pltpu.VMEM((RET_DK, RET_DV), F32)],
      compiler_params=_params(("arbitrary", "arbitrary")),
      name="ret_prompt" if prompt else "ret_latent",
  )(*args)


def _router_kernel(x_ref, g_ref, sh_ref, sc_ref, wrh_ref, wrl_ref, br_ref, tri_ref,
                   info_ref, cnt_ref, carry_s):
  @pl.when(pl.program_id(0) == 0)
  def _():
    carry_s[...] = jnp.zeros_like(carry_s)

  h = _norm_mod(x_ref[...], g_ref[...], sh_ref[...], sc_ref[...])
  logits = _router_logits(h, wrh_ref, wrl_ref, br_ref)
  lane = lax.broadcasted_iota(jnp.int32, logits.shape, 1)
  big = jnp.int32(127)

  def first_max(v):
    m = jnp.max(v, axis=-1, keepdims=True)
    idx = jnp.min(jnp.where(v == m, lane, big), axis=-1, keepdims=True)
    return m, idx

  lgm = jnp.where(lane < N_GROUPS, logits, NEG)
  _, gidx = first_max(lgm)
  lo = N_GROUPS + EPG * gidx
  le = jnp.where((lane >= lo) & (lane < lo + EPG), logits, NEG)
  _, i1 = first_max(le)
  _, i2 = first_max(jnp.where(lane == i1, NEG, le))
  e_lo = jnp.minimum(i1, i2) - N_GROUPS
  e_hi = jnp.maximum(i1, i2) - N_GROUPS
  a = e_lo - EPG * gidx
  b = e_hi - EPG * gidx
  pair_base = jnp.where(a == 0, 0, jnp.where(a == 1, 3, 5))
  cls = N_PAIRS * gidx + pair_base + (b - a - 1)

  onehot = jnp.where(lane == cls, 1.0, 0.0)
  before = jnp.dot(tri_ref[...], onehot.astype(BF16), preferred_element_type=F32) + carry_s[...]
  rank = jnp.sum(jnp.where(lane == cls, before, 0.0), axis=-1, keepdims=True)
  carry_s[...] = carry_s[...] + jnp.sum(onehot, axis=0, keepdims=True)
  cnt_ref[...] = carry_s[...]

  info_ref[...] = jnp.where(lane == 0, cls.astype(F32), jnp.where(lane == 1, rank, 0.0))


def _router(x, g, shift, scale, wr_hi, wr_lo, br, tri):
  return pl.pallas_call(
      _router_kernel,
      grid=(T // TM,),
      in_specs=[
          pl.BlockSpec((TM, D), lambda i: (i, 0)),
          pl.BlockSpec((1, D), lambda i: (0, 0)),
          pl.BlockSpec((None, 1, D), lambda i: (_seg(i, TM), 0, 0)),
          pl.BlockSpec((None, 1, D), lambda i: (_seg(i, TM), 0, 0)),
          pl.BlockSpec((D, 128), lambda i: (0, 0)),
          pl.BlockSpec((D, 128), lambda i: (0, 0)),
          pl.BlockSpec((1, 128), lambda i: (0, 0)),
          pl.BlockSpec((TM, TM), lambda i: (0, 0)),
      ],
      out_specs=(
          pl.BlockSpec((TM, 128), lambda i: (i, 0)),
          pl.BlockSpec((1, 128), lambda i: (0, 0)),
      ),
      out_shape=(
          jax.ShapeDtypeStruct((T, 128), F32),
          jax.ShapeDtypeStruct((1, 128), F32),
      ),
      scratch_shapes=[pltpu.VMEM((1, 128), F32)],
      compiler_params=_params(("arbitrary",)),
      name="router",
  )(x, g, shift, scale, wr_hi, wr_lo, br, tri)


def _tile_rows(ref, row):
  return ref.at[pl.ds(pl.multiple_of(row * ROW_SUB, ROW_SUB), ROW_SUB), :]


def _scatter_copy(slot_ref, step, r, stage, buf, hs_hbm, sem):
  return pltpu.make_async_copy(
      stage.at[buf, pl.ds(r * ROW_SUB, ROW_SUB), :],
      _tile_rows(hs_hbm, slot_ref[step * TS + r]), sem.at[buf])


def _scatter_kernel(slot_ref, pstart_ref, pn_ref, nv_ref,
                    x_ref, g_ref, sh_ref, sc_ref, hs_hbm,
                    stage, zero_s, sem, zsem):
  i = pl.program_id(0)
  n = pl.num_programs(0)
  buf = i % 2

  def pad_copy(c, r):
    return pltpu.make_async_copy(zero_s, _tile_rows(hs_hbm, pstart_ref[c] + r), zsem.at[0])

  def tail_copy(t):
    rows = TMM * ROW_SUB
    return pltpu.make_async_copy(
        stage.at[1], hs_hbm.at[pl.ds(pl.multiple_of(t * rows, rows), rows), :], zsem.at[1])

  @pl.when(i == 0)
  def _():
    stage[...] = jnp.zeros_like(stage)
    zero_s[...] = jnp.zeros_like(zero_s)
    for c in range(N_CLASSES):
      def start(r, carry, c=c):
        pad_copy(c, r).start()
        return carry
      lax.fori_loop(0, pn_ref[c], start, 0)

    def tail_start(t, carry):
      tail_copy(t).start()
      return carry
    lax.fori_loop(nv_ref[0], NT_E, tail_start, 0)
    for c in range(N_CLASSES):
      def wait(r, carry, c=c):
        pad_copy(c, r).wait()
        return carry
      lax.fori_loop(0, pn_ref[c], wait, 0)

    def tail_wait(t, carry):
      tail_copy(t).wait()
      return carry
    lax.fori_loop(nv_ref[0], NT_E, tail_wait, 0)

  h = _norm_mod(x_ref[...], g_ref[...], sh_ref[...], sc_ref[...])
  for s in range(ROW_SUB):
    stage[buf, pl.ds(s, TS, stride=ROW_SUB), :] = h[:, s * 128:(s + 1) * 128]

  def start(r, carry):
    _scatter_copy(slot_ref, i, r, stage, buf, hs_hbm, sem).start()
    return carry
  lax.fori_loop(0, TS, start, 0, unroll=8)

  @pl.when(i > 0)
  def _():
    def wait(r, carry):
      _scatter_copy(slot_ref, i - 1, r, stage, 1 - buf, hs_hbm, sem).wait()
      return carry
    lax.fori_loop(0, TS, wait, 0, unroll=8)

  @pl.when(i == n - 1)
  def _():
    def wait(r, carry):
      _scatter_copy(slot_ref, i, r, stage, buf, hs_hbm, sem).wait()
      return carry
    lax.fori_loop(0, TS, wait, 0, unroll=8)


def _scatter_rows(slot, pad_start, pad_n, n_valid, x, g, shift, scale):
  assert TS == TMM
  grid_spec = pltpu.PrefetchScalarGridSpec(
      num_scalar_prefetch=4,
      grid=(T // TS,),
      in_specs=[
          pl.BlockSpec((TS, D), lambda i, *_: (i, 0)),
          pl.BlockSpec((1, D), lambda i, *_: (0, 0)),
          pl.BlockSpec((None, 1, D), lambda i, *_: (_seg(i, TS), 0, 0)),
          pl.BlockSpec((None, 1, D), lambda i, *_: (_seg(i, TS), 0, 0)),
      ],
      out_specs=pl.BlockSpec(memory_space=pl.ANY),
      scratch_shapes=[
          pltpu.VMEM((2, TS * ROW_SUB, 128), F32),
          pltpu.VMEM((ROW_SUB, 128), F32),
          pltpu.SemaphoreType.DMA((2,)),
          pltpu.SemaphoreType.DMA((2,)),
      ],
  )
  return pl.pallas_call(
      _scatter_kernel,
      grid_spec=grid_spec,
      out_shape=jax.ShapeDtypeStruct((P_PAD * ROW_SUB, 128), F32),
      compiler_params=_params(("arbitrary",)),
      name="scatter_rows",
  )(slot, pad_start, pad_n, n_valid, x, g, shift, scale)


def _router_logits(h, wrh_ref, wrl_ref, br_ref):
  h_hi = h.astype(BF16)
  h_lo = (h - h_hi.astype(F32)).astype(BF16)
  dot = functools.partial(jnp.dot, preferred_element_type=F32)
  return (dot(h_hi, wrh_ref[...]) + dot(h_lo, wrh_ref[...]) + dot(h_hi, wrl_ref[...])
          + br_ref[...])


def _expert_kernel(elo_ref, ehi_ref, nv_ref, hs_ref, wrh_ref, wrl_ref, br_ref,
                   wg_lo, wg_hi, wu_lo, wu_hi, wd_lo, wd_hi, y_ref):
  i = pl.program_id(0)

  @pl.when(i < nv_ref[0])
  def _():
    h = jnp.concatenate(
        [hs_ref[pl.ds(s, TMM, stride=ROW_SUB), :] for s in range(ROW_SUB)], axis=1)
    hb = h.astype(BF16)
    logits = _router_logits(h, wrh_ref, wrl_ref, br_ref)
    lane = lax.broadcasted_iota(jnp.int32, logits.shape, 1)
    elo, ehi = elo_ref[i], ehi_ref[i]

    def pick(idx):
      return jnp.sum(jnp.where(lane == idx, logits, 0.0), axis=-1, keepdims=True)
    l_g, l_lo, l_hi = pick(elo // EPG), pick(N_GROUPS + elo), pick(N_GROUPS + ehi)
    p_top = 1.0 / jnp.sum(jnp.where(lane < N_GROUPS, jnp.exp(logits - l_g), 0.0),
                          axis=-1, keepdims=True)
    wl = jnp.broadcast_to(p_top / (1.0 + jnp.exp(l_hi - l_lo)), (TMM, D_EXPERT))
    wh = jnp.broadcast_to(p_top / (1.0 + jnp.exp(l_lo - l_hi)), (TMM, D_EXPERT))
    dot = functools.partial(jnp.dot, preferred_element_type=F32)
    a_lo = (_silu(dot(hb, wg_lo[...])) * dot(hb, wu_lo[...])) * wl
    a_hi = (_silu(dot(hb, wg_hi[...])) * dot(hb, wu_hi[...])) * wh
    y = dot(a_lo.astype(BF16), wd_lo[...]) + dot(a_hi.astype(BF16), wd_hi[...])
    for s in range(ROW_SUB):
      y_ref[pl.ds(s, TMM, stride=ROW_SUB), :] = y[:, s * 128:(s + 1) * 128]

  @pl.when(i >= nv_ref[0])
  def _():
    y_ref[...] = jnp.zeros_like(y_ref)


def _experts(tile_elo, tile_ehi, n_valid, hs, wr_hi, wr_lo, br, w_gate, w_up, w_down):
  up_spec_lo = pl.BlockSpec((None, D, D_EXPERT), lambda i, elo, ehi, nv: (elo[i], 0, 0))
  up_spec_hi = pl.BlockSpec((None, D, D_EXPERT), lambda i, elo, ehi, nv: (ehi[i], 0, 0))
  dn_spec_lo = pl.BlockSpec((None, D_EXPERT, D), lambda i, elo, ehi, nv: (elo[i], 0, 0))
  dn_spec_hi = pl.BlockSpec((None, D_EXPERT, D), lambda i, elo, ehi, nv: (ehi[i], 0, 0))
  grid_spec = pltpu.PrefetchScalarGridSpec(
      num_scalar_prefetch=3,
      grid=(NT_E,),
      in_specs=[
          pl.BlockSpec((TMM * ROW_SUB, 128), lambda i, elo, ehi, nv: (i, 0)),
          pl.BlockSpec((D, 128), lambda i, elo, ehi, nv: (0, 0)),
          pl.BlockSpec((D, 128), lambda i, elo, ehi, nv: (0, 0)),
          pl.BlockSpec((1, 128), lambda i, elo, ehi, nv: (0, 0)),
          up_spec_lo, up_spec_hi, up_spec_lo, up_spec_hi, dn_spec_lo, dn_spec_hi,
      ],
      out_specs=pl.BlockSpec((TMM * ROW_SUB, 128), lambda i, elo, ehi, nv: (i, 0)),
  )
  return pl.pallas_call(
      _expert_kernel,
      grid_spec=grid_spec,
      out_shape=jax.ShapeDtypeStruct((P_PAD * ROW_SUB, 128), F32),
      compiler_params=_params(("arbitrary",)),
      name="experts",
  )(tile_elo, tile_ehi, n_valid, hs, wr_hi, wr_lo, br,
    w_gate, w_gate, w_up, w_up, w_down, w_down)


def _gather_copy(slot_ref, step, r, y_hbm, ybuf, buf, sem):
  return pltpu.make_async_copy(
      _tile_rows(y_hbm, slot_ref[step * TC + r]),
      ybuf.at[buf, pl.ds(r * ROW_SUB, ROW_SUB), :], sem.at[buf])


def _combine_kernel(slot_ref, x_ref, gate_ref, nf_ref, y_hbm, o_ref, ybuf, sem, *, final):
  i = pl.program_id(0)
  n = pl.num_programs(0)
  buf = i % 2

  def start(step, b):
    def body(r, carry):
      _gather_copy(slot_ref, step, r, y_hbm, ybuf, b, sem).start()
      return carry
    lax.fori_loop(0, TC, body, 0, unroll=8)

  @pl.when(i == 0)
  def _():
    start(0, 0)

  @pl.when(i + 1 < n)
  def _():
    start(i + 1, 1 - buf)

  def wait(r, carry):
    _gather_copy(slot_ref, i, r, y_hbm, ybuf, buf, sem).wait()
    return carry
  lax.fori_loop(0, TC, wait, 0, unroll=8)

  y = jnp.concatenate(
      [ybuf[buf, pl.ds(s, TC, stride=ROW_SUB), :] for s in range(ROW_SUB)], axis=1)
  x = x_ref[...] + gate_ref[...] * y
  if final:
    x = (x * lax.rsqrt(jnp.mean(x * x, axis=-1, keepdims=True) + EPS)) * nf_ref[...]
  o_ref[...] = x


def _combine(slot, x, gate, norm_f, y_sorted, final):
  grid_spec = pltpu.PrefetchScalarGridSpec(
      num_scalar_prefetch=1,
      grid=(T // TC,),
      in_specs=[
          pl.BlockSpec((TC, D), lambda i, s: (i, 0)),
          pl.BlockSpec((None, 1, D), lambda i, s: (_seg(i, TC), 0, 0)),
          pl.BlockSpec((1, D), lambda i, s: (0, 0)),
          pl.BlockSpec(memory_space=pl.ANY),
      ],
      out_specs=pl.BlockSpec((TC, D), lambda i, s: (i, 0)),
      scratch_shapes=[pltpu.VMEM((2, TC * ROW_SUB, 128), F32), pltpu.SemaphoreType.DMA((2,))],
  )
  return pl.pallas_call(
      functools.partial(_combine_kernel, final=final),
      grid_spec=grid_spec,
      out_shape=jax.ShapeDtypeStruct((T, D), F32),
      compiler_params=_params(("arbitrary",)),
      name="combine_final" if final else "combine",
  )(slot, x, gate, norm_f, y_sorted)


def _class_experts():
  lo, hi = [], []
  for g in range(N_GROUPS):
    for a in range(EPG):
      for b in range(a + 1, EPG):
        lo.append(g * EPG + a)
        hi.append(g * EPG + b)
  return np.asarray(lo, np.int32), np.asarray(hi, np.int32)


def _moe_layer(x, g, shift, scale, gate, wr_hi, wr_lo, br, tri, w_gate, w_up, w_down, norm_f, final):
  info, cnt = _router(x, g, shift, scale, wr_hi, wr_lo, br, tri)

  cls = info[:, 0].astype(jnp.int32)
  rank = info[:, 1].astype(jnp.int32)
  counts = cnt[0, :N_CLASSES].astype(jnp.int32)
  tiles = (counts + TMM - 1) // TMM
  tile_end = jnp.cumsum(tiles)
  offs = (tile_end - tiles) * TMM
  slot = offs[cls] + rank
  n_valid = tile_end[-1]
  tile_ids = jnp.minimum(jnp.arange(NT_E, dtype=jnp.int32), n_valid - 1)
  tile_cls = jnp.sum((tile_ids[:, None] >= tile_end[None, :]).astype(jnp.int32), axis=1)
  tile_cls = jnp.minimum(tile_cls, N_CLASSES - 1)
  cls_lo, cls_hi = _class_experts()
  tile_elo = jnp.asarray(cls_lo)[tile_cls]
  tile_ehi = jnp.asarray(cls_hi)[tile_cls]

  n_valid = n_valid.reshape(1)
  hs = _scatter_rows(slot, offs + counts, tiles * TMM - counts, n_valid, x, g, shift, scale)
  y_sorted = _experts(tile_elo, tile_ehi, n_valid, hs, wr_hi, wr_lo, br, w_gate, w_up, w_down)
  return _combine(slot, x, gate, norm_f, y_sorted, final)


def kernel(x_prompt, x_sample, cache_k, cache_v, state_ret, c, c_ctx, norm1, norm2, w_ada, b_ada, attn_w_qkv, attn_q_norm, attn_k_norm, attn_w_o, pool_w, pool_scale, ret_w_in, ret_decay_logit, ret_norm, ret_w_out, moe_w_router_g, moe_b_router_g, moe_w_router_e, moe_b_router_e, moe_w_gate, moe_w_up, moe_w_down, norm_f):
  x = jnp.concatenate([x_prompt.reshape(T_P, D), x_sample.reshape(T_S, D)], axis=0)
  cond8 = jnp.concatenate([c_ctx[None, :], c, jnp.zeros((N_SEG - 1 - NB_S, D), F32)], axis=0)
  mods = _adaln(cond8, w_ada, b_ada)
  mods = mods.reshape(DEPTH, N_SEG, 6, 1, D).transpose(0, 2, 1, 3, 4)
  rope = _rope_tables()
  tri = jnp.tril(jnp.ones((TM, TM), BF16), -1)
  pad_r = 128 - N_GROUPS - N_EXPERTS
  norm_f2 = norm_f.reshape(1, D)

  new_k, new_v, new_s = [], [], []
  for i in range(DEPTH):
    kind, j = i % 3, i // 3
    m = mods[i]
    g1 = norm1[i].reshape(1, D)
    g2 = norm2[i].reshape(1, D)
    if kind == 0:
      qkv = _nm_matmul(x, g1, m[0], m[1], attn_w_qkv[j].astype(BF16), TM, "qkv_proj")
      qn = attn_q_norm[j].reshape(1, HEAD_DIM)
      kn = attn_k_norm[j].reshape(1, HEAD_DIM)
      o_p, kc, vc = _attn_prompt(qkv, qn, kn)
      ck = cache_k[:, j].reshape(NB_S, PAST, N_KV * HEAD_DIM)
      cv = cache_v[:, j].reshape(NB_S, PAST, N_KV * HEAD_DIM)
      o_s = _attn_latent(qkv, ck, cv, qn, kn, rope)
      new_k.append(kc.reshape(NB_P, SEQ_P, N_KV, HEAD_DIM))
      new_v.append(vc.reshape(NB_P, SEQ_P, N_KV, HEAD_DIM))
      x = _mm_res(jnp.concatenate([o_p, o_s], axis=0), attn_w_o[j].astype(BF16), x, m[2], "attn_out")
    elif kind == 1:
      x = _pool_layer(x, g1, m[0], m[1], m[2], pool_w[j].astype(BF16), pool_scale[j].reshape(1, D))
    else:
      p = _nm_matmul(x, g1, m[0], m[1], ret_w_in[j].astype(BF16), 256, "ret_proj")
      dl = jnp.broadcast_to(ret_decay_logit[j].T[:, :, None, None], (RET_HEADS, 2, 1, RET_DK))
      ng = ret_norm[j].reshape(1, RET_HEADS * RET_DV)
      y_p, s_new = _retention(p, dl, ng, None, True)
      y_s = _retention(p, dl, ng, state_ret[:, j], False)
      new_s.append(s_new)
      x = _mm_res(jnp.concatenate([y_p, y_s], axis=0), ret_w_out[j].astype(BF16), x, m[2], "ret_out")
    wr = jnp.concatenate([moe_w_router_g[i], moe_w_router_e[i], jnp.zeros((D, pad_r), F32)], axis=1)
    br = jnp.concatenate([moe_b_router_g[i], moe_b_router_e[i], jnp.zeros((pad_r,), F32)]).reshape(1, 128)
    wr_hi = wr.astype(BF16)
    wr_lo = (wr - wr_hi.astype(F32)).astype(BF16)
    x = _moe_layer(x, g2, m[3], m[4], m[5], wr_hi, wr_lo, br, tri,
                   moe_w_gate[i].astype(BF16), moe_w_up[i].astype(BF16), moe_w_down[i].astype(BF16),
                   norm_f2, i == DEPTH - 1)

  y_prompt = x[:T_P].reshape(NB_P, SEQ_P, D)
  y_sample = x[T_P:].reshape(NB_S, SEQ_S, D)
  new_cache_k = jnp.stack(new_k, axis=1)
  new_cache_v = jnp.stack(new_v, axis=1)
  new_state_ret = jnp.stack(new_s, axis=1)
  return (y_prompt, y_sample, new_cache_k, new_cache_v, new_state_ret)
```

```python
import functools

import jax
import jax.numpy as jnp
import numpy as np
from jax import lax
from jax.experimental import pallas as pl
from jax.experimental.pallas import tpu as pltpu

F32 = jnp.float32
BF16 = jnp.bfloat16

D = 1024
NB_P, SEQ_P = 32, 256
NB_S, SEQ_S = 4, 2048
T_P = NB_P * SEQ_P
T_S = NB_S * SEQ_S
T = T_P + T_S
DEPTH = 4
GRID_W = 64
HEAD_DIM = 128
N_HEADS = 8
N_KV = 2
KV_GROUP = N_HEADS // N_KV
PAST = 256
ROPE_THETA = 10000.0
POOL_WINDOWS = (2, 4, 8, 16)
POOL_GROUP = D // 4
POOL_HALO = 8
RET_HEADS = 8
RET_DK = 128
RET_DV = 256
RET_CHUNK = 128
RET_HEADS_PER_STEP = 2
N_GROUPS = 4
EPG = 4
N_EXPERTS = 16
D_EXPERT = 256
N_PAIRS = 6
N_CLASSES = N_GROUPS * N_PAIRS
EPS = 1e-6
LOG2E = 1.4426950408889634
NEG = -1e30
N_SEG = 8

VMEM_LIMIT_BYTES = 52 * 1024 * 1024

TM = 512
TQ = 128
ATTN_KEY_CHUNK = 256
ATTN_HEAD_STACK = 2
TP = 256
TMM = 256
TS = 256
TC = 256
ROW_SUB = 8
PACK_SUB = D // 2 // 128
INFO_SUB = PACK_SUB
P_PAD = T + N_CLASSES * TMM
NT_E = P_PAD // TMM


def _params(sem):
  return pltpu.CompilerParams(dimension_semantics=sem, vmem_limit_bytes=VMEM_LIMIT_BYTES)


def _seg(i, tm):
  npt = T_P // tm
  return jnp.where(i < npt, 0, (i - npt) // (SEQ_S // tm) + 1)


def _norm_mod(x, g, shift, scale):
  r = lax.rsqrt(jnp.mean(x * x, axis=-1, keepdims=True) + EPS)
  return ((x * r) * g) * (1.0 + scale) + shift


def _silu(x):
  return x * (1.0 / (1.0 + jnp.exp(-x)))


def _adaln_kernel(c_ref, w_ref, b_ref, o_ref):
  s = _silu(c_ref[...]).astype(BF16)
  o_ref[...] = jnp.dot(s, w_ref[...].astype(BF16), preferred_element_type=F32) + b_ref[...]


def _adaln(cond8, w_ada, b_ada):
  tn = 1536
  return pl.pallas_call(
      _adaln_kernel,
      grid=(DEPTH, 6 * D // tn),
      in_specs=[
          pl.BlockSpec((N_SEG, D), lambda l, j: (0, 0)),
          pl.BlockSpec((None, D, tn), lambda l, j: (l, 0, j)),
          pl.BlockSpec((None, 1, tn), lambda l, j: (l, 0, j)),
      ],
      out_specs=pl.BlockSpec((None, N_SEG, tn), lambda l, j: (l, 0, j)),
      out_shape=jax.ShapeDtypeStruct((DEPTH, N_SEG, 6 * D), F32),
      compiler_params=_params(("arbitrary", "arbitrary")),
      name="adaln",
  )(cond8, w_ada, b_ada.reshape(DEPTH, 1, 6 * D))


def _nm_matmul_kernel(x_ref, g_ref, sh_ref, sc_ref, w_ref, o_ref, *, n_chunk):
  h = _norm_mod(x_ref[...], g_ref[...], sh_ref[...], sc_ref[...]).astype(BF16)
  n = w_ref.shape[1]
  for c in range(0, n, n_chunk):
    o_ref[:, c:c + n_chunk] = jnp.dot(
        h, w_ref[:, c:c + n_chunk], preferred_element_type=F32).astype(o_ref.dtype)


def _nm_matmul(x, g, shift, scale, w, tm, name):
  n = w.shape[1]
  return pl.pallas_call(
      functools.partial(_nm_matmul_kernel, n_chunk=512),
      grid=(T // tm,),
      in_specs=[
          pl.BlockSpec((tm, D), lambda i: (i, 0)),
          pl.BlockSpec((1, D), lambda i: (0, 0)),
          pl.BlockSpec((None, 1, D), lambda i: (_seg(i, tm), 0, 0)),
          pl.BlockSpec((None, 1, D), lambda i: (_seg(i, tm), 0, 0)),
          pl.BlockSpec((D, n), lambda i: (0, 0)),
      ],
      out_specs=pl.BlockSpec((tm, n), lambda i: (i, 0)),
      out_shape=jax.ShapeDtypeStruct((T, n), BF16),
      compiler_params=_params(("arbitrary",)),
      name=name,
  )(x, g, shift, scale, w)


def _mm_res_kernel(ap_ref, as_ref, w_ref, x_ref, gate_ref, o_ref):
  def run(a_ref):
    y = jnp.dot(a_ref[...], w_ref[...], preferred_element_type=F32)
    o_ref[...] = x_ref[...] + gate_ref[...] * y

  is_prompt = pl.program_id(0) < T_P // TM
  pl.when(is_prompt)(lambda: run(ap_ref))
  pl.when(jnp.logical_not(is_prompt))(lambda: run(as_ref))


def _mm_res(a_p, a_s, w, x, gate, name):
  k = a_p.shape[1]
  npt = T_P // TM
  return pl.pallas_call(
      _mm_res_kernel,
      grid=(T // TM,),
      in_specs=[
          pl.BlockSpec((TM, k), lambda i: (jnp.minimum(i, npt - 1), 0)),
          pl.BlockSpec((TM, k), lambda i: (jnp.maximum(i - npt, 0), 0)),
          pl.BlockSpec((k, D), lambda i: (0, 0)),
          pl.BlockSpec((TM, D), lambda i: (i, 0)),
          pl.BlockSpec((None, 1, D), lambda i: (_seg(i, TM), 0, 0)),
      ],
      out_specs=pl.BlockSpec((TM, D), lambda i: (i, 0)),
      out_shape=jax.ShapeDtypeStruct((T, D), F32),
      compiler_params=_params(("arbitrary",)),
      name=name,
  )(a_p, a_s, w, x, gate)


def _rope(x, c, a, b):
  return x * c + pltpu.roll(x, 96, 1) * a + pltpu.roll(x, 32, 1) * b


def _head_norm(x, w):
  return (x * lax.rsqrt(jnp.mean(x * x, axis=-1, keepdims=True) + EPS)) * w


def _attn_kernel(*refs, latent, tq):
  it = iter(refs)
  q_ref, kn_ref, vn_ref = next(it), next(it), next(it)
  if latent:
    ck_ref, cv_ref = next(it), next(it)
  qw_ref, kw_ref = next(it), next(it)
  if latent:
    cq_ref, aq_ref, bq_ref, ckk_ref, akk_ref, bkk_ref = [next(it) for _ in range(6)]
  o_ref = next(it)
  if not latent:
    ko_ref, vo_ref = next(it), next(it)
  k_s, vt_s = next(it), next(it)
  st_refs = [next(it) for _ in range(KV_GROUP // ATTN_HEAD_STACK)]
  n_cache = PAST if latent else 0

  @pl.when(pl.program_id(2) == 0)
  def _():
    k = _head_norm(kn_ref[...].astype(F32), kw_ref[...])
    if latent:
      k = _rope(k, ckk_ref[...], akk_ref[...], bkk_ref[...])
      k_s[0:PAST, :] = ck_ref[...].astype(BF16)
    else:
      ko_ref[...] = k
      vo_ref[...] = vn_ref[...].astype(F32)
    k_s[n_cache:, :] = k.astype(BF16)
    vt_s[:, n_cache:] = vn_ref[...].astype(F32).T.astype(BF16)
    if latent:
      vt_s[:, 0:PAST] = cv_ref[...].T.astype(BF16)

  qa = q_ref[...].astype(F32)
  lk = k_s.shape[0]
  def prep_q(h):
    qh = _head_norm(qa[:, h * HEAD_DIM:(h + 1) * HEAD_DIM], qw_ref[...])
    if latent:
      qh = _rope(qh, cq_ref[...], aq_ref[...], bq_ref[...])
    return (qh * (HEAD_DIM ** -0.5 * LOG2E)).astype(BF16)

  nq = ATTN_HEAD_STACK * tq
  for h0 in range(0, KV_GROUP, ATTN_HEAD_STACK):
    qp = jnp.concatenate([prep_q(h0 + d) for d in range(ATTN_HEAD_STACK)], axis=0)
    st_s = st_refs[h0 // ATTN_HEAD_STACK]
    chunks = [(c0, min(c0 + ATTN_KEY_CHUNK, lk)) for c0 in range(0, lk, ATTN_KEY_CHUNK)]
    m = jnp.full((1, nq), NEG, F32)
    for c0, c1 in chunks:
      st = lax.dot_general(k_s[c0:c1, :], qp, (((1,), (1,)), ((), ())),
                           preferred_element_type=F32)
      st_s[c0:c1, :] = st
      m = jnp.maximum(m, jnp.max(st, axis=0, keepdims=True))
    l = jnp.zeros((1, nq), F32)
    acc = jnp.zeros((HEAD_DIM, nq), F32)
    for c0, c1 in chunks:
      pt = jnp.exp2(st_s[c0:c1, :] - m)
      l = l + jnp.sum(pt, axis=0, keepdims=True)
      acc = acc + jnp.dot(vt_s[:, c0:c1], pt.astype(BF16), preferred_element_type=F32)
    o = acc / l
    for d in range(ATTN_HEAD_STACK):
      h = h0 + d
      o_ref[:, h * HEAD_DIM:(h + 1) * HEAD_DIM] = o[:, d * tq:(d + 1) * tq].T.astype(BF16)


def _attn_scratch(lk):
  stacks = KV_GROUP // ATTN_HEAD_STACK
  return ([pltpu.VMEM((lk, HEAD_DIM), BF16), pltpu.VMEM((HEAD_DIM, lk), BF16)]
          + [pltpu.VMEM((lk, ATTN_HEAD_STACK * TQ), F32) for _ in range(stacks)])


def _rope_tables():
  rows = SEQ_S // GRID_W
  t_row = jnp.broadcast_to(jnp.arange(rows)[:, None], (rows, GRID_W)).reshape(-1)
  t_col = jnp.broadcast_to(jnp.arange(GRID_W)[None, :], (rows, GRID_W)).reshape(-1)
  nf = HEAD_DIM // 4
  inv = ROPE_THETA ** (-jnp.arange(nf, dtype=F32) / nf)
  ang_r = t_row.astype(F32)[:, None] * inv[None, :]
  ang_c = t_col.astype(F32)[:, None] * inv[None, :]
  cr, sr, cc, sc = jnp.cos(ang_r), jnp.sin(ang_r), jnp.cos(ang_c), jnp.sin(ang_c)
  z = jnp.zeros_like(sr)
  c = jnp.concatenate([cr, cr, cc, cc], axis=1)
  a = jnp.concatenate([-sr, z, -sc, z], axis=1)
  b = jnp.concatenate([z, sr, z, sc], axis=1)
  return c, a, b


def _attn_prompt(qkv, q_norm, k_norm):
  kcol = N_HEADS * HEAD_DIM // HEAD_DIM
  vcol = kcol + N_KV
  out_shapes = (
      jax.ShapeDtypeStruct((T_P, N_HEADS * HEAD_DIM), BF16),
      jax.ShapeDtypeStruct((NB_P, SEQ_P, N_KV * HEAD_DIM), F32),
      jax.ShapeDtypeStruct((NB_P, SEQ_P, N_KV * HEAD_DIM), F32),
  )
  nq = SEQ_P // TQ
  return pl.pallas_call(
      functools.partial(_attn_kernel, latent=False, tq=TQ),
      grid=(NB_P, N_KV, nq),
      in_specs=[
          pl.BlockSpec((TQ, KV_GROUP * HEAD_DIM), lambda b, h, i: (b * nq + i, h)),
          pl.BlockSpec((SEQ_P, HEAD_DIM), lambda b, h, i: (b, kcol + h)),
          pl.BlockSpec((SEQ_P, HEAD_DIM), lambda b, h, i: (b, vcol + h)),
          pl.BlockSpec((1, HEAD_DIM), lambda b, h, i: (0, 0)),
          pl.BlockSpec((1, HEAD_DIM), lambda b, h, i: (0, 0)),
      ],
      out_specs=(
          pl.BlockSpec((TQ, KV_GROUP * HEAD_DIM), lambda b, h, i: (b * nq + i, h)),
          pl.BlockSpec((None, SEQ_P, HEAD_DIM), lambda b, h, i: (b, 0, h)),
          pl.BlockSpec((None, SEQ_P, HEAD_DIM), lambda b, h, i: (b, 0, h)),
      ),
      out_shape=out_shapes,
      scratch_shapes=_attn_scratch(SEQ_P),
      compiler_params=_params(("arbitrary", "arbitrary", "arbitrary")),
      name="attn_prompt",
  )(qkv, qkv, qkv, q_norm, k_norm)


def _attn_latent(qkv, cache_k, cache_v, q_norm, k_norm, rope):
  kcol = N_HEADS
  vcol = kcol + N_KV
  nq = SEQ_S // TQ
  row0 = T_P // TQ
  seq0 = T_P // SEQ_S
  lk = PAST + SEQ_S
  c, a, b = rope
  tab_q = pl.BlockSpec((TQ, HEAD_DIM), lambda bb, h, i: (i, 0))
  tab_k = pl.BlockSpec((SEQ_S, HEAD_DIM), lambda bb, h, i: (0, 0))
  return pl.pallas_call(
      functools.partial(_attn_kernel, latent=True, tq=TQ),
      grid=(NB_S, N_KV, nq),
      in_specs=[
          pl.BlockSpec((TQ, KV_GROUP * HEAD_DIM), lambda bb, h, i: (row0 + bb * nq + i, h)),
          pl.BlockSpec((SEQ_S, HEAD_DIM), lambda bb, h, i: (seq0 + bb, kcol + h)),
          pl.BlockSpec((SEQ_S, HEAD_DIM), lambda bb, h, i: (seq0 + bb, vcol + h)),
          pl.BlockSpec((None, PAST, HEAD_DIM), lambda bb, h, i: (bb, 0, h)),
          pl.BlockSpec((None, PAST, HEAD_DIM), lambda bb, h, i: (bb, 0, h)),
          pl.BlockSpec((1, HEAD_DIM), lambda bb, h, i: (0, 0)),
          pl.BlockSpec((1, HEAD_DIM), lambda bb, h, i: (0, 0)),
          tab_q, tab_q, tab_q, tab_k, tab_k, tab_k,
      ],
      out_specs=pl.BlockSpec((TQ, KV_GROUP * HEAD_DIM), lambda bb, h, i: (bb * nq + i, h)),
      out_shape=jax.ShapeDtypeStruct((T_S, N_HEADS * HEAD_DIM), BF16),
      scratch_shapes=_attn_scratch(lk),
      compiler_params=_params(("arbitrary", "arbitrary", "arbitrary")),
      name="attn_latent",
  )(qkv, qkv, qkv, cache_k, cache_v, q_norm, k_norm, c, a, b, c, a, b)


def _pool_kernel(x_ref, xp_ref, xn_ref, g_ref, sh_ref, sc_ref, gate_ref, w_ref, ps_ref, o_ref):
  t = pl.program_id(0)
  npt = T_P // TP
  tiles_s = SEQ_S // TP
  is_p = t < npt
  pos = jnp.where(is_p, 0, (t - npt) % tiles_s)
  ntile = jnp.where(is_p, SEQ_P // TP, tiles_s)
  seq_len = ntile * TP
  keep_prev = jnp.where(pos == 0, 0.0, 1.0)
  keep_next = jnp.where(pos == ntile - 1, 0.0, 1.0)

  g, sh, sc = g_ref[...], sh_ref[...], sc_ref[...]
  x = x_ref[...]
  h = _norm_mod(x, g, sh, sc)
  hp = _norm_mod(xp_ref[...], g, sh, sc) * keep_prev
  hn = _norm_mod(xn_ref[...], g, sh, sc) * keep_next
  ext = jnp.concatenate([hp, h, hn], axis=0)
  n_ext = TP + 2 * POOL_HALO
  tseq = pos * TP + lax.broadcasted_iota(jnp.int32, (TP, POOL_GROUP), 0)

  outs = []
  for gi, win in enumerate(POOL_WINDOWS):
    lo, hi = gi * POOL_GROUP, (gi + 1) * POOL_GROUP
    acc = ext[:, lo:hi]
    span = 1
    while span < win:
      acc = acc + pltpu.roll(acc, n_ext - span, 0)
      span *= 2
    start = POOL_HALO - win // 2
    if start:
      acc = pltpu.roll(acc, n_ext - start, 0)
    ssum = acc[0:TP]
    cnt = (jnp.minimum(tseq + win // 2, seq_len) - jnp.maximum(tseq - win // 2, 0)).astype(F32)
    dlt = (ssum / cnt - h[:, lo:hi]).astype(BF16)
    outs.append(jnp.dot(dlt, w_ref[gi], preferred_element_type=F32))
  y = jnp.concatenate(outs, axis=1) * ps_ref[...]
  o_ref[...] = x + gate_ref[...] * y


def _pool_layer(x, g, shift, scale, gate, w, pscale):
  hb = TP // POOL_HALO
  last = T // POOL_HALO - 1
  seg = lambda i: _seg(i, TP)
  return pl.pallas_call(
      _pool_kernel,
      grid=(T // TP,),
      in_specs=[
          pl.BlockSpec((TP, D), lambda i: (i, 0)),
          pl.BlockSpec((POOL_HALO, D), lambda i: (jnp.maximum(i * hb - 1, 0), 0)),
          pl.BlockSpec((POOL_HALO, D), lambda i: (jnp.minimum((i + 1) * hb, last), 0)),
          pl.BlockSpec((1, D), lambda i: (0, 0)),
          pl.BlockSpec((None, 1, D), lambda i: (seg(i), 0, 0)),
          pl.BlockSpec((None, 1, D), lambda i: (seg(i), 0, 0)),
          pl.BlockSpec((None, 1, D), lambda i: (seg(i), 0, 0)),
          pl.BlockSpec((4, POOL_GROUP, POOL_GROUP), lambda i: (0, 0, 0)),
          pl.BlockSpec((1, D), lambda i: (0, 0)),
      ],
      out_specs=pl.BlockSpec((TP, D), lambda i: (i, 0)),
      out_shape=jax.ShapeDtypeStruct((T, D), F32),
      compiler_params=_params(("arbitrary",)),
      name="pool",
  )(x, x, x, g, shift, scale, gate, w, pscale)


def _ret_kernel(*refs, seq_len, has_s0, nh):
  it = iter(refs)
  q_ref, k_ref, v_ref, gate_ref, dl_ref, ng_ref = [next(it) for _ in range(6)]
  s0_ref = next(it) if has_s0 else None
  y_ref = next(it)
  so_ref = None if has_s0 else next(it)
  u_s = next(it)
  c = RET_CHUNK
  n = seq_len // c
  kscale = RET_DK ** -0.5
  ri = lax.broadcasted_iota(jnp.int32, (c, c), 0).astype(F32)
  ci = lax.broadcasted_iota(jnp.int32, (c, c), 1).astype(F32)
  ri2 = lax.broadcasted_iota(jnp.int32, (c, RET_DV), 0).astype(F32)
  nt = (((1,), (1,)), ((), ()))
  tn = (((0,), (0,)), ((), ()))

  for hh in range(nh):
    qcols = slice(hh * RET_DK, (hh + 1) * RET_DK)
    vcols = slice(hh * RET_DV, (hh + 1) * RET_DV)
    lg = -jnp.log1p(jnp.exp(-dl_ref[hh]))
    lgf, lgb = lg[0], lg[1]
    lgf2 = jnp.concatenate([lgf, lgf], axis=1)
    lgb2 = jnp.concatenate([lgb, lgb], axis=1)
    kd2 = jnp.concatenate([jnp.exp((c - 1.0 - ri) * lgf), jnp.exp(ri * lgb)], axis=1) * kscale

    for j in range(n):
      rows = slice(j * c, (j + 1) * c)
      kc = k_ref[rows, qcols].astype(F32)
      k2 = (jnp.concatenate([kc, kc], axis=1) * kd2).astype(BF16)
      u = lax.dot_general(k2, v_ref[rows, vcols], tn, preferred_element_type=F32)
      u_s[0, hh, j] = u[0:RET_DK]
      u_s[1, hh, j] = u[RET_DK:]

    for d, order, lg2 in ((0, range(n), lgf2), (1, reversed(range(n)), lgb2)):
      cdec = jnp.exp(c * lg2)
      st = s0_ref[d, hh] if has_s0 else jnp.zeros((RET_DK, RET_DV), F32)
      for j in order:
        u = u_s[d, hh, j]
        u_s[d, hh, j] = st
        st = st * cdec + u
      if not has_s0:
        so_ref[d, hh] = st

    intra = (jnp.where(ri >= ci, jnp.exp((ri - ci) * lgf), 0.0)
             + jnp.where(ri <= ci, jnp.exp((ci - ri) * lgb), 0.0)) * kscale
    qd_f = jnp.exp((ri2 + 1.0) * lgf2)
    qd_b = jnp.exp((c - ri2) * lgb2)
    for j in range(n):
      rows = slice(j * c, (j + 1) * c)
      qc = q_ref[rows, qcols]
      vc = v_ref[rows, vcols]
      s = lax.dot_general(qc, k_ref[rows, qcols], nt, preferred_element_type=F32) * intra
      st2 = jnp.concatenate([u_s[0, hh, j], u_s[1, hh, j]], axis=1).astype(BF16)
      inter = jnp.dot(qc, st2, preferred_element_type=F32)
      o = (jnp.dot(s.astype(BF16), vc, preferred_element_type=F32)
           + inter[:, :RET_DV] * qd_f + inter[:, RET_DV:] * qd_b)
      mu = jnp.mean(o, axis=-1, keepdims=True)
      dv = o - mu
      var = jnp.mean(dv * dv, axis=-1, keepdims=True)
      on = (dv * lax.rsqrt(var + EPS)) * ng_ref[:, vcols]
      y_ref[rows, vcols] = (on * _silu(gate_ref[rows, vcols].astype(F32))).astype(BF16)


def _retention(p, dl, ng, state0, prompt):
  seq_len = SEQ_P if prompt else SEQ_S
  nb = NB_P if prompt else NB_S
  nh = RET_HEADS_PER_STEP
  row0 = 0 if prompt else T_P // SEQ_S
  hb = RET_HEADS // nh
  kcol = hb
  vcol = (2 * RET_HEADS * RET_DK) // (nh * RET_DV)
  gcol = vcol + hb
  in_specs = [
      pl.BlockSpec((seq_len, nh * RET_DK), lambda b, h: (row0 + b, h)),
      pl.BlockSpec((seq_len, nh * RET_DK), lambda b, h: (row0 + b, kcol + h)),
      pl.BlockSpec((seq_len, nh * RET_DV), lambda b, h: (row0 + b, vcol + h)),
      pl.BlockSpec((seq_len, nh * RET_DV), lambda b, h: (row0 + b, gcol + h)),
      pl.BlockSpec((nh, 2, 1, RET_DK), lambda b, h: (h, 0, 0, 0)),
      pl.BlockSpec((1, nh * RET_DV), lambda b, h: (0, h)),
  ]
  args = [p, p, p, p, dl, ng]
  y_spec = pl.BlockSpec((seq_len, nh * RET_DV), lambda b, h: (b, h))
  y_shape = jax.ShapeDtypeStruct((nb * seq_len, RET_HEADS * RET_DV), BF16)
  state_spec = pl.BlockSpec((None, 2, nh, RET_DK, RET_DV), lambda b, h: (b, 0, h, 0, 0))
  if prompt:
    out_specs = (y_spec, state_spec)
    out_shape = (y_shape, jax.ShapeDtypeStruct((NB_P, 2, RET_HEADS, RET_DK, RET_DV), F32))
  else:
    in_specs.append(state_spec)
    args.append(state0)
    out_specs = y_spec
    out_shape = y_shape
  return pl.pallas_call(
      functools.partial(_ret_kernel, seq_len=seq_len, has_s0=not prompt, nh=nh),
      grid=(nb, RET_HEADS // nh),
      in_specs=in_specs,
      out_specs=out_specs,
      out_shape=out_shape,
      scratch_shapes=[pltpu.VMEM((2, nh, seq_len // RET_CHUNK, RET_DK, RET_DV), F32)],
      compiler_params=_params(("arbitrary", "arbitrary")),
      name="ret_prompt" if prompt else "ret_latent",
  )(*args)


def _router_kernel(x_ref, g_ref, sh_ref, sc_ref, wrh_ref, wrl_ref, br_ref, tri_ref,
                   info_ref, cnt_ref, carry_s):
  @pl.when(pl.program_id(0) == 0)
  def _():
    carry_s[...] = jnp.zeros_like(carry_s)

  h = _norm_mod(x_ref[...], g_ref[...], sh_ref[...], sc_ref[...])
  logits = _router_logits(h, wrh_ref, wrl_ref, br_ref)
  lane = lax.broadcasted_iota(jnp.int32, logits.shape, 1)
  big = jnp.int32(127)

  def first_max(v):
    m = jnp.max(v, axis=-1, keepdims=True)
    idx = jnp.min(jnp.where(v == m, lane, big), axis=-1, keepdims=True)
    return m, idx

  lgm = jnp.where(lane < N_GROUPS, logits, NEG)
  _, gidx = first_max(lgm)
  lo = N_GROUPS + EPG * gidx
  le = jnp.where((lane >= lo) & (lane < lo + EPG), logits, NEG)
  _, i1 = first_max(le)
  _, i2 = first_max(jnp.where(lane == i1, NEG, le))
  e_lo = jnp.minimum(i1, i2) - N_GROUPS
  e_hi = jnp.maximum(i1, i2) - N_GROUPS
  a = e_lo - EPG * gidx
  b = e_hi - EPG * gidx
  pair_base = jnp.where(a == 0, 0, jnp.where(a == 1, 3, 5))
  cls = N_PAIRS * gidx + pair_base + (b - a - 1)

  onehot = jnp.where(lane == cls, 1.0, 0.0)
  before = jnp.dot(tri_ref[...], onehot.astype(BF16), preferred_element_type=F32) + carry_s[...]
  rank = jnp.sum(jnp.where(lane == cls, before, 0.0), axis=-1, keepdims=True)
  carry_s[...] = carry_s[...] + jnp.sum(onehot, axis=0, keepdims=True)
  cnt_ref[...] = carry_s[...]

  info = jnp.where(lane == 0, cls.astype(F32), jnp.where(lane == 1, rank, 0.0))
  info_ref[...] = info.T[0:8]


def _router(x, g, shift, scale, wr_hi, wr_lo, br, tri):
  return pl.pallas_call(
      _router_kernel,
      grid=(T // TM,),
      in_specs=[
          pl.BlockSpec((TM, D), lambda i: (i, 0)),
          pl.BlockSpec((1, D), lambda i: (0, 0)),
          pl.BlockSpec((None, 1, D), lambda i: (_seg(i, TM), 0, 0)),
          pl.BlockSpec((None, 1, D), lambda i: (_seg(i, TM), 0, 0)),
          pl.BlockSpec((D, 128), lambda i: (0, 0)),
          pl.BlockSpec((D, 128), lambda i: (0, 0)),
          pl.BlockSpec((1, 128), lambda i: (0, 0)),
          pl.BlockSpec((TM, TM), lambda i: (0, 0)),
      ],
      out_specs=(
          pl.BlockSpec((None, 8, TM), lambda i: (i, 0, 0)),
          pl.BlockSpec((1, 128), lambda i: (0, 0)),
      ),
      out_shape=(
          jax.ShapeDtypeStruct((T // TM, 8, TM), F32),
          jax.ShapeDtypeStruct((1, 128), F32),
      ),
      scratch_shapes=[pltpu.VMEM((1, 128), F32)],
      compiler_params=_params(("arbitrary",)),
      name="router",
  )(x, g, shift, scale, wr_hi, wr_lo, br, tri)


def _tile_rows(ref, row):
  return ref.at[pl.ds(pl.multiple_of(row * ROW_SUB, ROW_SUB), ROW_SUB), :]


def _scatter_copy(slot_ref, step, r, stage, buf, hs_hbm, sem):
  return pltpu.make_async_copy(
      stage.at[buf, pl.ds(r * ROW_SUB, ROW_SUB), :],
      _tile_rows(hs_hbm, slot_ref[step * TS + r]), sem.at[buf])


def _scatter_kernel(slot_ref, pstart_ref, pn_ref, nv_ref,
                    x_ref, g_ref, sh_ref, sc_ref, hs_hbm,
                    stage, zero_s, sem, zsem):
  i = pl.program_id(0)
  n = pl.num_programs(0)
  buf = i % 2

  def pad_copy(c, r):
    return pltpu.make_async_copy(zero_s, _tile_rows(hs_hbm, pstart_ref[c] + r), zsem.at[0])

  def tail_copy(t):
    rows = TMM * ROW_SUB
    return pltpu.make_async_copy(
        stage.at[1], hs_hbm.at[pl.ds(pl.multiple_of(t * rows, rows), rows), :], zsem.at[1])

  @pl.when(i == 0)
  def _():
    stage[...] = jnp.zeros_like(stage)
    zero_s[...] = jnp.zeros_like(zero_s)
    for c in range(N_CLASSES):
      def start(r, carry, c=c):
        pad_copy(c, r).start()
        return carry
      lax.fori_loop(0, pn_ref[c], start, 0)

    def tail_start(t, carry):
      tail_copy(t).start()
      return carry
    lax.fori_loop(nv_ref[0], NT_E, tail_start, 0)
    for c in range(N_CLASSES):
      def wait(r, carry, c=c):
        pad_copy(c, r).wait()
        return carry
      lax.fori_loop(0, pn_ref[c], wait, 0)

    def tail_wait(t, carry):
      tail_copy(t).wait()
      return carry
    lax.fori_loop(nv_ref[0], NT_E, tail_wait, 0)

  h = _norm_mod(x_ref[...], g_ref[...], sh_ref[...], sc_ref[...])
  for s in range(ROW_SUB):
    stage[buf, pl.ds(s, TS, stride=ROW_SUB), :] = h[:, s * 128:(s + 1) * 128]

  def start(r, carry):
    _scatter_copy(slot_ref, i, r, stage, buf, hs_hbm, sem).start()
    return carry
  lax.fori_loop(0, TS, start, 0, unroll=8)

  @pl.when(i > 0)
  def _():
    def wait(r, carry):
      _scatter_copy(slot_ref, i - 1, r, stage, 1 - buf, hs_hbm, sem).wait()
      return carry
    lax.fori_loop(0, TS, wait, 0, unroll=8)

  @pl.when(i == n - 1)
  def _():
    def wait(r, carry):
      _scatter_copy(slot_ref, i, r, stage, buf, hs_hbm, sem).wait()
      return carry
    lax.fori_loop(0, TS, wait, 0, unroll=8)


def _scatter_rows(slot, pad_start, pad_n, n_valid, x, g, shift, scale):
  assert TS == TMM
  grid_spec = pltpu.PrefetchScalarGridSpec(
      num_scalar_prefetch=4,
      grid=(T // TS,),
      in_specs=[
          pl.BlockSpec((TS, D), lambda i, *_: (i, 0)),
          pl.BlockSpec((1, D), lambda i, *_: (0, 0)),
          pl.BlockSpec((None, 1, D), lambda i, *_: (_seg(i, TS), 0, 0)),
          pl.BlockSpec((None, 1, D), lambda i, *_: (_seg(i, TS), 0, 0)),
      ],
      out_specs=pl.BlockSpec(memory_space=pl.ANY),
      scratch_shapes=[
          pltpu.VMEM((2, TS * ROW_SUB, 128), F32),
          pltpu.VMEM((ROW_SUB, 128), F32),
          pltpu.SemaphoreType.DMA((2,)),
          pltpu.SemaphoreType.DMA((2,)),
      ],
  )
  return pl.pallas_call(
      _scatter_kernel,
      grid_spec=grid_spec,
      out_shape=jax.ShapeDtypeStruct((P_PAD * ROW_SUB, 128), F32),
      compiler_params=_params(("arbitrary",)),
      name="scatter_rows",
  )(slot, pad_start, pad_n, n_valid, x, g, shift, scale)


def _router_logits(h, wrh_ref, wrl_ref, br_ref):
  h_hi = h.astype(BF16)
  h_lo = (h - h_hi.astype(F32)).astype(BF16)
  dot = functools.partial(jnp.dot, preferred_element_type=F32)
  return (dot(h_hi, wrh_ref[...]) + dot(h_lo, wrh_ref[...]) + dot(h_hi, wrl_ref[...])
          + br_ref[...])


def _expert_kernel(elo_ref, ehi_ref, nv_ref, hs_ref, wrh_ref, wrl_ref, br_ref,
                   wg_lo, wg_hi, wu_lo, wu_hi, wd_lo, wd_hi, y_ref):
  i = pl.program_id(0)

  @pl.when(i < nv_ref[0])
  def _():
    h = jnp.concatenate(
        [hs_ref[pl.ds(s, TMM, stride=ROW_SUB), :] for s in range(ROW_SUB)], axis=1)
    hb = h.astype(BF16)
    logits = _router_logits(h, wrh_ref, wrl_ref, br_ref)
    lane = lax.broadcasted_iota(jnp.int32, logits.shape, 1)
    elo, ehi = elo_ref[i], ehi_ref[i]

    def pick(idx):
      return jnp.sum(jnp.where(lane == idx, logits, 0.0), axis=-1, keepdims=True)
    l_g, l_lo, l_hi = pick(elo // EPG), pick(N_GROUPS + elo), pick(N_GROUPS + ehi)
    p_top = 1.0 / jnp.sum(jnp.where(lane < N_GROUPS, jnp.exp(logits - l_g), 0.0),
                          axis=-1, keepdims=True)
    wl = jnp.broadcast_to(p_top / (1.0 + jnp.exp(l_hi - l_lo)), (TMM, D_EXPERT))
    wh = jnp.broadcast_to(p_top / (1.0 + jnp.exp(l_lo - l_hi)), (TMM, D_EXPERT))
    dot = functools.partial(jnp.dot, preferred_element_type=F32)
    a_lo = (_silu(dot(hb, wg_lo[...])) * dot(hb, wu_lo[...])) * wl
    a_hi = (_silu(dot(hb, wg_hi[...])) * dot(hb, wu_hi[...])) * wh
    y = dot(a_lo.astype(BF16), wd_lo[...]) + dot(a_hi.astype(BF16), wd_hi[...])
    for s in range(ROW_SUB):
      y_ref[pl.ds(s, TMM, stride=ROW_SUB), :] = y[:, s * 128:(s + 1) * 128]

  @pl.when(i >= nv_ref[0])
  def _():
    y_ref[...] = jnp.zeros_like(y_ref)


def _experts(tile_elo, tile_ehi, n_valid, hs, wr_hi, wr_lo, br, w_gate, w_up, w_down):
  up_spec_lo = pl.BlockSpec((None, D, D_EXPERT), lambda i, elo, ehi, nv: (elo[i], 0, 0))
  up_spec_hi = pl.BlockSpec((None, D, D_EXPERT), lambda i, elo, ehi, nv: (ehi[i], 0, 0))
  dn_spec_lo = pl.BlockSpec((None, D_EXPERT, D), lambda i, elo, ehi, nv: (elo[i], 0, 0))
  dn_spec_hi = pl.BlockSpec((None, D_EXPERT, D), lambda i, elo, ehi, nv: (ehi[i], 0, 0))
  grid_spec = pltpu.PrefetchScalarGridSpec(
      num_scalar_prefetch=3,
      grid=(NT_E,),
      in_specs=[
          pl.BlockSpec((TMM * ROW_SUB, 128), lambda i, elo, ehi, nv: (i, 0)),
          pl.BlockSpec((D, 128), lambda i, elo, ehi, nv: (0, 0)),
          pl.BlockSpec((D, 128), lambda i, elo, ehi, nv: (0, 0)),
          pl.BlockSpec((1, 128), lambda i, elo, ehi, nv: (0, 0)),
          up_spec_lo, up_spec_hi, up_spec_lo, up_spec_hi, dn_spec_lo, dn_spec_hi,
      ],
      out_specs=pl.BlockSpec((TMM * ROW_SUB, 128), lambda i, elo, ehi, nv: (i, 0)),
  )
  return pl.pallas_call(
      _expert_kernel,
      grid_spec=grid_spec,
      out_shape=jax.ShapeDtypeStruct((P_PAD * ROW_SUB, 128), F32),
      compiler_params=_params(("arbitrary",)),
      name="experts",
  )(tile_elo, tile_ehi, n_valid, hs, wr_hi, wr_lo, br,
    w_gate, w_gate, w_up, w_up, w_down, w_down)


def _gather_copy(slot_ref, step, r, y_hbm, ybuf, buf, sem):
  return pltpu.make_async_copy(
      _tile_rows(y_hbm, slot_ref[step * TC + r]),
      ybuf.at[buf, pl.ds(r * ROW_SUB, ROW_SUB), :], sem.at[buf])


def _combine_kernel(slot_ref, x_ref, gate_ref, nf_ref, y_hbm, *rest, final):
  if final:
    op_ref, os_ref, ybuf, sem = rest
  else:
    o_ref, ybuf, sem = rest
  i = pl.program_id(0)
  n = pl.num_programs(0)
  buf = i % 2

  def start(step, b):
    def body(r, carry):
      _gather_copy(slot_ref, step, r, y_hbm, ybuf, b, sem).start()
      return carry
    lax.fori_loop(0, TC, body, 0, unroll=8)

  @pl.when(i == 0)
  def _():
    start(0, 0)

  @pl.when(i + 1 < n)
  def _():
    start(i + 1, 1 - buf)

  def wait(r, carry):
    _gather_copy(slot_ref, i, r, y_hbm, ybuf, buf, sem).wait()
    return carry
  lax.fori_loop(0, TC, wait, 0, unroll=8)

  y = jnp.concatenate(
      [ybuf[buf, pl.ds(s, TC, stride=ROW_SUB), :] for s in range(ROW_SUB)], axis=1)
  x = x_ref[...] + gate_ref[...] * y
  if not final:
    o_ref[...] = x
    return
  x = (x * lax.rsqrt(jnp.mean(x * x, axis=-1, keepdims=True) + EPS)) * nf_ref[...]
  is_prompt = i < T_P // TC

  @pl.when(is_prompt)
  def _():
    op_ref[...] = x

  @pl.when(jnp.logical_not(is_prompt))
  def _():
    os_ref[...] = x


def _combine(slot, x, gate, norm_f, y_sorted, final):
  npt = T_P // TC
  if final:
    out_specs = (pl.BlockSpec((TC, D), lambda i, s: (jnp.minimum(i, npt - 1), 0)),
                 pl.BlockSpec((TC, D), lambda i, s: (jnp.maximum(i - npt, 0), 0)))
    out_shape = (jax.ShapeDtypeStruct((T_P, D), F32), jax.ShapeDtypeStruct((T_S, D), F32))
  else:
    out_specs = pl.BlockSpec((TC, D), lambda i, s: (i, 0))
    out_shape = jax.ShapeDtypeStruct((T, D), F32)
  grid_spec = pltpu.PrefetchScalarGridSpec(
      num_scalar_prefetch=1,
      grid=(T // TC,),
      in_specs=[
          pl.BlockSpec((TC, D), lambda i, s: (i, 0)),
          pl.BlockSpec((None, 1, D), lambda i, s: (_seg(i, TC), 0, 0)),
          pl.BlockSpec((1, D), lambda i, s: (0, 0)),
          pl.BlockSpec(memory_space=pl.ANY),
      ],
      out_specs=out_specs,
      scratch_shapes=[pltpu.VMEM((2, TC * ROW_SUB, 128), F32), pltpu.SemaphoreType.DMA((2,))],
  )
  return pl.pallas_call(
      functools.partial(_combine_kernel, final=final),
      grid_spec=grid_spec,
      out_shape=out_shape,
      compiler_params=_params(("arbitrary",)),
      name="combine_final" if final else "combine",
  )(slot, x, gate, norm_f, y_sorted)


def _class_experts():
  lo, hi = [], []
  for g in range(N_GROUPS):
    for a in range(EPG):
      for b in range(a + 1, EPG):
        lo.append(g * EPG + a)
        hi.append(g * EPG + b)
  return np.asarray(lo, np.int32), np.asarray(hi, np.int32)


def _moe_layer(x, g, shift, scale, gate, wr_hi, wr_lo, br, tri, w_gate, w_up, w_down, norm_f, final):
  info, cnt = _router(x, g, shift, scale, wr_hi, wr_lo, br, tri)

  cls = info[:, 0, :].astype(jnp.int32)
  rank = info[:, 1, :].astype(jnp.int32)
  counts = cnt[0, :N_CLASSES].astype(jnp.int32)
  tiles = (counts + TMM - 1) // TMM
  tile_end = jnp.cumsum(tiles)
  offs = (tile_end - tiles) * TMM
  slot = rank
  for k in range(N_CLASSES):
    slot = slot + jnp.where(cls == k, offs[k], 0)
  slot = slot.reshape(T)
  n_valid = tile_end[-1]
  tile_ids = jnp.minimum(jnp.arange(NT_E, dtype=jnp.int32), n_valid - 1)
  tile_cls = jnp.sum((tile_ids[:, None] >= tile_end[None, :]).astype(jnp.int32), axis=1)
  tile_cls = jnp.minimum(tile_cls, N_CLASSES - 1)
  cls_lo, cls_hi = _class_experts()
  tile_elo = jnp.asarray(cls_lo)[tile_cls]
  tile_ehi = jnp.asarray(cls_hi)[tile_cls]

  n_valid = n_valid.reshape(1)
  hs = _scatter_rows(slot, offs + counts, tiles * TMM - counts, n_valid, x, g, shift, scale)
  y_sorted = _experts(tile_elo, tile_ehi, n_valid, hs, wr_hi, wr_lo, br, w_gate, w_up, w_down)
  return _combine(slot, x, gate, norm_f, y_sorted, final)


def kernel(x_prompt, x_sample, cache_k, cache_v, state_ret, c, c_ctx, norm1, norm2, w_ada, b_ada, attn_w_qkv, attn_q_norm, attn_k_norm, attn_w_o, pool_w, pool_scale, ret_w_in, ret_decay_logit, ret_norm, ret_w_out, moe_w_router_g, moe_b_router_g, moe_w_router_e, moe_b_router_e, moe_w_gate, moe_w_up, moe_w_down, norm_f):
  x = jnp.concatenate([x_prompt.reshape(T_P, D), x_sample.reshape(T_S, D)], axis=0)
  cond8 = jnp.concatenate([c_ctx[None, :], c, jnp.zeros((N_SEG - 1 - NB_S, D), F32)], axis=0)
  mods = _adaln(cond8, w_ada, b_ada)
  mods = mods.reshape(DEPTH, N_SEG, 6, 1, D).transpose(0, 2, 1, 3, 4)
  rope = _rope_tables()
  tri = jnp.tril(jnp.ones((TM, TM), BF16), -1)
  pad_r = 128 - N_GROUPS - N_EXPERTS
  norm_f2 = norm_f.reshape(1, D)

  new_k, new_v, new_s = [], [], []
  for i in range(DEPTH):
    kind, j = i % 3, i // 3
    m = mods[i]
    g1 = norm1[i].reshape(1, D)
    g2 = norm2[i].reshape(1, D)
    if kind == 0:
      qkv = _nm_matmul(x, g1, m[0], m[1], attn_w_qkv[j].astype(BF16), TM, "qkv_proj")
      qn = attn_q_norm[j].reshape(1, HEAD_DIM)
      kn = attn_k_norm[j].reshape(1, HEAD_DIM)
      o_p, kc, vc = _attn_prompt(qkv, qn, kn)
      ck = cache_k[:, j].reshape(NB_S, PAST, N_KV * HEAD_DIM)
      cv = cache_v[:, j].reshape(NB_S, PAST, N_KV * HEAD_DIM)
      o_s = _attn_latent(qkv, ck, cv, qn, kn, rope)
      new_k.append(kc.reshape(NB_P, SEQ_P, N_KV, HEAD_DIM))
      new_v.append(vc.reshape(NB_P, SEQ_P, N_KV, HEAD_DIM))
      x = _mm_res(o_p, o_s, attn_w_o[j].astype(BF16), x, m[2], "attn_out")
    elif kind == 1:
      x = _pool_layer(x, g1, m[0], m[1], m[2], pool_w[j].astype(BF16), pool_scale[j].reshape(1, D))
    else:
      p = _nm_matmul(x, g1, m[0], m[1], ret_w_in[j].astype(BF16), 256, "ret_proj")
      dl = jnp.broadcast_to(ret_decay_logit[j].T[:, :, None, None], (RET_HEADS, 2, 1, RET_DK))
      ng = ret_norm[j].reshape(1, RET_HEADS * RET_DV)
      y_p, s_new = _retention(p, dl, ng, None, True)
      y_s = _retention(p, dl, ng, state_ret[:, j], False)
      new_s.append(s_new)
      x = _mm_res(y_p, y_s, ret_w_out[j].astype(BF16), x, m[2], "ret_out")
    wr = jnp.concatenate([moe_w_router_g[i], moe_w_router_e[i], jnp.zeros((D, pad_r), F32)], axis=1)
    br = jnp.concatenate([moe_b_router_g[i], moe_b_router_e[i], jnp.zeros((pad_r,), F32)]).reshape(1, 128)
    wr_hi = wr.astype(BF16)
    wr_lo = (wr - wr_hi.astype(F32)).astype(BF16)
    x = _moe_layer(x, g2, m[3], m[4], m[5], wr_hi, wr_lo, br, tri,
                   moe_w_gate[i].astype(BF16), moe_w_up[i].astype(BF16), moe_w_down[i].astype(BF16),
                   norm_f2, i == DEPTH - 1)

  y_prompt = x[0].reshape(NB_P, SEQ_P, D)
  y_sample = x[1].reshape(NB_S, SEQ_S, D)
  new_cache_k = jnp.stack(new_k, axis=1)
  new_cache_v = jnp.stack(new_v, axis=1)
  new_state_ret = jnp.stack(new_s, axis=1)
  return (y_prompt, y_sample, new_cache_k, new_cache_v, new_state_ret)
```

```python
import functools

import jax
import jax.numpy as jnp
import numpy as np
from jax import lax
from jax.experimental import pallas as pl
from jax.experimental.pallas import tpu as pltpu

F32 = jnp.float32
BF16 = jnp.bfloat16

D = 1024
NB_P, SEQ_P = 32, 256
NB_S, SEQ_S = 4, 2048
T_P = NB_P * SEQ_P
T_S = NB_S * SEQ_S
T = T_P + T_S
DEPTH = 4
GRID_W = 64
HEAD_DIM = 128
N_HEADS = 8
N_KV = 2
KV_GROUP = N_HEADS // N_KV
PAST = 256
ROPE_THETA = 10000.0
POOL_WINDOWS = (2, 4, 8, 16)
POOL_GROUP = D // 4
POOL_HALO = 8
RET_HEADS = 8
RET_DK = 128
RET_DV = 256
RET_CHUNK = 128
RET_HEADS_PER_STEP = 2
N_GROUPS = 4
EPG = 4
N_EXPERTS = 16
D_EXPERT = 256
N_PAIRS = 6
N_CLASSES = N_GROUPS * N_PAIRS
ROUTER_ROWS = 32
EPS = 1e-6
LOG2E = 1.4426950408889634
NEG = -1e30
N_SEG = 8

VMEM_LIMIT_BYTES = 52 * 1024 * 1024

TM = 512
TQ = 128
ATTN_KEY_CHUNK = 256
ATTN_HEAD_STACK = 2
TP = 256
TMM = 256
TS = 256
TC = 256
ROW_SUB = 8
DMA_QUEUES = 2
PACK_SUB = D // 2 // 128
INFO_SUB = PACK_SUB
P_PAD = T + N_CLASSES * TMM
NT_E = P_PAD // TMM


def _params(sem):
  return pltpu.CompilerParams(dimension_semantics=sem, vmem_limit_bytes=VMEM_LIMIT_BYTES)


def _seg(i, tm):
  npt = T_P // tm
  return jnp.where(i < npt, 0, (i - npt) // (SEQ_S // tm) + 1)


def _norm_mod(x, g, shift, scale):
  r = lax.rsqrt(jnp.mean(x * x, axis=-1, keepdims=True) + EPS)
  return ((x * r) * g) * (1.0 + scale) + shift


def _silu(x):
  return x * (1.0 / (1.0 + jnp.exp(-x)))


def _adaln_kernel(c_ref, w_ref, b_ref, o_ref):
  s = _silu(c_ref[...]).astype(BF16)
  o_ref[...] = jnp.dot(s, w_ref[...].astype(BF16), preferred_element_type=F32) + b_ref[...]


def _adaln(cond8, w_ada, b_ada):
  tn = 1536
  return pl.pallas_call(
      _adaln_kernel,
      grid=(DEPTH, 6 * D // tn),
      in_specs=[
          pl.BlockSpec((N_SEG, D), lambda l, j: (0, 0)),
          pl.BlockSpec((None, D, tn), lambda l, j: (l, 0, j)),
          pl.BlockSpec((None, 1, tn), lambda l, j: (l, 0, j)),
      ],
      out_specs=pl.BlockSpec((None, N_SEG, tn), lambda l, j: (l, 0, j)),
      out_shape=jax.ShapeDtypeStruct((DEPTH, N_SEG, 6 * D), F32),
      compiler_params=_params(("arbitrary", "arbitrary")),
      name="adaln",
  )(cond8, w_ada, b_ada.reshape(DEPTH, 1, 6 * D))


def _nm_matmul_kernel(x_ref, g_ref, sh_ref, sc_ref, w_ref, o_ref, *, n_chunk):
  h = _norm_mod(x_ref[...], g_ref[...], sh_ref[...], sc_ref[...]).astype(BF16)
  n = w_ref.shape[1]
  for c in range(0, n, n_chunk):
    o_ref[:, c:c + n_chunk] = jnp.dot(
        h, w_ref[:, c:c + n_chunk], preferred_element_type=F32).astype(o_ref.dtype)


def _nm_matmul(x, g, shift, scale, w, tm, name):
  n = w.shape[1]
  return pl.pallas_call(
      functools.partial(_nm_matmul_kernel, n_chunk=512),
      grid=(T // tm,),
      in_specs=[
          pl.BlockSpec((tm, D), lambda i: (i, 0)),
          pl.BlockSpec((1, D), lambda i: (0, 0)),
          pl.BlockSpec((None, 1, D), lambda i: (_seg(i, tm), 0, 0)),
          pl.BlockSpec((None, 1, D), lambda i: (_seg(i, tm), 0, 0)),
          pl.BlockSpec((D, n), lambda i: (0, 0)),
      ],
      out_specs=pl.BlockSpec((tm, n), lambda i: (i, 0)),
      out_shape=jax.ShapeDtypeStruct((T, n), BF16),
      compiler_params=_params(("arbitrary",)),
      name=name,
  )(x, g, shift, scale, w)


def _mm_res_kernel(ap_ref, as_ref, w_ref, x_ref, gate_ref, o_ref):
  def run(a_ref):
    y = jnp.dot(a_ref[...], w_ref[...], preferred_element_type=F32)
    o_ref[...] = x_ref[...] + gate_ref[...] * y

  is_prompt = pl.program_id(0) < T_P // TM
  pl.when(is_prompt)(lambda: run(ap_ref))
  pl.when(jnp.logical_not(is_prompt))(lambda: run(as_ref))


def _mm_res(a_p, a_s, w, x, gate, name):
  k = a_p.shape[1]
  npt = T_P // TM
  return pl.pallas_call(
      _mm_res_kernel,
      grid=(T // TM,),
      in_specs=[
          pl.BlockSpec((TM, k), lambda i: (jnp.minimum(i, npt - 1), 0)),
          pl.BlockSpec((TM, k), lambda i: (jnp.maximum(i - npt, 0), 0)),
          pl.BlockSpec((k, D), lambda i: (0, 0)),
          pl.BlockSpec((TM, D), lambda i: (i, 0)),
          pl.BlockSpec((None, 1, D), lambda i: (_seg(i, TM), 0, 0)),
      ],
      out_specs=pl.BlockSpec((TM, D), lambda i: (i, 0)),
      out_shape=jax.ShapeDtypeStruct((T, D), F32),
      compiler_params=_params(("arbitrary",)),
      name=name,
  )(a_p, a_s, w, x, gate)


def _rope(x, c, a, b):
  return x * c + pltpu.roll(x, 96, 1) * a + pltpu.roll(x, 32, 1) * b


def _head_norm(x, w):
  return (x * lax.rsqrt(jnp.mean(x * x, axis=-1, keepdims=True) + EPS)) * w


def _attn_latent_kernel(q_ref, kn_ref, vn_ref, ck_ref, cv_ref, qw_ref, kw_ref,
                        cq_ref, aq_ref, bq_ref, ckk_ref, akk_ref, bkk_ref,
                        o_ref, k_s, vt_s, *st_refs, tq):
  @pl.when(pl.program_id(2) == 0)
  def _():
    k = _rope(_head_norm(kn_ref[...].astype(F32), kw_ref[...]), ckk_ref[...], akk_ref[...], bkk_ref[...])
    k_s[0:PAST, :] = ck_ref[...].astype(BF16)
    k_s[PAST:, :] = k.astype(BF16)
    vt_s[:, 0:PAST] = cv_ref[...].T.astype(BF16)
    vt_s[:, PAST:] = vn_ref[...].astype(F32).T.astype(BF16)

  qa = q_ref[...].astype(F32)
  lk = k_s.shape[0]

  def prep_q(h):
    qh = _head_norm(qa[:, h * HEAD_DIM:(h + 1) * HEAD_DIM], qw_ref[...])
    qh = _rope(qh, cq_ref[...], aq_ref[...], bq_ref[...])
    return (qh * (HEAD_DIM ** -0.5 * LOG2E)).astype(BF16)

  nq = ATTN_HEAD_STACK * tq
  for h0 in range(0, KV_GROUP, ATTN_HEAD_STACK):
    qp = jnp.concatenate([prep_q(h0 + d) for d in range(ATTN_HEAD_STACK)], axis=0)
    st_s = st_refs[h0 // ATTN_HEAD_STACK]
    chunks = [(c0, min(c0 + ATTN_KEY_CHUNK, lk)) for c0 in range(0, lk, ATTN_KEY_CHUNK)]
    m = jnp.full((1, nq), NEG, F32)
    for c0, c1 in chunks:
      st = lax.dot_general(k_s[c0:c1, :], qp, (((1,), (1,)), ((), ())),
                           preferred_element_type=F32)
      st_s[c0:c1, :] = st
      m = jnp.maximum(m, jnp.max(st, axis=0, keepdims=True))
    l = jnp.zeros((1, nq), F32)
    acc = jnp.zeros((HEAD_DIM, nq), F32)
    for c0, c1 in chunks:
      pt = jnp.exp2(st_s[c0:c1, :] - m)
      l = l + jnp.sum(pt, axis=0, keepdims=True)
      acc = acc + jnp.dot(vt_s[:, c0:c1], pt.astype(BF16), preferred_element_type=F32)
    o = acc / l
    for d in range(ATTN_HEAD_STACK):
      h = h0 + d
      o_ref[:, h * HEAD_DIM:(h + 1) * HEAD_DIM] = o[:, d * tq:(d + 1) * tq].T.astype(BF16)


def _attn_scratch(lk):
  stacks = KV_GROUP // ATTN_HEAD_STACK
  return ([pltpu.VMEM((lk, HEAD_DIM), BF16), pltpu.VMEM((HEAD_DIM, lk), BF16)]
          + [pltpu.VMEM((lk, ATTN_HEAD_STACK * TQ), F32) for _ in range(stacks)])


def _rope_tables():
  rows = SEQ_S // GRID_W
  t_row = jnp.broadcast_to(jnp.arange(rows)[:, None], (rows, GRID_W)).reshape(-1)
  t_col = jnp.broadcast_to(jnp.arange(GRID_W)[None, :], (rows, GRID_W)).reshape(-1)
  nf = HEAD_DIM // 4
  inv = ROPE_THETA ** (-jnp.arange(nf, dtype=F32) / nf)
  ang_r = t_row.astype(F32)[:, None] * inv[None, :]
  ang_c = t_col.astype(F32)[:, None] * inv[None, :]
  cr, sr, cc, sc = jnp.cos(ang_r), jnp.sin(ang_r), jnp.cos(ang_c), jnp.sin(ang_c)
  z = jnp.zeros_like(sr)
  c = jnp.concatenate([cr, cr, cc, cc], axis=1)
  a = jnp.concatenate([-sr, z, -sc, z], axis=1)
  b = jnp.concatenate([z, sr, z, sc], axis=1)
  return c, a, b


def _attn_prompt_kernel(qkv_ref, qw_ref, kw_ref, o_ref, ko_ref, vo_ref):
  nq = N_HEADS * HEAD_DIM
  nk = N_KV * HEAD_DIM
  for kh in range(N_KV):
    kcols = slice(nq + kh * HEAD_DIM, nq + (kh + 1) * HEAD_DIM)
    vcols = slice(nq + nk + kh * HEAD_DIM, nq + nk + (kh + 1) * HEAD_DIM)
    ocols = slice(kh * HEAD_DIM, (kh + 1) * HEAD_DIM)
    k = _head_norm(qkv_ref[:, kcols].astype(F32), kw_ref[...])
    v = qkv_ref[:, vcols].astype(F32)
    ko_ref[:, ocols] = k
    vo_ref[:, ocols] = v
    kb = k.astype(BF16)
    vt = v.T.astype(BF16)
    for h in range(kh * KV_GROUP, (kh + 1) * KV_GROUP):
      hcols = slice(h * HEAD_DIM, (h + 1) * HEAD_DIM)
      qh = _head_norm(qkv_ref[:, hcols].astype(F32), qw_ref[...])
      qh = (qh * (HEAD_DIM ** -0.5 * LOG2E)).astype(BF16)
      st = lax.dot_general(kb, qh, (((1,), (1,)), ((), ())), preferred_element_type=F32)
      pt = jnp.exp2(st - jnp.max(st, axis=0, keepdims=True))
      l = jnp.sum(pt, axis=0, keepdims=True)
      acc = jnp.dot(vt, pt.astype(BF16), preferred_element_type=F32)
      o_ref[:, hcols] = (acc / l).T.astype(BF16)


def _attn_prompt(qkv, q_norm, k_norm):
  width = (N_HEADS + 2 * N_KV) * HEAD_DIM
  out_shapes = (
      jax.ShapeDtypeStruct((T_P, N_HEADS * HEAD_DIM), BF16),
      jax.ShapeDtypeStruct((NB_P, SEQ_P, N_KV * HEAD_DIM), F32),
      jax.ShapeDtypeStruct((NB_P, SEQ_P, N_KV * HEAD_DIM), F32),
  )
  return pl.pallas_call(
      _attn_prompt_kernel,
      grid=(NB_P,),
      in_specs=[
          pl.BlockSpec((SEQ_P, width), lambda b: (b, 0)),
          pl.BlockSpec((1, HEAD_DIM), lambda b: (0, 0)),
          pl.BlockSpec((1, HEAD_DIM), lambda b: (0, 0)),
      ],
      out_specs=(
          pl.BlockSpec((SEQ_P, N_HEADS * HEAD_DIM), lambda b: (b, 0)),
          pl.BlockSpec((None, SEQ_P, N_KV * HEAD_DIM), lambda b: (b, 0, 0)),
          pl.BlockSpec((None, SEQ_P, N_KV * HEAD_DIM), lambda b: (b, 0, 0)),
      ),
      out_shape=out_shapes,
      compiler_params=_params(("arbitrary",)),
      name="attn_prompt",
  )(qkv, q_norm, k_norm)


def _attn_latent(qkv, cache_k, cache_v, q_norm, k_norm, rope):
  kcol = N_HEADS
  vcol = kcol + N_KV
  nq = SEQ_S // TQ
  row0 = T_P // TQ
  seq0 = T_P // SEQ_S
  lk = PAST + SEQ_S
  c, a, b = rope
  tab_q = pl.BlockSpec((TQ, HEAD_DIM), lambda bb, h, i: (i, 0))
  tab_k = pl.BlockSpec((SEQ_S, HEAD_DIM), lambda bb, h, i: (0, 0))
  return pl.pallas_call(
      functools.partial(_attn_latent_kernel, tq=TQ),
      grid=(NB_S, N_KV, nq),
      in_specs=[
          pl.BlockSpec((TQ, KV_GROUP * HEAD_DIM), lambda bb, h, i: (row0 + bb * nq + i, h)),
          pl.BlockSpec((SEQ_S, HEAD_DIM), lambda bb, h, i: (seq0 + bb, kcol + h)),
          pl.BlockSpec((SEQ_S, HEAD_DIM), lambda bb, h, i: (seq0 + bb, vcol + h)),
          pl.BlockSpec((None, PAST, HEAD_DIM), lambda bb, h, i: (bb, 0, h)),
          pl.BlockSpec((None, PAST, HEAD_DIM), lambda bb, h, i: (bb, 0, h)),
          pl.BlockSpec((1, HEAD_DIM), lambda bb, h, i: (0, 0)),
          pl.BlockSpec((1, HEAD_DIM), lambda bb, h, i: (0, 0)),
          tab_q, tab_q, tab_q, tab_k, tab_k, tab_k,
      ],
      out_specs=pl.BlockSpec((TQ, KV_GROUP * HEAD_DIM), lambda bb, h, i: (bb * nq + i, h)),
      out_shape=jax.ShapeDtypeStruct((T_S, N_HEADS * HEAD_DIM), BF16),
      scratch_shapes=_attn_scratch(lk),
      compiler_params=_params(("arbitrary", "arbitrary", "arbitrary")),
      name="attn_latent",
  )(qkv, qkv, qkv, cache_k, cache_v, q_norm, k_norm, c, a, b, c, a, b)


def _pool_kernel(x_ref, xp_ref, xn_ref, g_ref, sh_ref, sc_ref, gate_ref, w_ref, ps_ref, o_ref):
  t = pl.program_id(0)
  npt = T_P // TP
  tiles_s = SEQ_S // TP
  is_p = t < npt
  pos = jnp.where(is_p, 0, (t - npt) % tiles_s)
  ntile = jnp.where(is_p, SEQ_P // TP, tiles_s)
  seq_len = ntile * TP
  keep_prev = jnp.where(pos == 0, 0.0, 1.0)
  keep_next = jnp.where(pos == ntile - 1, 0.0, 1.0)

  g, sh, sc = g_ref[...], sh_ref[...], sc_ref[...]
  x = x_ref[...]
  h = _norm_mod(x, g, sh, sc)
  hp = _norm_mod(xp_ref[...], g, sh, sc) * keep_prev
  hn = _norm_mod(xn_ref[...], g, sh, sc) * keep_next
  ext = jnp.concatenate([hp, h, hn], axis=0)
  n_ext = TP + 2 * POOL_HALO
  tseq = pos * TP + lax.broadcasted_iota(jnp.int32, (TP, POOL_GROUP), 0)

  outs = []
  for gi, win in enumerate(POOL_WINDOWS):
    lo, hi = gi * POOL_GROUP, (gi + 1) * POOL_GROUP
    acc = ext[:, lo:hi]
    span = 1
    while span < win:
      acc = acc + pltpu.roll(acc, n_ext - span, 0)
      span *= 2
    start = POOL_HALO - win // 2
    if start:
      acc = pltpu.roll(acc, n_ext - start, 0)
    ssum = acc[0:TP]
    cnt = (jnp.minimum(tseq + win // 2, seq_len) - jnp.maximum(tseq - win // 2, 0)).astype(F32)
    dlt = (ssum / cnt - h[:, lo:hi]).astype(BF16)
    outs.append(jnp.dot(dlt, w_ref[gi], preferred_element_type=F32))
  y = jnp.concatenate(outs, axis=1) * ps_ref[...]
  o_ref[...] = x + gate_ref[...] * y


def _pool_layer(x, g, shift, scale, gate, w, pscale):
  hb = TP // POOL_HALO
  last = T // POOL_HALO - 1
  seg = lambda i: _seg(i, TP)
  return pl.pallas_call(
      _pool_kernel,
      grid=(T // TP,),
      in_specs=[
          pl.BlockSpec((TP, D), lambda i: (i, 0)),
          pl.BlockSpec((POOL_HALO, D), lambda i: (jnp.maximum(i * hb - 1, 0), 0)),
          pl.BlockSpec((POOL_HALO, D), lambda i: (jnp.minimum((i + 1) * hb, last), 0)),
          pl.BlockSpec((1, D), lambda i: (0, 0)),
          pl.BlockSpec((None, 1, D), lambda i: (seg(i), 0, 0)),
          pl.BlockSpec((None, 1, D), lambda i: (seg(i), 0, 0)),
          pl.BlockSpec((None, 1, D), lambda i: (seg(i), 0, 0)),
          pl.BlockSpec((4, POOL_GROUP, POOL_GROUP), lambda i: (0, 0, 0)),
          pl.BlockSpec((1, D), lambda i: (0, 0)),
      ],
      out_specs=pl.BlockSpec((TP, D), lambda i: (i, 0)),
      out_shape=jax.ShapeDtypeStruct((T, D), F32),
      compiler_params=_params(("arbitrary",)),
      name="pool",
  )(x, x, x, g, shift, scale, gate, w, pscale)


def _ret_kernel(*refs, seq_len, has_s0, nh):
  it = iter(refs)
  q_ref, k_ref, v_ref, gate_ref, dl_ref, ng_ref = [next(it) for _ in range(6)]
  s0_ref = next(it) if has_s0 else None
  y_ref = next(it)
  so_ref = None if has_s0 else next(it)
  u_s = next(it)
  dec_s, cdec_s = next(it), next(it)
  c = RET_CHUNK
  n = seq_len // c
  kscale = RET_DK ** -0.5
  nt = (((1,), (1,)), ((), ()))
  tn = (((0,), (0,)), ((), ()))

  @pl.when(pl.program_id(1) == 0)
  def _():
    ri = lax.broadcasted_iota(jnp.int32, (c, c), 0).astype(F32)
    ci = lax.broadcasted_iota(jnp.int32, (c, c), 1).astype(F32)
    ri2 = lax.broadcasted_iota(jnp.int32, (c, RET_DV), 0).astype(F32)
    for hh in range(nh):
      lg = -jnp.log1p(jnp.exp(-dl_ref[hh]))
      lgf, lgb = lg[0], lg[1]
      lgf2 = jnp.concatenate([lgf, lgf], axis=1)
      lgb2 = jnp.concatenate([lgb, lgb], axis=1)
      dec_s[hh, 0] = jnp.concatenate(
          [jnp.exp((c - 1.0 - ri) * lgf), jnp.exp(ri * lgb)], axis=1) * kscale
      intra = (jnp.where(ri >= ci, jnp.exp((ri - ci) * lgf), 0.0)
               + jnp.where(ri <= ci, jnp.exp((ci - ri) * lgb), 0.0)) * kscale
      dec_s[hh, 1] = jnp.concatenate([intra, intra], axis=1)
      dec_s[hh, 2] = jnp.exp((ri2 + 1.0) * lgf2)
      dec_s[hh, 3] = jnp.exp((c - ri2) * lgb2)
      cdec_s[hh, 0] = jnp.exp(c * lgf2)
      cdec_s[hh, 1] = jnp.exp(c * lgb2)

  for hh in range(nh):
    qcols = slice(hh * RET_DK, (hh + 1) * RET_DK)
    vcols = slice(hh * RET_DV, (hh + 1) * RET_DV)
    kd2 = dec_s[hh, 0]

    for j in range(n):
      rows = slice(j * c, (j + 1) * c)
      kc = k_ref[rows, qcols].astype(F32)
      k2 = (jnp.concatenate([kc, kc], axis=1) * kd2).astype(BF16)
      u = lax.dot_general(k2, v_ref[rows, vcols], tn, preferred_element_type=F32)
      u_s[0, hh, j] = u[0:RET_DK]
      u_s[1, hh, j] = u[RET_DK:]

    for d, order in ((0, range(n)), (1, reversed(range(n)))):
      cdec = cdec_s[hh, d]
      st = s0_ref[d, hh] if has_s0 else jnp.zeros((RET_DK, RET_DV), F32)
      for j in order:
        u = u_s[d, hh, j]
        u_s[d, hh, j] = st
        st = st * cdec + u
      if not has_s0:
        so_ref[d, hh] = st

    intra = dec_s[hh, 1][:, :c]
    qd_f = dec_s[hh, 2]
    qd_b = dec_s[hh, 3]
    for j in range(n):
      rows = slice(j * c, (j + 1) * c)
      qc = q_ref[rows, qcols]
      vc = v_ref[rows, vcols]
      s = lax.dot_general(qc, k_ref[rows, qcols], nt, preferred_element_type=F32) * intra
      st2 = jnp.concatenate([u_s[0, hh, j], u_s[1, hh, j]], axis=1).astype(BF16)
      inter = jnp.dot(qc, st2, preferred_element_type=F32)
      o = (jnp.dot(s.astype(BF16), vc, preferred_element_type=F32)
           + inter[:, :RET_DV] * qd_f + inter[:, RET_DV:] * qd_b)
      mu = jnp.mean(o, axis=-1, keepdims=True)
      dv = o - mu
      var = jnp.mean(dv * dv, axis=-1, keepdims=True)
      on = (dv * lax.rsqrt(var + EPS)) * ng_ref[:, vcols]
      y_ref[rows, vcols] = (on * _silu(gate_ref[rows, vcols].astype(F32))).astype(BF16)


def _retention(p, dl, ng, state0, prompt):
  seq_len = SEQ_P if prompt else SEQ_S
  nb = NB_P if prompt else NB_S
  nh = RET_HEADS_PER_STEP
  row0 = 0 if prompt else T_P // SEQ_S
  hb = RET_HEADS // nh
  kcol = hb
  vcol = (2 * RET_HEADS * RET_DK) // (nh * RET_DV)
  gcol = vcol + hb
  in_specs = [
      pl.BlockSpec((seq_len, nh * RET_DK), lambda h, b: (row0 + b, h)),
      pl.BlockSpec((seq_len, nh * RET_DK), lambda h, b: (row0 + b, kcol + h)),
      pl.BlockSpec((seq_len, nh * RET_DV), lambda h, b: (row0 + b, vcol + h)),
      pl.BlockSpec((seq_len, nh * RET_DV), lambda h, b: (row0 + b, gcol + h)),
      pl.BlockSpec((nh, 2, 1, RET_DK), lambda h, b: (h, 0, 0, 0)),
      pl.BlockSpec((1, nh * RET_DV), lambda h, b: (0, h)),
  ]
  args = [p, p, p, p, dl, ng]
  y_spec = pl.BlockSpec((seq_len, nh * RET_DV), lambda h, b: (b, h))
  y_shape = jax.ShapeDtypeStruct((nb * seq_len, RET_HEADS * RET_DV), BF16)
  state_spec = pl.BlockSpec((None, 2, nh, RET_DK, RET_DV), lambda h, b: (b, 0, h, 0, 0))
  if prompt:
    out_specs = (y_spec, state_spec)
    out_shape = (y_shape, jax.ShapeDtypeStruct((NB_P, 2, RET_HEADS, RET_DK, RET_DV), F32))
  else:
    in_specs.append(state_spec)
    args.append(state0)
    out_specs = y_spec
    out_shape = y_shape
  return pl.pallas_call(
      functools.partial(_ret_kernel, seq_len=seq_len, has_s0=not prompt, nh=nh),
      grid=(hb, nb),
      in_specs=in_specs,
      out_specs=out_specs,
      out_shape=out_shape,
      scratch_shapes=[
          pltpu.VMEM((2, nh, seq_len // RET_CHUNK, RET_DK, RET_DV), F32),
          pltpu.VMEM((nh, 4, RET_CHUNK, RET_DV), F32),
          pltpu.VMEM((nh, 2, 1, RET_DV), F32),
      ],
      compiler_params=_params(("arbitrary", "arbitrary")),
      name="ret_prompt" if prompt else "ret_latent",
  )(*args)


def _router_kernel(x_ref, g_ref, sh_ref, sc_ref, wrh_ref, wrl_ref, br_ref, triu_ref,
                   info_ref, cnt_ref, carry_s):
  @pl.when(pl.program_id(0) == 0)
  def _():
    carry_s[...] = jnp.zeros_like(carry_s)

  h = _norm_mod(x_ref[...], g_ref[...], sh_ref[...], sc_ref[...])
  logits = _router_logits(h, wrh_ref, wrl_ref, br_ref)
  lt = logits.T[0:ROUTER_ROWS]
  row = lax.broadcasted_iota(jnp.int32, lt.shape, 0)
  big = jnp.int32(ROUTER_ROWS)

  def first_max(v):
    m = jnp.max(v, axis=0, keepdims=True)
    return jnp.min(jnp.where(v == m, row, big), axis=0, keepdims=True)

  gidx = first_max(jnp.where(row < N_GROUPS, lt, NEG))
  lo = N_GROUPS + EPG * gidx
  le = jnp.where((row >= lo) & (row < lo + EPG), lt, NEG)
  i1 = first_max(le)
  i2 = first_max(jnp.where(row == i1, NEG, le))
  e_lo = jnp.minimum(i1, i2) - N_GROUPS
  e_hi = jnp.maximum(i1, i2) - N_GROUPS
  a = e_lo - EPG * gidx
  b = e_hi - EPG * gidx
  pair_base = jnp.where(a == 0, 0, jnp.where(a == 1, 3, 5))
  cls = N_PAIRS * gidx + pair_base + (b - a - 1)

  onehot = jnp.where(row == cls, 1.0, 0.0)
  before = jnp.dot(onehot.astype(BF16), triu_ref[...], preferred_element_type=F32) + carry_s[...]
  rank = jnp.sum(jnp.where(row == cls, before, 0.0), axis=0, keepdims=True)
  carry_s[...] = carry_s[...] + jnp.sum(onehot, axis=1, keepdims=True)
  cnt_ref[...] = carry_s[...]

  row8 = lax.broadcasted_iota(jnp.int32, info_ref.shape, 0)
  info_ref[...] = jnp.where(row8 == 0, cls.astype(F32), jnp.where(row8 == 1, rank, 0.0))


def _router(x, g, shift, scale, wr_hi, wr_lo, br, tri):
  return pl.pallas_call(
      _router_kernel,
      grid=(T // TM,),
      in_specs=[
          pl.BlockSpec((TM, D), lambda i: (i, 0)),
          pl.BlockSpec((1, D), lambda i: (0, 0)),
          pl.BlockSpec((None, 1, D), lambda i: (_seg(i, TM), 0, 0)),
          pl.BlockSpec((None, 1, D), lambda i: (_seg(i, TM), 0, 0)),
          pl.BlockSpec((D, 128), lambda i: (0, 0)),
          pl.BlockSpec((D, 128), lambda i: (0, 0)),
          pl.BlockSpec((1, 128), lambda i: (0, 0)),
          pl.BlockSpec((TM, TM), lambda i: (0, 0)),
      ],
      out_specs=(
          pl.BlockSpec((None, 8, TM), lambda i: (i, 0, 0)),
          pl.BlockSpec((ROUTER_ROWS, TM), lambda i: (0, 0)),
      ),
      out_shape=(
          jax.ShapeDtypeStruct((T // TM, 8, TM), F32),
          jax.ShapeDtypeStruct((ROUTER_ROWS, TM), F32),
      ),
      scratch_shapes=[pltpu.VMEM((ROUTER_ROWS, TM), F32)],
      compiler_params=_params(("arbitrary",)),
      name="router",
  )(x, g, shift, scale, wr_hi, wr_lo, br, tri)


def _slot_of(slot_ref, token):
  shift = TM.bit_length() - 1
  assert TM == 1 << shift
  return slot_ref[lax.shift_right_logical(token, jnp.int32(shift)), token & (TM - 1)]


def _tile_rows(ref, row):
  return ref.at[pl.ds(pl.multiple_of(row * ROW_SUB, ROW_SUB), ROW_SUB), :]


def _scatter_copy(slot_ref, step, r, stage, buf, hs_hbm, sem):
  return pltpu.make_async_copy(
      stage.at[buf, pl.ds(r * ROW_SUB, ROW_SUB), :],
      _tile_rows(hs_hbm, _slot_of(slot_ref, step * TS + r)), sem.at[buf])


def _scatter_kernel(slot_ref, pstart_ref, pn_ref, nv_ref,
                    x_ref, g_ref, sh_ref, sc_ref, hs_hbm,
                    stage, zero_s, sem, zsem):
  i = pl.program_id(0)
  n = pl.num_programs(0)
  buf = i % 2

  def pad_copy(c, r):
    return pltpu.make_async_copy(zero_s, _tile_rows(hs_hbm, pstart_ref[c] + r), zsem.at[0])

  def tail_copy(t):
    rows = TMM * ROW_SUB
    return pltpu.make_async_copy(
        stage.at[1], hs_hbm.at[pl.ds(pl.multiple_of(t * rows, rows), rows), :], zsem.at[1])

  @pl.when(i == 0)
  def _():
    stage[...] = jnp.zeros_like(stage)
    zero_s[...] = jnp.zeros_like(zero_s)
    for c in range(N_CLASSES):
      def start(r, carry, c=c):
        pad_copy(c, r).start()
        return carry
      lax.fori_loop(0, pn_ref[c], start, 0)

    def tail_start(t, carry):
      tail_copy(t).start()
      return carry
    lax.fori_loop(nv_ref[0], NT_E, tail_start, 0)
    for c in range(N_CLASSES):
      def wait(r, carry, c=c):
        pad_copy(c, r).wait()
        return carry
      lax.fori_loop(0, pn_ref[c], wait, 0)

    def tail_wait(t, carry):
      tail_copy(t).wait()
      return carry
    lax.fori_loop(nv_ref[0], NT_E, tail_wait, 0)

  h = _norm_mod(x_ref[...], g_ref[...], sh_ref[...], sc_ref[...])
  for s in range(ROW_SUB):
    stage[buf, pl.ds(s, TS, stride=ROW_SUB), :] = h[:, s * 128:(s + 1) * 128]

  def start(r2, carry):
    for q in range(DMA_QUEUES):
      _scatter_copy(slot_ref, i, r2 * DMA_QUEUES + q, stage, buf, hs_hbm, sem).start(priority=q)
    return carry
  lax.fori_loop(0, TS // DMA_QUEUES, start, 0, unroll=4)

  @pl.when(i > 0)
  def _():
    def wait(r, carry):
      _scatter_copy(slot_ref, i - 1, r, stage, 1 - buf, hs_hbm, sem).wait()
      return carry
    lax.fori_loop(0, TS, wait, 0, unroll=8)

  @pl.when(i == n - 1)
  def _():
    def wait(r, carry):
      _scatter_copy(slot_ref, i, r, stage, buf, hs_hbm, sem).wait()
      return carry
    lax.fori_loop(0, TS, wait, 0, unroll=8)


def _scatter_rows(slot, pad_start, pad_n, n_valid, x, g, shift, scale):
  assert TS == TMM
  grid_spec = pltpu.PrefetchScalarGridSpec(
      num_scalar_prefetch=4,
      grid=(T // TS,),
      in_specs=[
          pl.BlockSpec((TS, D), lambda i, *_: (i, 0)),
          pl.BlockSpec((1, D), lambda i, *_: (0, 0)),
          pl.BlockSpec((None, 1, D), lambda i, *_: (_seg(i, TS), 0, 0)),
          pl.BlockSpec((None, 1, D), lambda i, *_: (_seg(i, TS), 0, 0)),
      ],
      out_specs=pl.BlockSpec(memory_space=pl.ANY),
      scratch_shapes=[
          pltpu.VMEM((2, TS * ROW_SUB, 128), F32),
          pltpu.VMEM((ROW_SUB, 128), F32),
          pltpu.SemaphoreType.DMA((2,)),
          pltpu.SemaphoreType.DMA((2,)),
      ],
  )
  return pl.pallas_call(
      _scatter_kernel,
      grid_spec=grid_spec,
      out_shape=jax.ShapeDtypeStruct((P_PAD * ROW_SUB, 128), F32),
      compiler_params=_params(("arbitrary",)),
      name="scatter_rows",
  )(slot, pad_start, pad_n, n_valid, x, g, shift, scale)


def _router_logits(h, wrh_ref, wrl_ref, br_ref):
  h_hi = h.astype(BF16)
  h_lo = (h - h_hi.astype(F32)).astype(BF16)
  dot = functools.partial(jnp.dot, preferred_element_type=F32)
  return (dot(h_hi, wrh_ref[...]) + dot(h_lo, wrh_ref[...]) + dot(h_hi, wrl_ref[...])
          + br_ref[...])


def _expert_kernel(elo_ref, ehi_ref, nv_ref, hs_ref, wrh_ref, wrl_ref, br_ref,
                   wg_lo, wg_hi, wu_lo, wu_hi, wd_lo, wd_hi, y_ref):
  i = pl.program_id(0)

  @pl.when(i < nv_ref[0])
  def _():
    h = jnp.concatenate(
        [hs_ref[pl.ds(s, TMM, stride=ROW_SUB), :] for s in range(ROW_SUB)], axis=1)
    hb = h.astype(BF16)
    logits = _router_logits(h, wrh_ref, wrl_ref, br_ref)
    lane = lax.broadcasted_iota(jnp.int32, logits.shape, 1)
    elo, ehi = elo_ref[i], ehi_ref[i]

    def pick(idx):
      return jnp.sum(jnp.where(lane == idx, logits, 0.0), axis=-1, keepdims=True)
    l_g, l_lo, l_hi = pick(elo // EPG), pick(N_GROUPS + elo), pick(N_GROUPS + ehi)
    p_top = 1.0 / jnp.sum(jnp.where(lane < N_GROUPS, jnp.exp(logits - l_g), 0.0),
                          axis=-1, keepdims=True)
    wl = jnp.broadcast_to(p_top / (1.0 + jnp.exp(l_hi - l_lo)), (TMM, D_EXPERT))
    wh = jnp.broadcast_to(p_top / (1.0 + jnp.exp(l_lo - l_hi)), (TMM, D_EXPERT))
    dot = functools.partial(jnp.dot, preferred_element_type=F32)
    a_lo = (_silu(dot(hb, wg_lo[...])) * dot(hb, wu_lo[...])) * wl
    a_hi = (_silu(dot(hb, wg_hi[...])) * dot(hb, wu_hi[...])) * wh
    y = dot(a_lo.astype(BF16), wd_lo[...]) + dot(a_hi.astype(BF16), wd_hi[...])
    for s in range(ROW_SUB):
      y_ref[pl.ds(s, TMM, stride=ROW_SUB), :] = y[:, s * 128:(s + 1) * 128]

  @pl.when(i >= nv_ref[0])
  def _():
    y_ref[...] = jnp.zeros_like(y_ref)


def _experts(tile_elo, tile_ehi, n_valid, hs, wr_hi, wr_lo, br, w_gate, w_up, w_down):
  up_spec_lo = pl.BlockSpec((None, D, D_EXPERT), lambda i, elo, ehi, nv: (elo[i], 0, 0))
  up_spec_hi = pl.BlockSpec((None, D, D_EXPERT), lambda i, elo, ehi, nv: (ehi[i], 0, 0))
  dn_spec_lo = pl.BlockSpec((None, D_EXPERT, D), lambda i, elo, ehi, nv: (elo[i], 0, 0))
  dn_spec_hi = pl.BlockSpec((None, D_EXPERT, D), lambda i, elo, ehi, nv: (ehi[i], 0, 0))
  grid_spec = pltpu.PrefetchScalarGridSpec(
      num_scalar_prefetch=3,
      grid=(NT_E,),
      in_specs=[
          pl.BlockSpec((TMM * ROW_SUB, 128), lambda i, elo, ehi, nv: (i, 0)),
          pl.BlockSpec((D, 128), lambda i, elo, ehi, nv: (0, 0)),
          pl.BlockSpec((D, 128), lambda i, elo, ehi, nv: (0, 0)),
          pl.BlockSpec((1, 128), lambda i, elo, ehi, nv: (0, 0)),
          up_spec_lo, up_spec_hi, up_spec_lo, up_spec_hi, dn_spec_lo, dn_spec_hi,
      ],
      out_specs=pl.BlockSpec((TMM * ROW_SUB, 128), lambda i, elo, ehi, nv: (i, 0)),
  )
  return pl.pallas_call(
      _expert_kernel,
      grid_spec=grid_spec,
      out_shape=jax.ShapeDtypeStruct((P_PAD * ROW_SUB, 128), F32),
      compiler_params=_params(("arbitrary",)),
      name="experts",
  )(tile_elo, tile_ehi, n_valid, hs, wr_hi, wr_lo, br,
    w_gate, w_gate, w_up, w_up, w_down, w_down)


def _gather_copy(slot_ref, step, r, y_hbm, ybuf, buf, sem):
  return pltpu.make_async_copy(
      _tile_rows(y_hbm, _slot_of(slot_ref, step * TC + r)),
      ybuf.at[buf, pl.ds(r * ROW_SUB, ROW_SUB), :], sem.at[buf])


def _combine_kernel(slot_ref, x_ref, gate_ref, nf_ref, y_hbm, *rest, final):
  if final:
    op_ref, os_ref, ybuf, sem = rest
  else:
    o_ref, ybuf, sem = rest
  i = pl.program_id(0)
  n = pl.num_programs(0)
  buf = i % 2

  def start(step, b):
    def body(r2, carry):
      for q in range(DMA_QUEUES):
        _gather_copy(slot_ref, step, r2 * DMA_QUEUES + q, y_hbm, ybuf, b, sem).start(priority=q)
      return carry
    lax.fori_loop(0, TC // DMA_QUEUES, body, 0, unroll=4)

  @pl.when(i == 0)
  def _():
    start(0, 0)

  @pl.when(i + 1 < n)
  def _():
    start(i + 1, 1 - buf)

  def wait(r, carry):
    _gather_copy(slot_ref, i, r, y_hbm, ybuf, buf, sem).wait()
    return carry
  lax.fori_loop(0, TC, wait, 0, unroll=8)

  y = jnp.concatenate(
      [ybuf[buf, pl.ds(s, TC, stride=ROW_SUB), :] for s in range(ROW_SUB)], axis=1)
  x = x_ref[...] + gate_ref[...] * y
  if not final:
    o_ref[...] = x
    return
  x = (x * lax.rsqrt(jnp.mean(x * x, axis=-1, keepdims=True) + EPS)) * nf_ref[...]
  is_prompt = i < T_P // TC

  @pl.when(is_prompt)
  def _():
    op_ref[...] = x

  @pl.when(jnp.logical_not(is_prompt))
  def _():
    os_ref[...] = x


def _combine(slot, x, gate, norm_f, y_sorted, final):
  npt = T_P // TC
  if final:
    out_specs = (pl.BlockSpec((TC, D), lambda i, s: (jnp.minimum(i, npt - 1), 0)),
                 pl.BlockSpec((TC, D), lambda i, s: (jnp.maximum(i - npt, 0), 0)))
    out_shape = (jax.ShapeDtypeStruct((T_P, D), F32), jax.ShapeDtypeStruct((T_S, D), F32))
  else:
    out_specs = pl.BlockSpec((TC, D), lambda i, s: (i, 0))
    out_shape = jax.ShapeDtypeStruct((T, D), F32)
  grid_spec = pltpu.PrefetchScalarGridSpec(
      num_scalar_prefetch=1,
      grid=(T // TC,),
      in_specs=[
          pl.BlockSpec((TC, D), lambda i, s: (i, 0)),
          pl.BlockSpec((None, 1, D), lambda i, s: (_seg(i, TC), 0, 0)),
          pl.BlockSpec((1, D), lambda i, s: (0, 0)),
          pl.BlockSpec(memory_space=pl.ANY),
      ],
      out_specs=out_specs,
      scratch_shapes=[pltpu.VMEM((2, TC * ROW_SUB, 128), F32), pltpu.SemaphoreType.DMA((2,))],
  )
  return pl.pallas_call(
      functools.partial(_combine_kernel, final=final),
      grid_spec=grid_spec,
      out_shape=out_shape,
      compiler_params=_params(("arbitrary",)),
      name="combine_final" if final else "combine",
  )(slot, x, gate, norm_f, y_sorted)


def _class_experts():
  lo, hi = [], []
  for g in range(N_GROUPS):
    for a in range(EPG):
      for b in range(a + 1, EPG):
        lo.append(g * EPG + a)
        hi.append(g * EPG + b)
  return np.asarray(lo, np.int32), np.asarray(hi, np.int32)


def _moe_layer(x, g, shift, scale, gate, wr_hi, wr_lo, br, tri, w_gate, w_up, w_down, norm_f, final):
  info, cnt = _router(x, g, shift, scale, wr_hi, wr_lo, br, tri)

  cls = info[:, 0, :].astype(jnp.int32)
  rank = info[:, 1, :].astype(jnp.int32)
  counts = cnt[:N_CLASSES, 0].astype(jnp.int32)
  tiles = (counts + TMM - 1) // TMM
  tile_end = jnp.cumsum(tiles)
  offs = (tile_end - tiles) * TMM
  slot = rank
  for k in range(N_CLASSES):
    slot = slot + jnp.where(cls == k, offs[k], 0)
  n_valid = tile_end[-1]
  tile_ids = jnp.minimum(jnp.arange(NT_E, dtype=jnp.int32), n_valid - 1)
  tile_cls = jnp.sum((tile_ids[:, None] >= tile_end[None, :]).astype(jnp.int32), axis=1)
  tile_cls = jnp.minimum(tile_cls, N_CLASSES - 1)
  cls_lo, cls_hi = _class_experts()
  tile_elo = jnp.asarray(cls_lo)[tile_cls]
  tile_ehi = jnp.asarray(cls_hi)[tile_cls]

  n_valid = n_valid.reshape(1)
  hs = _scatter_rows(slot, offs + counts, tiles * TMM - counts, n_valid, x, g, shift, scale)
  y_sorted = _experts(tile_elo, tile_ehi, n_valid, hs, wr_hi, wr_lo, br, w_gate, w_up, w_down)
  return _combine(slot, x, gate, norm_f, y_sorted, final)


def kernel(x_prompt, x_sample, cache_k, cache_v, state_ret, c, c_ctx, norm1, norm2, w_ada, b_ada, attn_w_qkv, attn_q_norm, attn_k_norm, attn_w_o, pool_w, pool_scale, ret_w_in, ret_decay_logit, ret_norm, ret_w_out, moe_w_router_g, moe_b_router_g, moe_w_router_e, moe_b_router_e, moe_w_gate, moe_w_up, moe_w_down, norm_f):
  x = jnp.concatenate([x_prompt.reshape(T_P, D), x_sample.reshape(T_S, D)], axis=0)
  cond8 = jnp.concatenate([c_ctx[None, :], c, jnp.zeros((N_SEG - 1 - NB_S, D), F32)], axis=0)
  mods = _adaln(cond8, w_ada, b_ada)
  mods = mods.reshape(DEPTH, N_SEG, 6, 1, D).transpose(0, 2, 1, 3, 4)
  rope = _rope_tables()
  tri = jnp.triu(jnp.ones((TM, TM), BF16), 1)
  pad_r = 128 - N_GROUPS - N_EXPERTS
  norm_f2 = norm_f.reshape(1, D)

  new_k, new_v, new_s = [], [], []
  for i in range(DEPTH):
    kind, j = i % 3, i // 3
    m = mods[i]
    g1 = norm1[i].reshape(1, D)
    g2 = norm2[i].reshape(1, D)
    if kind == 0:
      qkv = _nm_matmul(x, g1, m[0], m[1], attn_w_qkv[j].astype(BF16), TM, "qkv_proj")
      qn = attn_q_norm[j].reshape(1, HEAD_DIM)
      kn = attn_k_norm[j].reshape(1, HEAD_DIM)
      o_p, kc, vc = _attn_prompt(qkv, qn, kn)
      ck = cache_k[:, j].reshape(NB_S, PAST, N_KV * HEAD_DIM)
      cv = cache_v[:, j].reshape(NB_S, PAST, N_KV * HEAD_DIM)
      o_s = _attn_latent(qkv, ck, cv, qn, kn, rope)
      new_k.append(kc.reshape(NB_P, SEQ_P, N_KV, HEAD_DIM))
      new_v.append(vc.reshape(NB_P, SEQ_P, N_KV, HEAD_DIM))
      x = _mm_res(o_p, o_s, attn_w_o[j].astype(BF16), x, m[2], "attn_out")
    elif kind == 1:
      x = _pool_layer(x, g1, m[0], m[1], m[2], pool_w[j].astype(BF16), pool_scale[j].reshape(1, D))
    else:
      p = _nm_matmul(x, g1, m[0], m[1], ret_w_in[j].astype(BF16), 256, "ret_proj")
      dl = jnp.broadcast_to(ret_decay_logit[j].T[:, :, None, None], (RET_HEADS, 2, 1, RET_DK))
      ng = ret_norm[j].reshape(1, RET_HEADS * RET_DV)
      y_p, s_new = _retention(p, dl, ng, None, True)
      y_s = _retention(p, dl, ng, state_ret[:, j], False)
      new_s.append(s_new)
      x = _mm_res(y_p, y_s, ret_w_out[j].astype(BF16), x, m[2], "ret_out")
    wr = jnp.concatenate([moe_w_router_g[i], moe_w_router_e[i], jnp.zeros((D, pad_r), F32)], axis=1)
    br = jnp.concatenate([moe_b_router_g[i], moe_b_router_e[i], jnp.zeros((pad_r,), F32)]).reshape(1, 128)
    wr_hi = wr.astype(BF16)
    wr_lo = (wr - wr_hi.astype(F32)).astype(BF16)
    x = _moe_layer(x, g2, m[3], m[4], m[5], wr_hi, wr_lo, br, tri,
                   moe_w_gate[i].astype(BF16), moe_w_up[i].astype(BF16), moe_w_down[i].astype(BF16),
                   norm_f2, i == DEPTH - 1)

  y_prompt = x[0].reshape(NB_P, SEQ_P, D)
  y_sample = x[1].reshape(NB_S, SEQ_S, D)
  new_cache_k = jnp.stack(new_k, axis=1)
  new_cache_v = jnp.stack(new_v, axis=1)
  assert len(new_s) == 1
  new_state_ret = new_s[0].reshape(NB_P, 1, 2, RET_HEADS, RET_DK, RET_DV)
  return (y_prompt, y_sample, new_cache_k, new_cache_v, new_state_ret)
```

```python
import functools

import jax
import jax.numpy as jnp
import numpy as np
from jax import lax
from jax.experimental import pallas as pl
from jax.experimental.pallas import tpu as pltpu

F32 = jnp.float32
BF16 = jnp.bfloat16

D = 1024
NB_P, SEQ_P = 32, 256
NB_S, SEQ_S = 4, 2048
T_P = NB_P * SEQ_P
T_S = NB_S * SEQ_S
T = T_P + T_S
DEPTH = 4
GRID_W = 64
HEAD_DIM = 128
N_HEADS = 8
N_KV = 2
KV_GROUP = N_HEADS // N_KV
PAST = 256
ROPE_THETA = 10000.0
POOL_WINDOWS = (2, 4, 8, 16)
POOL_GROUP = D // 4
POOL_HALO = 8
RET_HEADS = 8
RET_DK = 128
RET_DV = 256
RET_CHUNK = 128
RET_HEADS_PER_STEP = 2
N_GROUPS = 4
EPG = 4
N_EXPERTS = 16
D_EXPERT = 256
N_PAIRS = 6
N_CLASSES = N_GROUPS * N_PAIRS
ROUTER_ROWS = 32
EPS = 1e-6
LOG2E = 1.4426950408889634
NEG = -1e30
N_SEG = 8

VMEM_LIMIT_BYTES = 52 * 1024 * 1024

TM = 512
TQ = 128
ATTN_KEY_CHUNK = 256
ATTN_HEAD_STACK = 2
TP = 256
TMM = 256
TS = 256
TC = 256
ROW_SUB = 8
ROW_GATHER_PRIORITY = 1
PACK_SUB = D // 2 // 128
INFO_SUB = PACK_SUB
P_PAD = T + N_CLASSES * TMM
NT_E = P_PAD // TMM


def _params(sem):
  return pltpu.CompilerParams(dimension_semantics=sem, vmem_limit_bytes=VMEM_LIMIT_BYTES)


def _seg(i, tm):
  npt = T_P // tm
  return jnp.where(i < npt, 0, (i - npt) // (SEQ_S // tm) + 1)


def _norm_mod(x, g, shift, scale):
  r = lax.rsqrt(jnp.mean(x * x, axis=-1, keepdims=True) + EPS)
  return ((x * r) * g) * (1.0 + scale) + shift


def _silu(x):
  return x * (1.0 / (1.0 + jnp.exp(-x)))


def _adaln_kernel(c_ref, w_ref, b_ref, o_ref):
  s = _silu(c_ref[...]).astype(BF16)
  o_ref[...] = jnp.dot(s, w_ref[...].astype(BF16), preferred_element_type=F32) + b_ref[...]


def _adaln(cond8, w_ada, b_ada):
  tn = 1536
  return pl.pallas_call(
      _adaln_kernel,
      grid=(DEPTH, 6 * D // tn),
      in_specs=[
          pl.BlockSpec((N_SEG, D), lambda l, j: (0, 0)),
          pl.BlockSpec((None, D, tn), lambda l, j: (l, 0, j)),
          pl.BlockSpec((None, 1, tn), lambda l, j: (l, 0, j)),
      ],
      out_specs=pl.BlockSpec((None, N_SEG, tn), lambda l, j: (l, 0, j)),
      out_shape=jax.ShapeDtypeStruct((DEPTH, N_SEG, 6 * D), F32),
      compiler_params=_params(("arbitrary", "arbitrary")),
      name="adaln",
  )(cond8, w_ada, b_ada.reshape(DEPTH, 1, 6 * D))


def _token_rows(x, tm, width):
  npt = T_P // tm
  if isinstance(x, tuple):
    arrays = x
    latent_map = lambda i: (jnp.maximum(i - npt, 0), 0)
  else:
    arrays = (x, x)
    latent_map = lambda i: (jnp.maximum(i, npt), 0)
  specs = [pl.BlockSpec((tm, width), lambda i: (jnp.minimum(i, npt - 1), 0)),
           pl.BlockSpec((tm, width), latent_map)]
  return arrays, specs


def _pick_rows(p_ref, s_ref, tm):
  return jnp.where(pl.program_id(0) < T_P // tm, p_ref[...], s_ref[...])


def _nm_matmul_kernel(xp_ref, xs_ref, g_ref, sh_ref, sc_ref, w_ref, o_ref, *, tm, n_chunk):
  x = _pick_rows(xp_ref, xs_ref, tm)
  h = _norm_mod(x, g_ref[...], sh_ref[...], sc_ref[...]).astype(BF16)
  n = w_ref.shape[1]
  for c in range(0, n, n_chunk):
    o_ref[:, c:c + n_chunk] = jnp.dot(
        h, w_ref[:, c:c + n_chunk], preferred_element_type=F32).astype(o_ref.dtype)


def _nm_matmul(x, g, shift, scale, w, tm, name):
  n = w.shape[1]
  x_arrays, x_specs = _token_rows(x, tm, D)
  return pl.pallas_call(
      functools.partial(_nm_matmul_kernel, tm=tm, n_chunk=512),
      grid=(T // tm,),
      in_specs=x_specs + [
          pl.BlockSpec((1, D), lambda i: (0, 0)),
          pl.BlockSpec((None, 1, D), lambda i: (_seg(i, tm), 0, 0)),
          pl.BlockSpec((None, 1, D), lambda i: (_seg(i, tm), 0, 0)),
          pl.BlockSpec((D, n), lambda i: (0, 0)),
      ],
      out_specs=pl.BlockSpec((tm, n), lambda i: (i, 0)),
      out_shape=jax.ShapeDtypeStruct((T, n), BF16),
      compiler_params=_params(("arbitrary",)),
      name=name,
  )(*x_arrays, g, shift, scale, w)


def _mm_res_kernel(ap_ref, as_ref, xp_ref, xs_ref, w_ref, gate_ref, o_ref):
  a = _pick_rows(ap_ref, as_ref, TM)
  x = _pick_rows(xp_ref, xs_ref, TM)
  o_ref[...] = x + gate_ref[...] * jnp.dot(a, w_ref[...], preferred_element_type=F32)


def _mm_res(a, w, x, gate, name):
  k = w.shape[0]
  a_arrays, a_specs = _token_rows(a, TM, k)
  x_arrays, x_specs = _token_rows(x, TM, D)
  return pl.pallas_call(
      _mm_res_kernel,
      grid=(T // TM,),
      in_specs=a_specs + x_specs + [
          pl.BlockSpec((k, D), lambda i: (0, 0)),
          pl.BlockSpec((None, 1, D), lambda i: (_seg(i, TM), 0, 0)),
      ],
      out_specs=pl.BlockSpec((TM, D), lambda i: (i, 0)),
      out_shape=jax.ShapeDtypeStruct((T, D), F32),
      compiler_params=_params(("arbitrary",)),
      name=name,
  )(*a_arrays, *x_arrays, w, gate)


def _rope(x, c, a, b):
  return x * c + pltpu.roll(x, 96, 1) * a + pltpu.roll(x, 32, 1) * b


def _head_norm(x, w):
  return (x * lax.rsqrt(jnp.mean(x * x, axis=-1, keepdims=True) + EPS)) * w


def _attn_latent_kernel(q_ref, kn_ref, vn_ref, ck_ref, cv_ref, qw_ref, kw_ref,
                        cq_ref, aq_ref, bq_ref, ckk_ref, akk_ref, bkk_ref,
                        o_ref, k_s, vt_s, *st_refs, tq):
  @pl.when(pl.program_id(2) == 0)
  def _():
    k = _rope(_head_norm(kn_ref[...].astype(F32), kw_ref[...]), ckk_ref[...], akk_ref[...], bkk_ref[...])
    k_s[0:PAST, :] = ck_ref[...].astype(BF16)
    k_s[PAST:, :] = k.astype(BF16)
    vt_s[:, 0:PAST] = cv_ref[...].T.astype(BF16)
    vt_s[:, PAST:] = vn_ref[...].astype(F32).T.astype(BF16)

  qa = q_ref[...].astype(F32)
  lk = k_s.shape[0]

  def prep_q(h):
    qh = _head_norm(qa[:, h * HEAD_DIM:(h + 1) * HEAD_DIM], qw_ref[...])
    qh = _rope(qh, cq_ref[...], aq_ref[...], bq_ref[...])
    return (qh * (HEAD_DIM ** -0.5 * LOG2E)).astype(BF16)

  nq = ATTN_HEAD_STACK * tq
  for h0 in range(0, KV_GROUP, ATTN_HEAD_STACK):
    qp = jnp.concatenate([prep_q(h0 + d) for d in range(ATTN_HEAD_STACK)], axis=0)
    st_s = st_refs[h0 // ATTN_HEAD_STACK]
    chunks = [(c0, min(c0 + ATTN_KEY_CHUNK, lk)) for c0 in range(0, lk, ATTN_KEY_CHUNK)]
    m = jnp.full((1, nq), NEG, F32)
    for c0, c1 in chunks:
      st = lax.dot_general(k_s[c0:c1, :], qp, (((1,), (1,)), ((), ())),
                           preferred_element_type=F32)
      st_s[c0:c1, :] = st
      m = jnp.maximum(m, jnp.max(st, axis=0, keepdims=True))
    l = jnp.zeros((1, nq), F32)
    acc = jnp.zeros((HEAD_DIM, nq), F32)
    for c0, c1 in chunks:
      pt = jnp.exp2(st_s[c0:c1, :] - m)
      l = l + jnp.sum(pt, axis=0, keepdims=True)
      acc = acc + jnp.dot(vt_s[:, c0:c1], pt.astype(BF16), preferred_element_type=F32)
    o = acc / l
    for d in range(ATTN_HEAD_STACK):
      h = h0 + d
      o_ref[:, h * HEAD_DIM:(h + 1) * HEAD_DIM] = o[:, d * tq:(d + 1) * tq].T.astype(BF16)


def _attn_scratch(lk):
  stacks = KV_GROUP // ATTN_HEAD_STACK
  return ([pltpu.VMEM((lk, HEAD_DIM), BF16), pltpu.VMEM((HEAD_DIM, lk), BF16)]
          + [pltpu.VMEM((lk, ATTN_HEAD_STACK * TQ), F32) for _ in range(stacks)])


def _rope_tables():
  rows = SEQ_S // GRID_W
  t_row = jnp.broadcast_to(jnp.arange(rows)[:, None], (rows, GRID_W)).reshape(-1)
  t_col = jnp.broadcast_to(jnp.arange(GRID_W)[None, :], (rows, GRID_W)).reshape(-1)
  nf = HEAD_DIM // 4
  inv = ROPE_THETA ** (-jnp.arange(nf, dtype=F32) / nf)
  ang_r = t_row.astype(F32)[:, None] * inv[None, :]
  ang_c = t_col.astype(F32)[:, None] * inv[None, :]
  cr, sr, cc, sc = jnp.cos(ang_r), jnp.sin(ang_r), jnp.cos(ang_c), jnp.sin(ang_c)
  z = jnp.zeros_like(sr)
  c = jnp.concatenate([cr, cr, cc, cc], axis=1)
  a = jnp.concatenate([-sr, z, -sc, z], axis=1)
  b = jnp.concatenate([z, sr, z, sc], axis=1)
  return c, a, b


def _attn_prompt_kernel(qkv_ref, qw_ref, kw_ref, o_ref, ko_ref, vo_ref):
  nq = N_HEADS * HEAD_DIM
  nk = N_KV * HEAD_DIM
  for kh in range(N_KV):
    kcols = slice(nq + kh * HEAD_DIM, nq + (kh + 1) * HEAD_DIM)
    vcols = slice(nq + nk + kh * HEAD_DIM, nq + nk + (kh + 1) * HEAD_DIM)
    k = _head_norm(qkv_ref[:, kcols].astype(F32), kw_ref[...])
    v = qkv_ref[:, vcols].astype(F32)
    ko_ref[pl.ds(kh, SEQ_P, stride=N_KV), :] = k
    vo_ref[pl.ds(kh, SEQ_P, stride=N_KV), :] = v
    kb = k.astype(BF16)
    vt = v.T.astype(BF16)
    for h in range(kh * KV_GROUP, (kh + 1) * KV_GROUP):
      hcols = slice(h * HEAD_DIM, (h + 1) * HEAD_DIM)
      qh = _head_norm(qkv_ref[:, hcols].astype(F32), qw_ref[...])
      qh = (qh * (HEAD_DIM ** -0.5 * LOG2E)).astype(BF16)
      st = lax.dot_general(kb, qh, (((1,), (1,)), ((), ())), preferred_element_type=F32)
      pt = jnp.exp2(st - jnp.max(st, axis=0, keepdims=True))
      l = jnp.sum(pt, axis=0, keepdims=True)
      acc = jnp.dot(vt, pt.astype(BF16), preferred_element_type=F32)
      o_ref[:, hcols] = (acc / l).T.astype(BF16)


def _attn_prompt(qkv, q_norm, k_norm):
  width = (N_HEADS + 2 * N_KV) * HEAD_DIM
  out_shapes = (
      jax.ShapeDtypeStruct((T_P, N_HEADS * HEAD_DIM), BF16),
      jax.ShapeDtypeStruct((NB_P, SEQ_P * N_KV, HEAD_DIM), F32),
      jax.ShapeDtypeStruct((NB_P, SEQ_P * N_KV, HEAD_DIM), F32),
  )
  return pl.pallas_call(
      _attn_prompt_kernel,
      grid=(NB_P,),
      in_specs=[
          pl.BlockSpec((SEQ_P, width), lambda b: (b, 0)),
          pl.BlockSpec((1, HEAD_DIM), lambda b: (0, 0)),
          pl.BlockSpec((1, HEAD_DIM), lambda b: (0, 0)),
      ],
      out_specs=(
          pl.BlockSpec((SEQ_P, N_HEADS * HEAD_DIM), lambda b: (b, 0)),
          pl.BlockSpec((None, SEQ_P * N_KV, HEAD_DIM), lambda b: (b, 0, 0)),
          pl.BlockSpec((None, SEQ_P * N_KV, HEAD_DIM), lambda b: (b, 0, 0)),
      ),
      out_shape=out_shapes,
      compiler_params=_params(("arbitrary",)),
      name="attn_prompt",
  )(qkv, q_norm, k_norm)


def _attn_latent(qkv, cache_k, cache_v, q_norm, k_norm, rope):
  kcol = N_HEADS
  vcol = kcol + N_KV
  nq = SEQ_S // TQ
  row0 = T_P // TQ
  seq0 = T_P // SEQ_S
  lk = PAST + SEQ_S
  c, a, b = rope
  tab_q = pl.BlockSpec((TQ, HEAD_DIM), lambda bb, h, i: (i, 0))
  tab_k = pl.BlockSpec((SEQ_S, HEAD_DIM), lambda bb, h, i: (0, 0))
  return pl.pallas_call(
      functools.partial(_attn_latent_kernel, tq=TQ),
      grid=(NB_S, N_KV, nq),
      in_specs=[
          pl.BlockSpec((TQ, KV_GROUP * HEAD_DIM), lambda bb, h, i: (row0 + bb * nq + i, h)),
          pl.BlockSpec((SEQ_S, HEAD_DIM), lambda bb, h, i: (seq0 + bb, kcol + h)),
          pl.BlockSpec((SEQ_S, HEAD_DIM), lambda bb, h, i: (seq0 + bb, vcol + h)),
          pl.BlockSpec((None, PAST, HEAD_DIM), lambda bb, h, i: (bb, 0, h)),
          pl.BlockSpec((None, PAST, HEAD_DIM), lambda bb, h, i: (bb, 0, h)),
          pl.BlockSpec((1, HEAD_DIM), lambda bb, h, i: (0, 0)),
          pl.BlockSpec((1, HEAD_DIM), lambda bb, h, i: (0, 0)),
          tab_q, tab_q, tab_q, tab_k, tab_k, tab_k,
      ],
      out_specs=pl.BlockSpec((TQ, KV_GROUP * HEAD_DIM), lambda bb, h, i: (bb * nq + i, h)),
      out_shape=jax.ShapeDtypeStruct((T_S, N_HEADS * HEAD_DIM), BF16),
      scratch_shapes=_attn_scratch(lk),
      compiler_params=_params(("arbitrary", "arbitrary", "arbitrary")),
      name="attn_latent",
  )(qkv, qkv, qkv, cache_k, cache_v, q_norm, k_norm, c, a, b, c, a, b)


def _pool_kernel(x_ref, xp_ref, xn_ref, g_ref, sh_ref, sc_ref, gate_ref, w_ref, ps_ref, o_ref):
  t = pl.program_id(0)
  npt = T_P // TP
  tiles_s = SEQ_S // TP
  is_p = t < npt
  pos = jnp.where(is_p, 0, (t - npt) % tiles_s)
  ntile = jnp.where(is_p, SEQ_P // TP, tiles_s)
  seq_len = ntile * TP
  keep_prev = jnp.where(pos == 0, 0.0, 1.0)
  keep_next = jnp.where(pos == ntile - 1, 0.0, 1.0)

  g, sh, sc = g_ref[...], sh_ref[...], sc_ref[...]
  x = x_ref[...]
  h = _norm_mod(x, g, sh, sc)
  hp = _norm_mod(xp_ref[...], g, sh, sc) * keep_prev
  hn = _norm_mod(xn_ref[...], g, sh, sc) * keep_next
  ext = jnp.concatenate([hp, h, hn], axis=0)
  n_ext = TP + 2 * POOL_HALO
  tseq = pos * TP + lax.broadcasted_iota(jnp.int32, (TP, POOL_GROUP), 0)

  outs = []
  for gi, win in enumerate(POOL_WINDOWS):
    lo, hi = gi * POOL_GROUP, (gi + 1) * POOL_GROUP
    acc = ext[:, lo:hi]
    span = 1
    while span < win:
      acc = acc + pltpu.roll(acc, n_ext - span, 0)
      span *= 2
    start = POOL_HALO - win // 2
    if start:
      acc = pltpu.roll(acc, n_ext - start, 0)
    ssum = acc[0:TP]
    cnt = (jnp.minimum(tseq + win // 2, seq_len) - jnp.maximum(tseq - win // 2, 0)).astype(F32)
    dlt = (ssum / cnt - h[:, lo:hi]).astype(BF16)
    outs.append(jnp.dot(dlt, w_ref[gi], preferred_element_type=F32))
  y = jnp.concatenate(outs, axis=1) * ps_ref[...]
  o_ref[...] = x + gate_ref[...] * y


def _pool_layer(x, g, shift, scale, gate, w, pscale):
  hb = TP // POOL_HALO
  last = T // POOL_HALO - 1
  seg = lambda i: _seg(i, TP)
  return pl.pallas_call(
      _pool_kernel,
      grid=(T // TP,),
      in_specs=[
          pl.BlockSpec((TP, D), lambda i: (i, 0)),
          pl.BlockSpec((POOL_HALO, D), lambda i: (jnp.maximum(i * hb - 1, 0), 0)),
          pl.BlockSpec((POOL_HALO, D), lambda i: (jnp.minimum((i + 1) * hb, last), 0)),
          pl.BlockSpec((1, D), lambda i: (0, 0)),
          pl.BlockSpec((None, 1, D), lambda i: (seg(i), 0, 0)),
          pl.BlockSpec((None, 1, D), lambda i: (seg(i), 0, 0)),
          pl.BlockSpec((None, 1, D), lambda i: (seg(i), 0, 0)),
          pl.BlockSpec((4, POOL_GROUP, POOL_GROUP), lambda i: (0, 0, 0)),
          pl.BlockSpec((1, D), lambda i: (0, 0)),
      ],
      out_specs=pl.BlockSpec((TP, D), lambda i: (i, 0)),
      out_shape=jax.ShapeDtypeStruct((T, D), F32),
      compiler_params=_params(("arbitrary",)),
      name="pool",
  )(x, x, x, g, shift, scale, gate, w, pscale)


def _ret_kernel(*refs, seq_len, has_s0, nh):
  it = iter(refs)
  q_ref, k_ref, v_ref, gate_ref, dl_ref, ng_ref = [next(it) for _ in range(6)]
  s0_ref = next(it) if has_s0 else None
  y_ref = next(it)
  so_ref = None if has_s0 else next(it)
  u_s = next(it)
  dec_s, cdec_s = next(it), next(it)
  c = RET_CHUNK
  n = seq_len // c
  kscale = RET_DK ** -0.5
  nt = (((1,), (1,)), ((), ()))
  tn = (((0,), (0,)), ((), ()))

  @pl.when(pl.program_id(1) == 0)
  def _():
    ri = lax.broadcasted_iota(jnp.int32, (c, c), 0).astype(F32)
    ci = lax.broadcasted_iota(jnp.int32, (c, c), 1).astype(F32)
    ri2 = lax.broadcasted_iota(jnp.int32, (c, RET_DV), 0).astype(F32)
    for hh in range(nh):
      lg = -jnp.log1p(jnp.exp(-dl_ref[hh]))
      lgf, lgb = lg[0], lg[1]
      lgf2 = jnp.concatenate([lgf, lgf], axis=1)
      lgb2 = jnp.concatenate([lgb, lgb], axis=1)
      dec_s[hh, 0] = jnp.concatenate(
          [jnp.exp((c - 1.0 - ri) * lgf), jnp.exp(ri * lgb)], axis=1) * kscale
      intra = (jnp.where(ri >= ci, jnp.exp((ri - ci) * lgf), 0.0)
               + jnp.where(ri <= ci, jnp.exp((ci - ri) * lgb), 0.0)) * kscale
      dec_s[hh, 1] = jnp.concatenate([intra, intra], axis=1)
      dec_s[hh, 2] = jnp.exp((ri2 + 1.0) * lgf2)
      dec_s[hh, 3] = jnp.exp((c - ri2) * lgb2)
      cdec_s[hh, 0] = jnp.exp(c * lgf2)
      cdec_s[hh, 1] = jnp.exp(c * lgb2)

  for hh in range(nh):
    qcols = slice(hh * RET_DK, (hh + 1) * RET_DK)
    vcols = slice(hh * RET_DV, (hh + 1) * RET_DV)
    kd2 = dec_s[hh, 0]

    for j in range(n):
      rows = slice(j * c, (j + 1) * c)
      kc = k_ref[rows, qcols].astype(F32)
      k2 = (jnp.concatenate([kc, kc], axis=1) * kd2).astype(BF16)
      u = lax.dot_general(k2, v_ref[rows, vcols], tn, preferred_element_type=F32)
      u_s[0, hh, j] = u[0:RET_DK]
      u_s[1, hh, j] = u[RET_DK:]

    for d, order in ((0, range(n)), (1, reversed(range(n)))):
      cdec = cdec_s[hh, d]
      st = s0_ref[d, hh] if has_s0 else jnp.zeros((RET_DK, RET_DV), F32)
      for j in order:
        u = u_s[d, hh, j]
        u_s[d, hh, j] = st
        st = st * cdec + u
      if not has_s0:
        so_ref[d, hh] = st

    intra = dec_s[hh, 1][:, :c]
    qd_f = dec_s[hh, 2]
    qd_b = dec_s[hh, 3]
    for j in range(n):
      rows = slice(j * c, (j + 1) * c)
      qc = q_ref[rows, qcols]
      vc = v_ref[rows, vcols]
      s = lax.dot_general(qc, k_ref[rows, qcols], nt, preferred_element_type=F32) * intra
      st2 = jnp.concatenate([u_s[0, hh, j], u_s[1, hh, j]], axis=1).astype(BF16)
      inter = jnp.dot(qc, st2, preferred_element_type=F32)
      o = (jnp.dot(s.astype(BF16), vc, preferred_element_type=F32)
           + inter[:, :RET_DV] * qd_f + inter[:, RET_DV:] * qd_b)
      mu = jnp.mean(o, axis=-1, keepdims=True)
      dv = o - mu
      var = jnp.mean(dv * dv, axis=-1, keepdims=True)
      on = (dv * lax.rsqrt(var + EPS)) * ng_ref[:, vcols]
      y_ref[rows, vcols] = (on * _silu(gate_ref[rows, vcols].astype(F32))).astype(BF16)


def _retention(p, dl, ng, state0, prompt):
  seq_len = SEQ_P if prompt else SEQ_S
  nb = NB_P if prompt else NB_S
  nh = RET_HEADS_PER_STEP
  row0 = 0 if prompt else T_P // SEQ_S
  hb = RET_HEADS // nh
  kcol = hb
  vcol = (2 * RET_HEADS * RET_DK) // (nh * RET_DV)
  gcol = vcol + hb
  in_specs = [
      pl.BlockSpec((seq_len, nh * RET_DK), lambda h, b: (row0 + b, h)),
      pl.BlockSpec((seq_len, nh * RET_DK), lambda h, b: (row0 + b, kcol + h)),
      pl.BlockSpec((seq_len, nh * RET_DV), lambda h, b: (row0 + b, vcol + h)),
      pl.BlockSpec((seq_len, nh * RET_DV), lambda h, b: (row0 + b, gcol + h)),
      pl.BlockSpec((nh, 2, 1, RET_DK), lambda h, b: (h, 0, 0, 0)),
      pl.BlockSpec((1, nh * RET_DV), lambda h, b: (0, h)),
  ]
  args = [p, p, p, p, dl, ng]
  y_spec = pl.BlockSpec((seq_len, nh * RET_DV), lambda h, b: (b, h))
  y_shape = jax.ShapeDtypeStruct((nb * seq_len, RET_HEADS * RET_DV), BF16)
  state_spec = pl.BlockSpec((None, 2, nh, RET_DK, RET_DV), lambda h, b: (b, 0, h, 0, 0))
  if prompt:
    out_specs = (y_spec, state_spec)
    out_shape = (y_shape, jax.ShapeDtypeStruct((NB_P, 2, RET_HEADS, RET_DK, RET_DV), F32))
  else:
    in_specs.append(state_spec)
    args.append(state0)
    out_specs = y_spec
    out_shape = y_shape
  return pl.pallas_call(
      functools.partial(_ret_kernel, seq_len=seq_len, has_s0=not prompt, nh=nh),
      grid=(hb, nb),
      in_specs=in_specs,
      out_specs=out_specs,
      out_shape=out_shape,
      scratch_shapes=[
          pltpu.VMEM((2, nh, seq_len // RET_CHUNK, RET_DK, RET_DV), F32),
          pltpu.VMEM((nh, 4, RET_CHUNK, RET_DV), F32),
          pltpu.VMEM((nh, 2, 1, RET_DV), F32),
      ],
      compiler_params=_params(("arbitrary", "arbitrary")),
      name="ret_prompt" if prompt else "ret_latent",
  )(*args)


def _router_kernel(x_ref, g_ref, sh_ref, sc_ref, wrh_ref, wrl_ref, br_ref, triu_ref,
                   info_ref, cnt_ref, carry_s):
  @pl.when(pl.program_id(0) == 0)
  def _():
    carry_s[...] = jnp.zeros_like(carry_s)

  h = _norm_mod(x_ref[...], g_ref[...], sh_ref[...], sc_ref[...])
  logits = _router_logits(h, wrh_ref, wrl_ref, br_ref)
  lt = logits.T[0:ROUTER_ROWS]
  row = lax.broadcasted_iota(jnp.int32, lt.shape, 0)
  big = jnp.int32(ROUTER_ROWS)

  def first_max(v):
    m = jnp.max(v, axis=0, keepdims=True)
    return jnp.min(jnp.where(v == m, row, big), axis=0, keepdims=True)

  gidx = first_max(jnp.where(row < N_GROUPS, lt, NEG))
  lo = N_GROUPS + EPG * gidx
  le = jnp.where((row >= lo) & (row < lo + EPG), lt, NEG)
  i1 = first_max(le)
  i2 = first_max(jnp.where(row == i1, NEG, le))
  e_lo = jnp.minimum(i1, i2) - N_GROUPS
  e_hi = jnp.maximum(i1, i2) - N_GROUPS
  a = e_lo - EPG * gidx
  b = e_hi - EPG * gidx
  pair_base = jnp.where(a == 0, 0, jnp.where(a == 1, 3, 5))
  cls = N_PAIRS * gidx + pair_base + (b - a - 1)

  onehot = jnp.where(row == cls, 1.0, 0.0)
  before = jnp.dot(onehot.astype(BF16), triu_ref[...], preferred_element_type=F32) + carry_s[...]
  rank = jnp.sum(jnp.where(row == cls, before, 0.0), axis=0, keepdims=True)
  carry_s[...] = carry_s[...] + jnp.sum(onehot, axis=1, keepdims=True)
  cnt_ref[...] = carry_s[...]

  row8 = lax.broadcasted_iota(jnp.int32, info_ref.shape, 0)
  info_ref[...] = jnp.where(row8 == 0, cls.astype(F32), jnp.where(row8 == 1, rank, 0.0))


def _router(x, g, shift, scale, wr_hi, wr_lo, br, tri):
  return pl.pallas_call(
      _router_kernel,
      grid=(T // TM,),
      in_specs=[
          pl.BlockSpec((TM, D), lambda i: (i, 0)),
          pl.BlockSpec((1, D), lambda i: (0, 0)),
          pl.BlockSpec((None, 1, D), lambda i: (_seg(i, TM), 0, 0)),
          pl.BlockSpec((None, 1, D), lambda i: (_seg(i, TM), 0, 0)),
          pl.BlockSpec((D, 128), lambda i: (0, 0)),
          pl.BlockSpec((D, 128), lambda i: (0, 0)),
          pl.BlockSpec((1, 128), lambda i: (0, 0)),
          pl.BlockSpec((TM, TM), lambda i: (0, 0)),
      ],
      out_specs=(
          pl.BlockSpec((None, 8, TM), lambda i: (i, 0, 0)),
          pl.BlockSpec((ROUTER_ROWS, TM), lambda i: (0, 0)),
      ),
      out_shape=(
          jax.ShapeDtypeStruct((T // TM, 8, TM), F32),
          jax.ShapeDtypeStruct((ROUTER_ROWS, TM), F32),
      ),
      scratch_shapes=[pltpu.VMEM((ROUTER_ROWS, TM), F32)],
      compiler_params=_params(("arbitrary",)),
      name="router",
  )(x, g, shift, scale, wr_hi, wr_lo, br, tri)


def _slot_of(slot_ref, token):
  shift = TM.bit_length() - 1
  assert TM == 1 << shift
  return slot_ref[lax.shift_right_logical(token, jnp.int32(shift)), token & (TM - 1)]


def _tile_rows(ref, row):
  return ref.at[pl.ds(pl.multiple_of(row * ROW_SUB, ROW_SUB), ROW_SUB), :]


def _scatter_copy(slot_ref, step, r, stage, buf, hs_hbm, sem):
  return pltpu.make_async_copy(
      stage.at[buf, pl.ds(r * ROW_SUB, ROW_SUB), :],
      _tile_rows(hs_hbm, _slot_of(slot_ref, step * TS + r)), sem.at[buf])


def _scatter_kernel(slot_ref, pstart_ref, pn_ref, nv_ref,
                    x_ref, g_ref, sh_ref, sc_ref, hs_hbm,
                    stage, zero_s, sem, zsem):
  i = pl.program_id(0)
  n = pl.num_programs(0)
  buf = i % 2

  def pad_copy(c, r):
    return pltpu.make_async_copy(zero_s, _tile_rows(hs_hbm, pstart_ref[c] + r), zsem.at[0])

  def tail_copy(t):
    rows = TMM * ROW_SUB
    return pltpu.make_async_copy(
        stage.at[1], hs_hbm.at[pl.ds(pl.multiple_of(t * rows, rows), rows), :], zsem.at[1])

  @pl.when(i == 0)
  def _():
    stage[...] = jnp.zeros_like(stage)
    zero_s[...] = jnp.zeros_like(zero_s)
    for c in range(N_CLASSES):
      def start(r, carry, c=c):
        pad_copy(c, r).start()
        return carry
      lax.fori_loop(0, pn_ref[c], start, 0)

    def tail_start(t, carry):
      tail_copy(t).start()
      return carry
    lax.fori_loop(nv_ref[0], NT_E, tail_start, 0)
    for c in range(N_CLASSES):
      def wait(r, carry, c=c):
        pad_copy(c, r).wait()
        return carry
      lax.fori_loop(0, pn_ref[c], wait, 0)

    def tail_wait(t, carry):
      tail_copy(t).wait()
      return carry
    lax.fori_loop(nv_ref[0], NT_E, tail_wait, 0)

  h = _norm_mod(x_ref[...], g_ref[...], sh_ref[...], sc_ref[...])
  for s in range(ROW_SUB):
    stage[buf, pl.ds(s, TS, stride=ROW_SUB), :] = h[:, s * 128:(s + 1) * 128]

  def start(r, carry):
    _scatter_copy(slot_ref, i, r, stage, buf, hs_hbm, sem).start()
    return carry
  lax.fori_loop(0, TS, start, 0, unroll=8)

  @pl.when(i > 0)
  def _():
    def wait(r, carry):
      _scatter_copy(slot_ref, i - 1, r, stage, 1 - buf, hs_hbm, sem).wait()
      return carry
    lax.fori_loop(0, TS, wait, 0, unroll=8)

  @pl.when(i == n - 1)
  def _():
    def wait(r, carry):
      _scatter_copy(slot_ref, i, r, stage, buf, hs_hbm, sem).wait()
      return carry
    lax.fori_loop(0, TS, wait, 0, unroll=8)


def _scatter_rows(slot, pad_start, pad_n, n_valid, x, g, shift, scale):
  assert TS == TMM
  grid_spec = pltpu.PrefetchScalarGridSpec(
      num_scalar_prefetch=4,
      grid=(T // TS,),
      in_specs=[
          pl.BlockSpec((TS, D), lambda i, *_: (i, 0)),
          pl.BlockSpec((1, D), lambda i, *_: (0, 0)),
          pl.BlockSpec((None, 1, D), lambda i, *_: (_seg(i, TS), 0, 0)),
          pl.BlockSpec((None, 1, D), lambda i, *_: (_seg(i, TS), 0, 0)),
      ],
      out_specs=pl.BlockSpec(memory_space=pl.ANY),
      scratch_shapes=[
          pltpu.VMEM((2, TS * ROW_SUB, 128), F32),
          pltpu.VMEM((ROW_SUB, 128), F32),
          pltpu.SemaphoreType.DMA((2,)),
          pltpu.SemaphoreType.DMA((2,)),
      ],
  )
  return pl.pallas_call(
      _scatter_kernel,
      grid_spec=grid_spec,
      out_shape=jax.ShapeDtypeStruct((P_PAD * ROW_SUB, 128), F32),
      compiler_params=_params(("arbitrary",)),
      name="scatter_rows",
  )(slot, pad_start, pad_n, n_valid, x, g, shift, scale)


def _router_logits(h, wrh_ref, wrl_ref, br_ref):
  h_hi = h.astype(BF16)
  h_lo = (h - h_hi.astype(F32)).astype(BF16)
  dot = functools.partial(jnp.dot, preferred_element_type=F32)
  return (dot(h_hi, wrh_ref[...]) + dot(h_lo, wrh_ref[...]) + dot(h_hi, wrl_ref[...])
          + br_ref[...])


def _expert_kernel(elo_ref, ehi_ref, nv_ref, hs_ref, wrh_ref, br_ref,
                   wg_lo, wg_hi, wu_lo, wu_hi, wd_lo, wd_hi, y_ref):
  i = pl.program_id(0)

  @pl.when(i < nv_ref[0])
  def _():
    h = jnp.concatenate(
        [hs_ref[pl.ds(s, TMM, stride=ROW_SUB), :] for s in range(ROW_SUB)], axis=1)
    hb = h.astype(BF16)
    logits = jnp.dot(hb, wrh_ref[...], preferred_element_type=F32) + br_ref[...]
    lane = lax.broadcasted_iota(jnp.int32, logits.shape, 1)
    elo, ehi = elo_ref[i], ehi_ref[i]

    def pick(idx):
      return jnp.sum(jnp.where(lane == idx, logits, 0.0), axis=-1, keepdims=True)
    l_g, l_lo, l_hi = pick(elo // EPG), pick(N_GROUPS + elo), pick(N_GROUPS + ehi)
    p_top = 1.0 / jnp.sum(jnp.where(lane < N_GROUPS, jnp.exp(logits - l_g), 0.0),
                          axis=-1, keepdims=True)
    wl = jnp.broadcast_to(p_top / (1.0 + jnp.exp(l_hi - l_lo)), (TMM, D_EXPERT))
    wh = jnp.broadcast_to(p_top / (1.0 + jnp.exp(l_lo - l_hi)), (TMM, D_EXPERT))
    dot = functools.partial(jnp.dot, preferred_element_type=F32)
    a_lo = (_silu(dot(hb, wg_lo[...])) * dot(hb, wu_lo[...])) * wl
    a_hi = (_silu(dot(hb, wg_hi[...])) * dot(hb, wu_hi[...])) * wh
    y = dot(a_lo.astype(BF16), wd_lo[...]) + dot(a_hi.astype(BF16), wd_hi[...])
    for s in range(ROW_SUB):
      y_ref[pl.ds(s, TMM, stride=ROW_SUB), :] = y[:, s * 128:(s + 1) * 128]

  @pl.when(i >= nv_ref[0])
  def _():
    y_ref[...] = jnp.zeros_like(y_ref)


def _experts(layer, tile_elo, tile_ehi, n_valid, hs, wr_hi, br, w_gate, w_up, w_down):
  up_spec_lo = pl.BlockSpec((None, None, D, D_EXPERT), lambda i, elo, ehi, nv: (layer, elo[i], 0, 0))
  up_spec_hi = pl.BlockSpec((None, None, D, D_EXPERT), lambda i, elo, ehi, nv: (layer, ehi[i], 0, 0))
  dn_spec_lo = pl.BlockSpec((None, None, D_EXPERT, D), lambda i, elo, ehi, nv: (layer, elo[i], 0, 0))
  dn_spec_hi = pl.BlockSpec((None, None, D_EXPERT, D), lambda i, elo, ehi, nv: (layer, ehi[i], 0, 0))
  grid_spec = pltpu.PrefetchScalarGridSpec(
      num_scalar_prefetch=3,
      grid=(NT_E,),
      in_specs=[
          pl.BlockSpec((TMM * ROW_SUB, 128), lambda i, elo, ehi, nv: (i, 0)),
          pl.BlockSpec((D, 128), lambda i, elo, ehi, nv: (0, 0)),
          pl.BlockSpec((1, 128), lambda i, elo, ehi, nv: (0, 0)),
          up_spec_lo, up_spec_hi, up_spec_lo, up_spec_hi, dn_spec_lo, dn_spec_hi,
      ],
      out_specs=pl.BlockSpec((TMM * ROW_SUB, 128), lambda i, elo, ehi, nv: (i, 0)),
  )
  return pl.pallas_call(
      _expert_kernel,
      grid_spec=grid_spec,
      out_shape=jax.ShapeDtypeStruct((P_PAD * ROW_SUB, 128), F32),
      compiler_params=_params(("arbitrary",)),
      name="experts",
  )(tile_elo, tile_ehi, n_valid, hs, wr_hi, br,
    w_gate, w_gate, w_up, w_up, w_down, w_down)


def _gather_copy(slot_ref, step, r, y_hbm, ybuf, buf, sem):
  return pltpu.make_async_copy(
      _tile_rows(y_hbm, _slot_of(slot_ref, step * TC + r)),
      ybuf.at[buf, pl.ds(r * ROW_SUB, ROW_SUB), :], sem.at[buf])


def _combine_kernel(slot_ref, x_ref, gate_ref, nf_ref, y_hbm, *rest, final):
  if final:
    op_ref, os_ref, ybuf, sem = rest
  else:
    o_ref, ybuf, sem = rest
  i = pl.program_id(0)
  n = pl.num_programs(0)
  buf = i % 2

  def start(step, b):
    def body(r, carry):
      _gather_copy(slot_ref, step, r, y_hbm, ybuf, b, sem).start(priority=ROW_GATHER_PRIORITY)
      return carry
    lax.fori_loop(0, TC, body, 0, unroll=8)

  @pl.when(i == 0)
  def _():
    start(0, 0)

  @pl.when(i + 1 < n)
  def _():
    start(i + 1, 1 - buf)

  def wait(r, carry):
    _gather_copy(slot_ref, i, r, y_hbm, ybuf, buf, sem).wait()
    return carry
  lax.fori_loop(0, TC, wait, 0, unroll=8)

  y = jnp.concatenate(
      [ybuf[buf, pl.ds(s, TC, stride=ROW_SUB), :] for s in range(ROW_SUB)], axis=1)
  x = x_ref[...] + gate_ref[...] * y
  if not final:
    o_ref[...] = x
    return
  x = (x * lax.rsqrt(jnp.mean(x * x, axis=-1, keepdims=True) + EPS)) * nf_ref[...]
  is_prompt = i < T_P // TC

  @pl.when(is_prompt)
  def _():
    op_ref[...] = x

  @pl.when(jnp.logical_not(is_prompt))
  def _():
    os_ref[...] = x


def _combine(slot, x, gate, norm_f, y_sorted, final):
  npt = T_P // TC
  if final:
    out_specs = (pl.BlockSpec((TC, D), lambda i, s: (jnp.minimum(i, npt - 1), 0)),
                 pl.BlockSpec((TC, D), lambda i, s: (jnp.maximum(i - npt, 0), 0)))
    out_shape = (jax.ShapeDtypeStruct((T_P, D), F32), jax.ShapeDtypeStruct((T_S, D), F32))
  else:
    out_specs = pl.BlockSpec((TC, D), lambda i, s: (i, 0))
    out_shape = jax.ShapeDtypeStruct((T, D), F32)
  grid_spec = pltpu.PrefetchScalarGridSpec(
      num_scalar_prefetch=1,
      grid=(T // TC,),
      in_specs=[
          pl.BlockSpec((TC, D), lambda i, s: (i, 0)),
          pl.BlockSpec((None, 1, D), lambda i, s: (_seg(i, TC), 0, 0)),
          pl.BlockSpec((1, D), lambda i, s: (0, 0)),
          pl.BlockSpec(memory_space=pl.ANY),
      ],
      out_specs=out_specs,
      scratch_shapes=[pltpu.VMEM((2, TC * ROW_SUB, 128), F32), pltpu.SemaphoreType.DMA((2,))],
  )
  return pl.pallas_call(
      functools.partial(_combine_kernel, final=final),
      grid_spec=grid_spec,
      out_shape=out_shape,
      compiler_params=_params(("arbitrary",)),
      name="combine_final" if final else "combine",
  )(slot, x, gate, norm_f, y_sorted)


def _class_experts():
  lo, hi = [], []
  for g in range(N_GROUPS):
    for a in range(EPG):
      for b in range(a + 1, EPG):
        lo.append(g * EPG + a)
        hi.append(g * EPG + b)
  return np.asarray(lo, np.int32), np.asarray(hi, np.int32)


def _moe_layer(layer, x, g, shift, scale, gate, wr_hi, wr_lo, br, tri, w_gate, w_up, w_down, norm_f):
  final = layer == DEPTH - 1
  info, cnt = _router(x, g, shift, scale, wr_hi, wr_lo, br, tri)

  cls = info[:, 0, :].astype(jnp.int32)
  rank = info[:, 1, :].astype(jnp.int32)
  counts = cnt[:N_CLASSES, 0].astype(jnp.int32)
  tiles = (counts + TMM - 1) // TMM
  tile_end = jnp.cumsum(tiles)
  offs = (tile_end - tiles) * TMM
  slot = rank
  for k in range(N_CLASSES):
    slot = slot + jnp.where(cls == k, offs[k], 0)
  n_valid = tile_end[-1]
  tile_ids = jnp.minimum(jnp.arange(NT_E, dtype=jnp.int32), n_valid - 1)
  tile_cls = jnp.sum((tile_ids[:, None] >= tile_end[None, :]).astype(jnp.int32), axis=1)
  tile_cls = jnp.minimum(tile_cls, N_CLASSES - 1)
  cls_lo, cls_hi = _class_experts()
  tile_elo = jnp.asarray(cls_lo)[tile_cls]
  tile_ehi = jnp.asarray(cls_hi)[tile_cls]

  n_valid = n_valid.reshape(1)
  hs = _scatter_rows(slot, offs + counts, tiles * TMM - counts, n_valid, x, g, shift, scale)
  y_sorted = _experts(layer, tile_elo, tile_ehi, n_valid, hs, wr_hi, br, w_gate, w_up, w_down)
  return _combine(slot, x, gate, norm_f, y_sorted, final)


def kernel(x_prompt, x_sample, cache_k, cache_v, state_ret, c, c_ctx, norm1, norm2, w_ada, b_ada, attn_w_qkv, attn_q_norm, attn_k_norm, attn_w_o, pool_w, pool_scale, ret_w_in, ret_decay_logit, ret_norm, ret_w_out, moe_w_router_g, moe_b_router_g, moe_w_router_e, moe_b_router_e, moe_w_gate, moe_w_up, moe_w_down, norm_f):
  x = (x_prompt.reshape(T_P, D), x_sample.reshape(T_S, D))
  cond8 = jnp.concatenate([c_ctx[None, :], c, jnp.zeros((N_SEG - 1 - NB_S, D), F32)], axis=0)
  mods = _adaln(cond8, w_ada, b_ada)
  mods = mods.reshape(DEPTH, N_SEG, 6, 1, D).transpose(0, 2, 1, 3, 4)
  rope = _rope_tables()
  tri = jnp.triu(jnp.ones((TM, TM), BF16), 1)
  pad_r = 128 - N_GROUPS - N_EXPERTS
  norm_f2 = norm_f.reshape(1, D)

  w_gate, w_up, w_down = moe_w_gate.astype(BF16), moe_w_up.astype(BF16), moe_w_down.astype(BF16)

  new_k, new_v, new_s = [], [], []
  for i in range(DEPTH):
    kind, j = i % 3, i // 3
    m = mods[i]
    g1 = norm1[i].reshape(1, D)
    g2 = norm2[i].reshape(1, D)
    if kind == 0:
      qkv = _nm_matmul(x, g1, m[0], m[1], attn_w_qkv[j].astype(BF16), TM, "qkv_proj")
      qn = attn_q_norm[j].reshape(1, HEAD_DIM)
      kn = attn_k_norm[j].reshape(1, HEAD_DIM)
      o_p, kc, vc = _attn_prompt(qkv, qn, kn)
      ck = cache_k[:, j].reshape(NB_S, PAST, N_KV * HEAD_DIM)
      cv = cache_v[:, j].reshape(NB_S, PAST, N_KV * HEAD_DIM)
      o_s = _attn_latent(qkv, ck, cv, qn, kn, rope)
      new_k.append(kc.reshape(NB_P, SEQ_P, N_KV, HEAD_DIM))
      new_v.append(vc.reshape(NB_P, SEQ_P, N_KV, HEAD_DIM))
      x = _mm_res((o_p, o_s), attn_w_o[j].astype(BF16), x, m[2], "attn_out")
    elif kind == 1:
      x = _pool_layer(x, g1, m[0], m[1], m[2], pool_w[j].astype(BF16), pool_scale[j].reshape(1, D))
    else:
      p = _nm_matmul(x, g1, m[0], m[1], ret_w_in[j].astype(BF16), 256, "ret_proj")
      dl = jnp.broadcast_to(ret_decay_logit[j].T[:, :, None, None], (RET_HEADS, 2, 1, RET_DK))
      ng = ret_norm[j].reshape(1, RET_HEADS * RET_DV)
      y_p, s_new = _retention(p, dl, ng, None, True)
      y_s = _retention(p, dl, ng, state_ret[:, j], False)
      new_s.append(s_new)
      x = _mm_res((y_p, y_s), ret_w_out[j].astype(BF16), x, m[2], "ret_out")
    wr = jnp.concatenate([moe_w_router_g[i], moe_w_router_e[i], jnp.zeros((D, pad_r), F32)], axis=1)
    br = jnp.concatenate([moe_b_router_g[i], moe_b_router_e[i], jnp.zeros((pad_r,), F32)]).reshape(1, 128)
    wr_hi = wr.astype(BF16)
    wr_lo = (wr - wr_hi.astype(F32)).astype(BF16)
    x = _moe_layer(i, x, g2, m[3], m[4], m[5], wr_hi, wr_lo, br, tri, w_gate, w_up, w_down, norm_f2)

  y_prompt = x[0].reshape(NB_P, SEQ_P, D)
  y_sample = x[1].reshape(NB_S, SEQ_S, D)
  new_cache_k = jnp.stack(new_k, axis=1)
  new_cache_v = jnp.stack(new_v, axis=1)
  assert len(new_s) == 1
  new_state_ret = new_s[0].reshape(NB_P, 1, 2, RET_HEADS, RET_DK, RET_DV)
  return (y_prompt, y_sample, new_cache_k, new_cache_v, new_state_ret)
```

```python
import functools

import jax
import jax.numpy as jnp
import numpy as np
from jax import lax
from jax.experimental import pallas as pl
from jax.experimental.pallas import tpu as pltpu

F32 = jnp.float32
BF16 = jnp.bfloat16

D = 1024
NB_P, SEQ_P = 32, 256
NB_S, SEQ_S = 4, 2048
T_P = NB_P * SEQ_P
T_S = NB_S * SEQ_S
T = T_P + T_S
DEPTH = 4
GRID_W = 64
HEAD_DIM = 128
N_HEADS = 8
N_KV = 2
KV_GROUP = N_HEADS // N_KV
PAST = 256
ROPE_THETA = 10000.0
POOL_WINDOWS = (2, 4, 8, 16)
POOL_GROUP = D // 4
POOL_HALO = 8
RET_HEADS = 8
RET_DK = 128
RET_DV = 256
RET_CHUNK = 128
RET_HEADS_PER_STEP = 2
N_GROUPS = 4
EPG = 4
N_EXPERTS = 16
D_EXPERT = 256
N_PAIRS = 6
N_CLASSES = N_GROUPS * N_PAIRS
ROUTER_ROWS = 32
EPS = 1e-6
LOG2E = 1.4426950408889634
NEG = -1e30
N_SEG = 8

VMEM_LIMIT_BYTES = 52 * 1024 * 1024

TM = 512
TQ = 256
ATTN_UNIT_Q = 128
ATTN_KEY_CHUNK = 256
ATTN_HEAD_STACK = 2
TP = 256
TMM = 256
TS = 256
TC = 256
ROW_SUB = 8
PACK_SUB = D // 2 // 128
INFO_SUB = PACK_SUB
P_PAD = T + N_CLASSES * TMM
NT_E = P_PAD // TMM


def _params(sem):
  return pltpu.CompilerParams(dimension_semantics=sem, vmem_limit_bytes=VMEM_LIMIT_BYTES)


def _seg(i, tm):
  npt = T_P // tm
  return jnp.where(i < npt, 0, (i - npt) // (SEQ_S // tm) + 1)


def _norm_mod(x, g, shift, scale):
  r = lax.rsqrt(jnp.mean(x * x, axis=-1, keepdims=True) + EPS)
  return ((x * r) * g) * (1.0 + scale) + shift


def _silu(x):
  return x * (1.0 / (1.0 + jnp.exp(-x)))


def _adaln_kernel(c_ref, w_ref, b_ref, o_ref):
  s = _silu(c_ref[...]).astype(BF16)
  o_ref[...] = jnp.dot(s, w_ref[...].astype(BF16), preferred_element_type=F32) + b_ref[...]


def _adaln(cond8, w_ada, b_ada):
  tn = 1536
  return pl.pallas_call(
      _adaln_kernel,
      grid=(DEPTH, 6 * D // tn),
      in_specs=[
          pl.BlockSpec((N_SEG, D), lambda l, j: (0, 0)),
          pl.BlockSpec((None, D, tn), lambda l, j: (l, 0, j)),
          pl.BlockSpec((None, 1, tn), lambda l, j: (l, 0, j)),
      ],
      out_specs=pl.BlockSpec((None, N_SEG, tn), lambda l, j: (l, 0, j)),
      out_shape=jax.ShapeDtypeStruct((DEPTH, N_SEG, 6 * D), F32),
      compiler_params=_params(("arbitrary", "arbitrary")),
      name="adaln",
  )(cond8, w_ada, b_ada.reshape(DEPTH, 1, 6 * D))


def _token_rows(x, tm, width):
  npt = T_P // tm
  if isinstance(x, tuple):
    arrays = x
    latent_map = lambda i: (jnp.maximum(i - npt, 0), 0)
  else:
    arrays = (x, x)
    latent_map = lambda i: (jnp.maximum(i, npt), 0)
  specs = [pl.BlockSpec((tm, width), lambda i: (jnp.minimum(i, npt - 1), 0)),
           pl.BlockSpec((tm, width), latent_map)]
  return arrays, specs


def _pick_rows(p_ref, s_ref, tm):
  return jnp.where(pl.program_id(0) < T_P // tm, p_ref[...], s_ref[...])


def _nm_matmul_kernel(xp_ref, xs_ref, g_ref, sh_ref, sc_ref, w_ref, o_ref, *, tm, n_chunk):
  x = _pick_rows(xp_ref, xs_ref, tm)
  h = _norm_mod(x, g_ref[...], sh_ref[...], sc_ref[...]).astype(BF16)
  n = w_ref.shape[1]
  for c in range(0, n, n_chunk):
    o_ref[:, c:c + n_chunk] = jnp.dot(
        h, w_ref[:, c:c + n_chunk], preferred_element_type=F32).astype(o_ref.dtype)


def _nm_matmul(x, g, shift, scale, w, tm, name):
  n = w.shape[1]
  x_arrays, x_specs = _token_rows(x, tm, D)
  return pl.pallas_call(
      functools.partial(_nm_matmul_kernel, tm=tm, n_chunk=512),
      grid=(T // tm,),
      in_specs=x_specs + [
          pl.BlockSpec((1, D), lambda i: (0, 0)),
          pl.BlockSpec((None, 1, D), lambda i: (_seg(i, tm), 0, 0)),
          pl.BlockSpec((None, 1, D), lambda i: (_seg(i, tm), 0, 0)),
          pl.BlockSpec((D, n), lambda i: (0, 0)),
      ],
      out_specs=pl.BlockSpec((tm, n), lambda i: (i, 0)),
      out_shape=jax.ShapeDtypeStruct((T, n), BF16),
      compiler_params=_params(("arbitrary",)),
      name=name,
  )(*x_arrays, g, shift, scale, w)


def _mm_res_route_kernel(ap_ref, as_ref, xp_ref, xs_ref, w_ref, gate_ref, *rest):
  n_route_in = len(_ROUTE_IN_SPECS)
  route_in, (o_ref, info_ref, cnt_ref, carry_s) = rest[:n_route_in], rest[n_route_in:]
  a = _pick_rows(ap_ref, as_ref, TM)
  x = _pick_rows(xp_ref, xs_ref, TM)
  x_new = x + gate_ref[...] * jnp.dot(a, w_ref[...], preferred_element_type=F32)
  o_ref[...] = x_new
  _route_tile(x_new, *route_in, info_ref, cnt_ref, carry_s)


def _mm_res_route(a, w, x, gate, route_args, name):
  k = w.shape[0]
  a_arrays, a_specs = _token_rows(a, TM, k)
  x_arrays, x_specs = _token_rows(x, TM, D)
  return pl.pallas_call(
      _mm_res_route_kernel,
      grid=(T // TM,),
      in_specs=a_specs + x_specs + [
          pl.BlockSpec((k, D), lambda i: (0, 0)),
          pl.BlockSpec((None, 1, D), lambda i: (_seg(i, TM), 0, 0)),
      ] + _ROUTE_IN_SPECS,
      out_specs=(pl.BlockSpec((TM, D), lambda i: (i, 0)),) + _ROUTE_OUT_SPECS,
      out_shape=(jax.ShapeDtypeStruct((T, D), F32),) + _ROUTE_OUT_SHAPE,
      scratch_shapes=_ROUTE_SCRATCH,
      compiler_params=_params(("arbitrary",)),
      name=name,
  )(*a_arrays, *x_arrays, w, gate, *route_args)


def _rope(x, c, a, b):
  return x * c + pltpu.roll(x, 96, 1) * a + pltpu.roll(x, 32, 1) * b


def _head_norm(x, w):
  return (x * lax.rsqrt(jnp.mean(x * x, axis=-1, keepdims=True) + EPS)) * w


def _attn_latent_kernel(q_ref, kn_ref, vn_ref, ck_ref, cv_ref, qw_ref, kw_ref,
                        cq_ref, aq_ref, bq_ref, ckk_ref, akk_ref, bkk_ref,
                        o_ref, k_s, vt_s, *st_refs, tq):
  @pl.when(pl.program_id(2) == 0)
  def _():
    k = _rope(_head_norm(kn_ref[...].astype(F32), kw_ref[...]), ckk_ref[...], akk_ref[...], bkk_ref[...])
    k_s[0:PAST, :] = ck_ref[...].astype(BF16)
    k_s[PAST:, :] = k.astype(BF16)
    vt_s[:, 0:PAST] = cv_ref[...].T.astype(BF16)
    vt_s[:, PAST:] = vn_ref[...].astype(F32).T.astype(BF16)

  qa = q_ref[...].astype(F32)
  lk = k_s.shape[0]

  def prep_q(h):
    qh = _head_norm(qa[:, h * HEAD_DIM:(h + 1) * HEAD_DIM], qw_ref[...])
    qh = _rope(qh, cq_ref[...], aq_ref[...], bq_ref[...])
    return (qh * (HEAD_DIM ** -0.5 * LOG2E)).astype(BF16)

  nq = ATTN_HEAD_STACK * ATTN_UNIT_Q
  chunks = [(c0, min(c0 + ATTN_KEY_CHUNK, lk)) for c0 in range(0, lk, ATTN_KEY_CHUNK)]
  qh_all = [prep_q(h) for h in range(KV_GROUP)]
  units = [(h0, r0) for r0 in range(0, tq, ATTN_UNIT_Q) for h0 in range(0, KV_GROUP, ATTN_HEAD_STACK)]

  def score_chunk(u, c0, c1, m):
    h0, r0 = units[u]
    qp = jnp.concatenate([qh_all[h0 + d][r0:r0 + ATTN_UNIT_Q] for d in range(ATTN_HEAD_STACK)], axis=0)
    st = lax.dot_general(k_s[c0:c1, :], qp, (((1,), (1,)), ((), ())),
                         preferred_element_type=F32)
    st_refs[u % 2][c0:c1, :] = st
    return jnp.maximum(m, jnp.max(st, axis=0, keepdims=True))

  def value_chunk(u, c0, c1, m, l, acc):
    pt = jnp.exp2(st_refs[u % 2][c0:c1, :] - m)
    return (l + jnp.sum(pt, axis=0, keepdims=True),
            acc + jnp.dot(vt_s[:, c0:c1], pt.astype(BF16), preferred_element_type=F32))

  def finish(u, l, acc):
    h0, r0 = units[u]
    o = acc / l
    for d in range(ATTN_HEAD_STACK):
      h = h0 + d
      o_ref[r0:r0 + ATTN_UNIT_Q, h * HEAD_DIM:(h + 1) * HEAD_DIM] = (
          o[:, d * ATTN_UNIT_Q:(d + 1) * ATTN_UNIT_Q].T.astype(BF16))

  m_prev = None
  for u in range(len(units) + 1):
    m = jnp.full((1, nq), NEG, F32)
    l = jnp.zeros((1, nq), F32)
    acc = jnp.zeros((HEAD_DIM, nq), F32)
    for c0, c1 in chunks:
      if u < len(units):
        m = score_chunk(u, c0, c1, m)
      if u > 0:
        l, acc = value_chunk(u - 1, c0, c1, m_prev, l, acc)
    if u > 0:
      finish(u - 1, l, acc)
    m_prev = m


def _attn_scratch(lk):
  return ([pltpu.VMEM((lk, HEAD_DIM), BF16), pltpu.VMEM((HEAD_DIM, lk), BF16)]
          + [pltpu.VMEM((lk, ATTN_HEAD_STACK * ATTN_UNIT_Q), F32) for _ in range(2)])


def _rope_tables():
  rows = SEQ_S // GRID_W
  t_row = jnp.broadcast_to(jnp.arange(rows)[:, None], (rows, GRID_W)).reshape(-1)
  t_col = jnp.broadcast_to(jnp.arange(GRID_W)[None, :], (rows, GRID_W)).reshape(-1)
  nf = HEAD_DIM // 4
  inv = ROPE_THETA ** (-jnp.arange(nf, dtype=F32) / nf)
  ang_r = t_row.astype(F32)[:, None] * inv[None, :]
  ang_c = t_col.astype(F32)[:, None] * inv[None, :]
  cr, sr, cc, sc = jnp.cos(ang_r), jnp.sin(ang_r), jnp.cos(ang_c), jnp.sin(ang_c)
  z = jnp.zeros_like(sr)
  c = jnp.concatenate([cr, cr, cc, cc], axis=1)
  a = jnp.concatenate([-sr, z, -sc, z], axis=1)
  b = jnp.concatenate([z, sr, z, sc], axis=1)
  return c, a, b


def _attn_prompt_kernel(qkv_ref, qw_ref, kw_ref, o_ref, ko_ref, vo_ref):
  nq = N_HEADS * HEAD_DIM
  nk = N_KV * HEAD_DIM
  for kh in range(N_KV):
    kcols = slice(nq + kh * HEAD_DIM, nq + (kh + 1) * HEAD_DIM)
    vcols = slice(nq + nk + kh * HEAD_DIM, nq + nk + (kh + 1) * HEAD_DIM)
    k = _head_norm(qkv_ref[:, kcols].astype(F32), kw_ref[...])
    v = qkv_ref[:, vcols].astype(F32)
    ko_ref[pl.ds(kh, SEQ_P, stride=N_KV), :] = k
    vo_ref[pl.ds(kh, SEQ_P, stride=N_KV), :] = v
    kb = k.astype(BF16)
    vt = v.T.astype(BF16)
    for h in range(kh * KV_GROUP, (kh + 1) * KV_GROUP):
      hcols = slice(h * HEAD_DIM, (h + 1) * HEAD_DIM)
      qh = _head_norm(qkv_ref[:, hcols].astype(F32), qw_ref[...])
      qh = (qh * (HEAD_DIM ** -0.5 * LOG2E)).astype(BF16)
      st = lax.dot_general(kb, qh, (((1,), (1,)), ((), ())), preferred_element_type=F32)
      pt = jnp.exp2(st - jnp.max(st, axis=0, keepdims=True))
      l = jnp.sum(pt, axis=0, keepdims=True)
      acc = jnp.dot(vt, pt.astype(BF16), preferred_element_type=F32)
      o_ref[:, hcols] = (acc / l).T.astype(BF16)


def _attn_prompt(qkv, q_norm, k_norm):
  width = (N_HEADS + 2 * N_KV) * HEAD_DIM
  out_shapes = (
      jax.ShapeDtypeStruct((T_P, N_HEADS * HEAD_DIM), BF16),
      jax.ShapeDtypeStruct((NB_P, SEQ_P * N_KV, HEAD_DIM), F32),
      jax.ShapeDtypeStruct((NB_P, SEQ_P * N_KV, HEAD_DIM), F32),
  )
  return pl.pallas_call(
      _attn_prompt_kernel,
      grid=(NB_P,),
      in_specs=[
          pl.BlockSpec((SEQ_P, width), lambda b: (b, 0)),
          pl.BlockSpec((1, HEAD_DIM), lambda b: (0, 0)),
          pl.BlockSpec((1, HEAD_DIM), lambda b: (0, 0)),
      ],
      out_specs=(
          pl.BlockSpec((SEQ_P, N_HEADS * HEAD_DIM), lambda b: (b, 0)),
          pl.BlockSpec((None, SEQ_P * N_KV, HEAD_DIM), lambda b: (b, 0, 0)),
          pl.BlockSpec((None, SEQ_P * N_KV, HEAD_DIM), lambda b: (b, 0, 0)),
      ),
      out_shape=out_shapes,
      compiler_params=_params(("arbitrary",)),
      name="attn_prompt",
  )(qkv, q_norm, k_norm)


def _attn_latent(qkv, cache_k, cache_v, q_norm, k_norm, rope):
  kcol = N_HEADS
  vcol = kcol + N_KV
  nq = SEQ_S // TQ
  row0 = T_P // TQ
  seq0 = T_P // SEQ_S
  lk = PAST + SEQ_S
  c, a, b = rope
  tab_q = pl.BlockSpec((TQ, HEAD_DIM), lambda bb, h, i: (i, 0))
  tab_k = pl.BlockSpec((SEQ_S, HEAD_DIM), lambda bb, h, i: (0, 0))
  return pl.pallas_call(
      functools.partial(_attn_latent_kernel, tq=TQ),
      grid=(NB_S, N_KV, nq),
      in_specs=[
          pl.BlockSpec((TQ, KV_GROUP * HEAD_DIM), lambda bb, h, i: (row0 + bb * nq + i, h)),
          pl.BlockSpec((SEQ_S, HEAD_DIM), lambda bb, h, i: (seq0 + bb, kcol + h)),
          pl.BlockSpec((SEQ_S, HEAD_DIM), lambda bb, h, i: (seq0 + bb, vcol + h)),
          pl.BlockSpec((None, PAST, HEAD_DIM), lambda bb, h, i: (bb, 0, h)),
          pl.BlockSpec((None, PAST, HEAD_DIM), lambda bb, h, i: (bb, 0, h)),
          pl.BlockSpec((1, HEAD_DIM), lambda bb, h, i: (0, 0)),
          pl.BlockSpec((1, HEAD_DIM), lambda bb, h, i: (0, 0)),
          tab_q, tab_q, tab_q, tab_k, tab_k, tab_k,
      ],
      out_specs=pl.BlockSpec((TQ, KV_GROUP * HEAD_DIM), lambda bb, h, i: (bb * nq + i, h)),
      out_shape=jax.ShapeDtypeStruct((T_S, N_HEADS * HEAD_DIM), BF16),
      scratch_shapes=_attn_scratch(lk),
      compiler_params=_params(("arbitrary", "arbitrary", "arbitrary")),
      name="attn_latent",
  )(qkv, qkv, qkv, cache_k, cache_v, q_norm, k_norm, c, a, b, c, a, b)


def _pool_kernel(x_ref, xp_ref, xn_ref, g_ref, sh_ref, sc_ref, gate_ref, w_ref, ps_ref, o_ref):
  t = pl.program_id(0)
  npt = T_P // TP
  tiles_s = SEQ_S // TP
  is_p = t < npt
  pos = jnp.where(is_p, 0, (t - npt) % tiles_s)
  ntile = jnp.where(is_p, SEQ_P // TP, tiles_s)
  seq_len = ntile * TP
  keep_prev = jnp.where(pos == 0, 0.0, 1.0)
  keep_next = jnp.where(pos == ntile - 1, 0.0, 1.0)

  g, sh, sc = g_ref[...], sh_ref[...], sc_ref[...]
  x = x_ref[...]
  h = _norm_mod(x, g, sh, sc)
  hp = _norm_mod(xp_ref[...], g, sh, sc) * keep_prev
  hn = _norm_mod(xn_ref[...], g, sh, sc) * keep_next
  ext = jnp.concatenate([hp, h, hn], axis=0)
  n_ext = TP + 2 * POOL_HALO
  tseq = pos * TP + lax.broadcasted_iota(jnp.int32, (TP, POOL_GROUP), 0)

  outs = []
  for gi, win in enumerate(POOL_WINDOWS):
    lo, hi = gi * POOL_GROUP, (gi + 1) * POOL_GROUP
    acc = ext[:, lo:hi]
    span = 1
    while span < win:
      acc = acc + pltpu.roll(acc, n_ext - span, 0)
      span *= 2
    start = POOL_HALO - win // 2
    if start:
      acc = pltpu.roll(acc, n_ext - start, 0)
    ssum = acc[0:TP]
    cnt = (jnp.minimum(tseq + win // 2, seq_len) - jnp.maximum(tseq - win // 2, 0)).astype(F32)
    dlt = (ssum / cnt - h[:, lo:hi]).astype(BF16)
    outs.append(jnp.dot(dlt, w_ref[gi], preferred_element_type=F32))
  y = jnp.concatenate(outs, axis=1) * ps_ref[...]
  o_ref[...] = x + gate_ref[...] * y


def _pool_layer(x, g, shift, scale, gate, w, pscale):
  hb = TP // POOL_HALO
  last = T // POOL_HALO - 1
  seg = lambda i: _seg(i, TP)
  return pl.pallas_call(
      _pool_kernel,
      grid=(T // TP,),
      in_specs=[
          pl.BlockSpec((TP, D), lambda i: (i, 0)),
          pl.BlockSpec((POOL_HALO, D), lambda i: (jnp.maximum(i * hb - 1, 0), 0)),
          pl.BlockSpec((POOL_HALO, D), lambda i: (jnp.minimum((i + 1) * hb, last), 0)),
          pl.BlockSpec((1, D), lambda i: (0, 0)),
          pl.BlockSpec((None, 1, D), lambda i: (seg(i), 0, 0)),
          pl.BlockSpec((None, 1, D), lambda i: (seg(i), 0, 0)),
          pl.BlockSpec((None, 1, D), lambda i: (seg(i), 0, 0)),
          pl.BlockSpec((4, POOL_GROUP, POOL_GROUP), lambda i: (0, 0, 0)),
          pl.BlockSpec((1, D), lambda i: (0, 0)),
      ],
      out_specs=pl.BlockSpec((TP, D), lambda i: (i, 0)),
      out_shape=jax.ShapeDtypeStruct((T, D), F32),
      compiler_params=_params(("arbitrary",)),
      name="pool",
  )(x, x, x, g, shift, scale, gate, w, pscale)


def _ret_kernel(*refs, seq_len, has_s0, nh):
  it = iter(refs)
  q_ref, k_ref, v_ref, gate_ref, dl_ref, ng_ref = [next(it) for _ in range(6)]
  s0_ref = next(it) if has_s0 else None
  y_ref = next(it)
  so_ref = None if has_s0 else next(it)
  u_s = next(it)
  dec_s, cdec_s = next(it), next(it)
  c = RET_CHUNK
  n = seq_len // c
  kscale = RET_DK ** -0.5
  nt = (((1,), (1,)), ((), ()))
  tn = (((0,), (0,)), ((), ()))

  @pl.when(pl.program_id(1) == 0)
  def _():
    ri = lax.broadcasted_iota(jnp.int32, (c, c), 0).astype(F32)
    ci = lax.broadcasted_iota(jnp.int32, (c, c), 1).astype(F32)
    ri2 = lax.broadcasted_iota(jnp.int32, (c, RET_DV), 0).astype(F32)
    for hh in range(nh):
      lg = -jnp.log1p(jnp.exp(-dl_ref[hh]))
      lgf, lgb = lg[0], lg[1]
      lgf2 = jnp.concatenate([lgf, lgf], axis=1)
      lgb2 = jnp.concatenate([lgb, lgb], axis=1)
      dec_s[hh, 0] = jnp.concatenate(
          [jnp.exp((c - 1.0 - ri) * lgf), jnp.exp(ri * lgb)], axis=1) * kscale
      intra = (jnp.where(ri >= ci, jnp.exp((ri - ci) * lgf), 0.0)
               + jnp.where(ri <= ci, jnp.exp((ci - ri) * lgb), 0.0)) * kscale
      dec_s[hh, 1] = jnp.concatenate([intra, intra], axis=1)
      dec_s[hh, 2] = jnp.exp((ri2 + 1.0) * lgf2)
      dec_s[hh, 3] = jnp.exp((c - ri2) * lgb2)
      cdec_s[hh, 0] = jnp.exp(c * lgf2)
      cdec_s[hh, 1] = jnp.exp(c * lgb2)

  for hh in range(nh):
    qcols = slice(hh * RET_DK, (hh + 1) * RET_DK)
    vcols = slice(hh * RET_DV, (hh + 1) * RET_DV)
    kd2 = dec_s[hh, 0]

    for j in range(n):
      rows = slice(j * c, (j + 1) * c)
      kc = k_ref[rows, qcols].astype(F32)
      k2 = (jnp.concatenate([kc, kc], axis=1) * kd2).astype(BF16)
      u = lax.dot_general(k2, v_ref[rows, vcols], tn, preferred_element_type=F32)
      u_s[0, hh, j] = u[0:RET_DK]
      u_s[1, hh, j] = u[RET_DK:]

    for d, order in ((0, range(n)), (1, reversed(range(n)))):
      cdec = cdec_s[hh, d]
      st = s0_ref[d, hh] if has_s0 else jnp.zeros((RET_DK, RET_DV), F32)
      for j in order:
        u = u_s[d, hh, j]
        u_s[d, hh, j] = st
        st = st * cdec + u
      if not has_s0:
        so_ref[d, hh] = st

    intra = dec_s[hh, 1][:, :c]
    qd_f = dec_s[hh, 2]
    qd_b = dec_s[hh, 3]
    for j in range(n):
      rows = slice(j * c, (j + 1) * c)
      qc = q_ref[rows, qcols]
      vc = v_ref[rows, vcols]
      s = lax.dot_general(qc, k_ref[rows, qcols], nt, preferred_element_type=F32) * intra
      st2 = jnp.concatenate([u_s[0, hh, j], u_s[1, hh, j]], axis=1).astype(BF16)
      inter = jnp.dot(qc, st2, preferred_element_type=F32)
      o = (jnp.dot(s.astype(BF16), vc, preferred_element_type=F32)
           + inter[:, :RET_DV] * qd_f + inter[:, RET_DV:] * qd_b)
      mu = jnp.mean(o, axis=-1, keepdims=True)
      dv = o - mu
      var = jnp.mean(dv * dv, axis=-1, keepdims=True)
      on = (dv * lax.rsqrt(var + EPS)) * ng_ref[:, vcols]
      y_ref[rows, vcols] = (on * _silu(gate_ref[rows, vcols].astype(F32))).astype(BF16)


def _retention(p, dl, ng, state0, prompt):
  seq_len = SEQ_P if prompt else SEQ_S
  nb = NB_P if prompt else NB_S
  nh = RET_HEADS_PER_STEP
  row0 = 0 if prompt else T_P // SEQ_S
  hb = RET_HEADS // nh
  kcol = hb
  vcol = (2 * RET_HEADS * RET_DK) // (nh * RET_DV)
  gcol = vcol + hb
  in_specs = [
      pl.BlockSpec((seq_len, nh * RET_DK), lambda h, b: (row0 + b, h)),
      pl.BlockSpec((seq_len, nh * RET_DK), lambda h, b: (row0 + b, kcol + h)),
      pl.BlockSpec((seq_len, nh * RET_DV), lambda h, b: (row0 + b, vcol + h)),
      pl.BlockSpec((seq_len, nh * RET_DV), lambda h, b: (row0 + b, gcol + h)),
      pl.BlockSpec((nh, 2, 1, RET_DK), lambda h, b: (h, 0, 0, 0)),
      pl.BlockSpec((1, nh * RET_DV), lambda h, b: (0, h)),
  ]
  args = [p, p, p, p, dl, ng]
  y_spec = pl.BlockSpec((seq_len, nh * RET_DV), lambda h, b: (b, h))
  y_shape = jax.ShapeDtypeStruct((nb * seq_len, RET_HEADS * RET_DV), BF16)
  state_spec = pl.BlockSpec((None, 2, nh, RET_DK, RET_DV), lambda h, b: (b, 0, h, 0, 0))
  if prompt:
    out_specs = (y_spec, state_spec)
    out_shape = (y_shape, jax.ShapeDtypeStruct((NB_P, 2, RET_HEADS, RET_DK, RET_DV), F32))
  else:
    in_specs.append(state_spec)
    args.append(state0)
    out_specs = y_spec
    out_shape = y_shape
  return pl.pallas_call(
      functools.partial(_ret_kernel, seq_len=seq_len, has_s0=not prompt, nh=nh),
      grid=(hb, nb),
      in_specs=in_specs,
      out_specs=out_specs,
      out_shape=out_shape,
      scratch_shapes=[
          pltpu.VMEM((2, nh, seq_len // RET_CHUNK, RET_DK, RET_DV), F32),
          pltpu.VMEM((nh, 4, RET_CHUNK, RET_DV), F32),
          pltpu.VMEM((nh, 2, 1, RET_DV), F32),
      ],
      compiler_params=_params(("arbitrary", "arbitrary")),
      name="ret_prompt" if prompt else "ret_latent",
  )(*args)


def _router_kernel(x_ref, *route_refs):
  _route_tile(x_ref[...], *route_refs)


def _route_tile(x, g_ref, sh_ref, sc_ref, wrh_ref, wrl_ref, br_ref, triu_ref,
                info_ref, cnt_ref, carry_s):
  @pl.when(pl.program_id(0) == 0)
  def _():
    carry_s[...] = jnp.zeros_like(carry_s)

  h = _norm_mod(x, g_ref[...], sh_ref[...], sc_ref[...])
  logits = _router_logits(h, wrh_ref, wrl_ref, br_ref)
  lt = logits.T[0:ROUTER_ROWS]
  row = lax.broadcasted_iota(jnp.int32, lt.shape, 0)
  big = jnp.int32(ROUTER_ROWS)

  def first_max(v):
    m = jnp.max(v, axis=0, keepdims=True)
    return jnp.min(jnp.where(v == m, row, big), axis=0, keepdims=True)

  gidx = first_max(jnp.where(row < N_GROUPS, lt, NEG))
  lo = N_GROUPS + EPG * gidx
  le = jnp.where((row >= lo) & (row < lo + EPG), lt, NEG)
  i1 = first_max(le)
  i2 = first_max(jnp.where(row == i1, NEG, le))
  e_lo = jnp.minimum(i1, i2) - N_GROUPS
  e_hi = jnp.maximum(i1, i2) - N_GROUPS
  a = e_lo - EPG * gidx
  b = e_hi - EPG * gidx
  pair_base = jnp.where(a == 0, 0, jnp.where(a == 1, 3, 5))
  cls = N_PAIRS * gidx + pair_base + (b - a - 1)

  onehot = jnp.where(row == cls, 1.0, 0.0)
  before = jnp.dot(onehot.astype(BF16), triu_ref[...], preferred_element_type=F32) + carry_s[...]
  rank = jnp.sum(jnp.where(row == cls, before, 0.0), axis=0, keepdims=True)
  carry_s[...] = carry_s[...] + jnp.sum(onehot, axis=1, keepdims=True)
  cnt_ref[...] = carry_s[...]

  row8 = lax.broadcasted_iota(jnp.int32, info_ref.shape, 0)
  info_ref[...] = jnp.where(row8 == 0, cls.astype(F32), jnp.where(row8 == 1, rank, 0.0))


_ROUTE_IN_SPECS = [
    pl.BlockSpec((1, D), lambda i: (0, 0)),
    pl.BlockSpec((None, 1, D), lambda i: (_seg(i, TM), 0, 0)),
    pl.BlockSpec((None, 1, D), lambda i: (_seg(i, TM), 0, 0)),
    pl.BlockSpec((D, 128), lambda i: (0, 0)),
    pl.BlockSpec((D, 128), lambda i: (0, 0)),
    pl.BlockSpec((1, 128), lambda i: (0, 0)),
    pl.BlockSpec((TM, TM), lambda i: (0, 0)),
]
_ROUTE_OUT_SPECS = (
    pl.BlockSpec((None, 8, TM), lambda i: (i, 0, 0)),
    pl.BlockSpec((ROUTER_ROWS, TM), lambda i: (0, 0)),
)
_ROUTE_OUT_SHAPE = (
    jax.ShapeDtypeStruct((T // TM, 8, TM), F32),
    jax.ShapeDtypeStruct((ROUTER_ROWS, TM), F32),
)
_ROUTE_SCRATCH = [pltpu.VMEM((ROUTER_ROWS, TM), F32)]


def _router(x, route_args):
  return pl.pallas_call(
      _router_kernel,
      grid=(T // TM,),
      in_specs=[pl.BlockSpec((TM, D), lambda i: (i, 0))] + _ROUTE_IN_SPECS,
      out_specs=_ROUTE_OUT_SPECS,
      out_shape=_ROUTE_OUT_SHAPE,
      scratch_shapes=_ROUTE_SCRATCH,
      compiler_params=_params(("arbitrary",)),
      name="router",
  )(x, *route_args)


def _slot_of(slot_ref, token):
  return slot_ref[token]


def _tile_rows(ref, row):
  return ref.at[pl.ds(pl.multiple_of(row * ROW_SUB, ROW_SUB), ROW_SUB), :]


def _scatter_copy(slot_ref, step, r, stage, buf, hs_hbm, sem):
  return pltpu.make_async_copy(
      stage.at[buf, pl.ds(r * ROW_SUB, ROW_SUB), :],
      _tile_rows(hs_hbm, _slot_of(slot_ref, step * TS + r)), sem.at[buf])


def _scatter_kernel(slot_ref, pstart_ref, pn_ref, nv_ref,
                    x_ref, g_ref, sh_ref, sc_ref, hs_hbm,
                    stage, zero_s, sem, zsem):
  i = pl.program_id(0)
  n = pl.num_programs(0)
  buf = i % 2

  def pad_copy(c, r):
    return pltpu.make_async_copy(zero_s, _tile_rows(hs_hbm, pstart_ref[c] + r), zsem.at[0])

  def tail_copy(t):
    rows = TMM * ROW_SUB
    return pltpu.make_async_copy(
        stage.at[1], hs_hbm.at[pl.ds(pl.multiple_of(t * rows, rows), rows), :], zsem.at[1])

  @pl.when(i == 0)
  def _():
    stage[...] = jnp.zeros_like(stage)
    zero_s[...] = jnp.zeros_like(zero_s)
    for c in range(N_CLASSES):
      def start(r, carry, c=c):
        pad_copy(c, r).start()
        return carry
      lax.fori_loop(0, pn_ref[c], start, 0)

    def tail_start(t, carry):
      tail_copy(t).start()
      return carry
    lax.fori_loop(nv_ref[0], NT_E, tail_start, 0)
    for c in range(N_CLASSES):
      def wait(r, carry, c=c):
        pad_copy(c, r).wait()
        return carry
      lax.fori_loop(0, pn_ref[c], wait, 0)

    def tail_wait(t, carry):
      tail_copy(t).wait()
      return carry
    lax.fori_loop(nv_ref[0], NT_E, tail_wait, 0)

  h = _norm_mod(x_ref[...], g_ref[...], sh_ref[...], sc_ref[...])
  for s in range(ROW_SUB):
    stage[buf, pl.ds(s, TS, stride=ROW_SUB), :] = h[:, s * 128:(s + 1) * 128]

  def start(r, carry):
    _scatter_copy(slot_ref, i, r, stage, buf, hs_hbm, sem).start()
    return carry
  lax.fori_loop(0, TS, start, 0, unroll=8)

  @pl.when(i > 0)
  def _():
    def wait(r, carry):
      _scatter_copy(slot_ref, i - 1, r, stage, 1 - buf, hs_hbm, sem).wait()
      return carry
    lax.fori_loop(0, TS, wait, 0, unroll=8)

  @pl.when(i == n - 1)
  def _():
    def wait(r, carry):
      _scatter_copy(slot_ref, i, r, stage, buf, hs_hbm, sem).wait()
      return carry
    lax.fori_loop(0, TS, wait, 0, unroll=8)


def _scatter_rows(slot, pad_start, pad_n, n_valid, x, g, shift, scale):
  assert TS == TMM
  grid_spec = pltpu.PrefetchScalarGridSpec(
      num_scalar_prefetch=4,
      grid=(T // TS,),
      in_specs=[
          pl.BlockSpec((TS, D), lambda i, *_: (i, 0)),
          pl.BlockSpec((1, D), lambda i, *_: (0, 0)),
          pl.BlockSpec((None, 1, D), lambda i, *_: (_seg(i, TS), 0, 0)),
          pl.BlockSpec((None, 1, D), lambda i, *_: (_seg(i, TS), 0, 0)),
      ],
      out_specs=pl.BlockSpec(memory_space=pl.ANY),
      scratch_shapes=[
          pltpu.VMEM((2, TS * ROW_SUB, 128), F32),
          pltpu.VMEM((ROW_SUB, 128), F32),
          pltpu.SemaphoreType.DMA((2,)),
          pltpu.SemaphoreType.DMA((2,)),
      ],
  )
  return pl.pallas_call(
      _scatter_kernel,
      grid_spec=grid_spec,
      out_shape=jax.ShapeDtypeStruct((P_PAD * ROW_SUB, 128), F32),
      compiler_params=_params(("arbitrary",)),
      name="scatter_rows",
  )(slot, pad_start, pad_n, n_valid, x, g, shift, scale)


def _router_logits(h, wrh_ref, wrl_ref, br_ref):
  h_hi = h.astype(BF16)
  h_lo = (h - h_hi.astype(F32)).astype(BF16)
  dot = functools.partial(jnp.dot, preferred_element_type=F32)
  return (dot(h_hi, wrh_ref[...]) + dot(h_lo, wrh_ref[...]) + dot(h_hi, wrl_ref[...])
          + br_ref[...])


def _expert_kernel(elo_ref, ehi_ref, nv_ref, hs_ref, wrh_ref, br_ref,
                   wg_lo, wg_hi, wu_lo, wu_hi, wd_lo, wd_hi, y_ref):
  i = pl.program_id(0)

  @pl.when(i < nv_ref[0])
  def _():
    h = jnp.concatenate(
        [hs_ref[pl.ds(s, TMM, stride=ROW_SUB), :] for s in range(ROW_SUB)], axis=1)
    hb = h.astype(BF16)
    logits = jnp.dot(hb, wrh_ref[...], preferred_element_type=F32) + br_ref[...]
    lane = lax.broadcasted_iota(jnp.int32, logits.shape, 1)
    elo, ehi = elo_ref[i], ehi_ref[i]

    def pick(idx):
      return jnp.sum(jnp.where(lane == idx, logits, 0.0), axis=-1, keepdims=True)
    l_g, l_lo, l_hi = pick(elo // EPG), pick(N_GROUPS + elo), pick(N_GROUPS + ehi)
    p_top = 1.0 / jnp.sum(jnp.where(lane < N_GROUPS, jnp.exp(logits - l_g), 0.0),
                          axis=-1, keepdims=True)
    wl = jnp.broadcast_to(p_top / (1.0 + jnp.exp(l_hi - l_lo)), (TMM, D_EXPERT))
    wh = jnp.broadcast_to(p_top / (1.0 + jnp.exp(l_lo - l_hi)), (TMM, D_EXPERT))
    dot = functools.partial(jnp.dot, preferred_element_type=F32)
    a_lo = (_silu(dot(hb, wg_lo[...])) * dot(hb, wu_lo[...])) * wl
    a_hi = (_silu(dot(hb, wg_hi[...])) * dot(hb, wu_hi[...])) * wh
    y = dot(a_lo.astype(BF16), wd_lo[...]) + dot(a_hi.astype(BF16), wd_hi[...])
    for s in range(ROW_SUB):
      y_ref[pl.ds(s, TMM, stride=ROW_SUB), :] = y[:, s * 128:(s + 1) * 128]

  @pl.when(i >= nv_ref[0])
  def _():
    y_ref[...] = jnp.zeros_like(y_ref)


def _experts(layer, tile_elo, tile_ehi, n_valid, hs, wr_hi, br, w_gate, w_up, w_down):
  up_spec_lo = pl.BlockSpec((None, None, D, D_EXPERT), lambda i, elo, ehi, nv: (layer, elo[i], 0, 0))
  up_spec_hi = pl.BlockSpec((None, None, D, D_EXPERT), lambda i, elo, ehi, nv: (layer, ehi[i], 0, 0))
  dn_spec_lo = pl.BlockSpec((None, None, D_EXPERT, D), lambda i, elo, ehi, nv: (layer, elo[i], 0, 0))
  dn_spec_hi = pl.BlockSpec((None, None, D_EXPERT, D), lambda i, elo, ehi, nv: (layer, ehi[i], 0, 0))
  grid_spec = pltpu.PrefetchScalarGridSpec(
      num_scalar_prefetch=3,
      grid=(NT_E,),
      in_specs=[
          pl.BlockSpec((TMM * ROW_SUB, 128), lambda i, elo, ehi, nv: (i, 0)),
          pl.BlockSpec((D, 128), lambda i, elo, ehi, nv: (0, 0)),
          pl.BlockSpec((1, 128), lambda i, elo, ehi, nv: (0, 0)),
          up_spec_lo, up_spec_hi, up_spec_lo, up_spec_hi, dn_spec_lo, dn_spec_hi,
      ],
      out_specs=pl.BlockSpec((TMM * ROW_SUB, 128), lambda i, elo, ehi, nv: (i, 0)),
  )
  return pl.pallas_call(
      _expert_kernel,
      grid_spec=grid_spec,
      out_shape=jax.ShapeDtypeStruct((P_PAD * ROW_SUB, 128), F32),
      compiler_params=_params(("arbitrary",)),
      name="experts",
  )(tile_elo, tile_ehi, n_valid, hs, wr_hi, br,
    w_gate, w_gate, w_up, w_up, w_down, w_down)


def _gather_copy(slot_ref, step, r, y_hbm, ybuf, buf, sem):
  return pltpu.make_async_copy(
      _tile_rows(y_hbm, _slot_of(slot_ref, step * TC + r)),
      ybuf.at[buf, pl.ds(r * ROW_SUB, ROW_SUB), :], sem.at[buf])


def _combine_kernel(slot_ref, x_ref, gate_ref, nf_ref, y_hbm, *rest, final):
  if final:
    op_ref, os_ref, ybuf, sem = rest
  else:
    o_ref, ybuf, sem = rest
  i = pl.program_id(0)
  n = pl.num_programs(0)
  buf = i % 2

  def start(step, b):
    def body(r, carry):
      _gather_copy(slot_ref, step, r, y_hbm, ybuf, b, sem).start()
      return carry
    lax.fori_loop(0, TC, body, 0, unroll=8)

  @pl.when(i == 0)
  def _():
    start(0, 0)

  @pl.when(i + 1 < n)
  def _():
    start(i + 1, 1 - buf)

  def wait(r, carry):
    _gather_copy(slot_ref, i, r, y_hbm, ybuf, buf, sem).wait()
    return carry
  lax.fori_loop(0, TC, wait, 0, unroll=8)

  y = jnp.concatenate(
      [ybuf[buf, pl.ds(s, TC, stride=ROW_SUB), :] for s in range(ROW_SUB)], axis=1)
  x = x_ref[...] + gate_ref[...] * y
  if not final:
    o_ref[...] = x
    return
  x = (x * lax.rsqrt(jnp.mean(x * x, axis=-1, keepdims=True) + EPS)) * nf_ref[...]
  is_prompt = i < T_P // TC

  @pl.when(is_prompt)
  def _():
    op_ref[...] = x

  @pl.when(jnp.logical_not(is_prompt))
  def _():
    os_ref[...] = x


def _combine(slot, x, gate, norm_f, y_sorted, final):
  npt = T_P // TC
  if final:
    out_specs = (pl.BlockSpec((TC, D), lambda i, s: (jnp.minimum(i, npt - 1), 0)),
                 pl.BlockSpec((TC, D), lambda i, s: (jnp.maximum(i - npt, 0), 0)))
    out_shape = (jax.ShapeDtypeStruct((T_P, D), F32), jax.ShapeDtypeStruct((T_S, D), F32))
  else:
    out_specs = pl.BlockSpec((TC, D), lambda i, s: (i, 0))
    out_shape = jax.ShapeDtypeStruct((T, D), F32)
  grid_spec = pltpu.PrefetchScalarGridSpec(
      num_scalar_prefetch=1,
      grid=(T // TC,),
      in_specs=[
          pl.BlockSpec((TC, D), lambda i, s: (i, 0)),
          pl.BlockSpec((None, 1, D), lambda i, s: (_seg(i, TC), 0, 0)),
          pl.BlockSpec((1, D), lambda i, s: (0, 0)),
          pl.BlockSpec(memory_space=pl.ANY),
      ],
      out_specs=out_specs,
      scratch_shapes=[pltpu.VMEM((2, TC * ROW_SUB, 128), F32), pltpu.SemaphoreType.DMA((2,))],
  )
  return pl.pallas_call(
      functools.partial(_combine_kernel, final=final),
      grid_spec=grid_spec,
      out_shape=out_shape,
      compiler_params=_params(("arbitrary",)),
      name="combine_final" if final else "combine",
  )(slot, x, gate, norm_f, y_sorted)


def _class_experts():
  lo, hi = [], []
  for g in range(N_GROUPS):
    for a in range(EPG):
      for b in range(a + 1, EPG):
        lo.append(g * EPG + a)
        hi.append(g * EPG + b)
  return np.asarray(lo, np.int32), np.asarray(hi, np.int32)


def _moe_layer(layer, x, routing, route_args, gate, w_gate, w_up, w_down, norm_f):
  final = layer == DEPTH - 1
  g, shift, scale, wr_hi, _, br, _ = route_args
  info, cnt = routing

  cls = info[:, 0, :].astype(jnp.int32)
  rank = info[:, 1, :].astype(jnp.int32)
  counts = cnt[:N_CLASSES, 0].astype(jnp.int32)
  tiles = (counts + TMM - 1) // TMM
  tile_end = jnp.cumsum(tiles)
  offs = (tile_end - tiles) * TMM
  slot = rank
  for k in range(N_CLASSES):
    slot = slot + jnp.where(cls == k, offs[k], 0)
  slot = slot.reshape(T)
  n_valid = tile_end[-1]
  tile_ids = jnp.minimum(jnp.arange(NT_E, dtype=jnp.int32), n_valid - 1)
  tile_cls = jnp.sum((tile_ids[:, None] >= tile_end[None, :]).astype(jnp.int32), axis=1)
  tile_cls = jnp.minimum(tile_cls, N_CLASSES - 1)
  cls_lo, cls_hi = _class_experts()
  tile_elo = jnp.asarray(cls_lo)[tile_cls]
  tile_ehi = jnp.asarray(cls_hi)[tile_cls]

  n_valid = n_valid.reshape(1)
  hs = _scatter_rows(slot, offs + counts, tiles * TMM - counts, n_valid, x, g, shift, scale)
  y_sorted = _experts(layer, tile_elo, tile_ehi, n_valid, hs, wr_hi, br, w_gate, w_up, w_down)
  return _combine(slot, x, gate, norm_f, y_sorted, final)


def kernel(x_prompt, x_sample, cache_k, cache_v, state_ret, c, c_ctx, norm1, norm2, w_ada, b_ada, attn_w_qkv, attn_q_norm, attn_k_norm, attn_w_o, pool_w, pool_scale, ret_w_in, ret_decay_logit, ret_norm, ret_w_out, moe_w_router_g, moe_b_router_g, moe_w_router_e, moe_b_router_e, moe_w_gate, moe_w_up, moe_w_down, norm_f):
  x = (x_prompt.reshape(T_P, D), x_sample.reshape(T_S, D))
  cond8 = jnp.concatenate([c_ctx[None, :], c, jnp.zeros((N_SEG - 1 - NB_S, D), F32)], axis=0)
  mods = _adaln(cond8, w_ada, b_ada)
  mods = mods.reshape(DEPTH, N_SEG, 6, 1, D).transpose(0, 2, 1, 3, 4)
  rope = _rope_tables()
  tri = jnp.triu(jnp.ones((TM, TM), BF16), 1)
  pad_r = 128 - N_GROUPS - N_EXPERTS
  norm_f2 = norm_f.reshape(1, D)

  w_gate, w_up, w_down = moe_w_gate.astype(BF16), moe_w_up.astype(BF16), moe_w_down.astype(BF16)

  new_k, new_v, new_s = [], [], []
  for i in range(DEPTH):
    kind, j = i % 3, i // 3
    m = mods[i]
    g1 = norm1[i].reshape(1, D)
    wr = jnp.concatenate([moe_w_router_g[i], moe_w_router_e[i], jnp.zeros((D, pad_r), F32)], axis=1)
    br = jnp.concatenate([moe_b_router_g[i], moe_b_router_e[i], jnp.zeros((pad_r,), F32)]).reshape(1, 128)
    wr_hi = wr.astype(BF16)
    wr_lo = (wr - wr_hi.astype(F32)).astype(BF16)
    route_args = (norm2[i].reshape(1, D), m[3], m[4], wr_hi, wr_lo, br, tri)
    if kind == 0:
      qkv = _nm_matmul(x, g1, m[0], m[1], attn_w_qkv[j].astype(BF16), TM, "qkv_proj")
      qn = attn_q_norm[j].reshape(1, HEAD_DIM)
      kn = attn_k_norm[j].reshape(1, HEAD_DIM)
      o_p, kc, vc = _attn_prompt(qkv, qn, kn)
      ck = cache_k[:, j].reshape(NB_S, PAST, N_KV * HEAD_DIM)
      cv = cache_v[:, j].reshape(NB_S, PAST, N_KV * HEAD_DIM)
      o_s = _attn_latent(qkv, ck, cv, qn, kn, rope)
      new_k.append(kc.reshape(NB_P, SEQ_P, N_KV, HEAD_DIM))
      new_v.append(vc.reshape(NB_P, SEQ_P, N_KV, HEAD_DIM))
      x, *routing = _mm_res_route((o_p, o_s), attn_w_o[j].astype(BF16), x, m[2], route_args, "attn_out")
    elif kind == 1:
      x = _pool_layer(x, g1, m[0], m[1], m[2], pool_w[j].astype(BF16), pool_scale[j].reshape(1, D))
      routing = _router(x, route_args)
    else:
      p = _nm_matmul(x, g1, m[0], m[1], ret_w_in[j].astype(BF16), 256, "ret_proj")
      dl = jnp.broadcast_to(ret_decay_logit[j].T[:, :, None, None], (RET_HEADS, 2, 1, RET_DK))
      ng = ret_norm[j].reshape(1, RET_HEADS * RET_DV)
      y_p, s_new = _retention(p, dl, ng, None, True)
      y_s = _retention(p, dl, ng, state_ret[:, j], False)
      new_s.append(s_new)
      x, *routing = _mm_res_route((y_p, y_s), ret_w_out[j].astype(BF16), x, m[2], route_args, "ret_out")
    x = _moe_layer(i, x, routing, route_args, m[5], w_gate, w_up, w_down, norm_f2)

  y_prompt = x[0].reshape(NB_P, SEQ_P, D)
  y_sample = x[1].reshape(NB_S, SEQ_S, D)
  new_cache_k = jnp.stack(new_k, axis=1)
  new_cache_v = jnp.stack(new_v, axis=1)
  assert len(new_s) == 1
  new_state_ret = new_s[0].reshape(NB_P, 1, 2, RET_HEADS, RET_DK, RET_DV)
  return (y_prompt, y_sample, new_cache_k, new_cache_v, new_state_ret)
```

```python
import functools

import jax
import jax.numpy as jnp
import numpy as np
from jax import lax
from jax.experimental import pallas as pl
from jax.experimental.pallas import tpu as pltpu

F32 = jnp.float32
BF16 = jnp.bfloat16

D = 1024
NB_P, SEQ_P = 32, 256
NB_S, SEQ_S = 4, 2048
T_P = NB_P * SEQ_P
T_S = NB_S * SEQ_S
T = T_P + T_S
DEPTH = 4
GRID_W = 64
HEAD_DIM = 128
N_HEADS = 8
N_KV = 2
KV_GROUP = N_HEADS // N_KV
PAST = 256
ROPE_THETA = 10000.0
POOL_WINDOWS = (2, 4, 8, 16)
POOL_GROUP = D // 4
POOL_HALO = 8
RET_HEADS = 8
RET_DK = 128
RET_DV = 256
RET_CHUNK = 128
RET_HEADS_PER_STEP_PROMPT = 4
RET_HEADS_PER_STEP_LATENT = 2
N_GROUPS = 4
EPG = 4
N_EXPERTS = 16
D_EXPERT = 256
N_PAIRS = 6
N_CLASSES = N_GROUPS * N_PAIRS
ROUTER_ROWS = 32
EPS = 1e-6
LOG2E = 1.4426950408889634
NEG = -1e30
N_SEG = 8

VMEM_LIMIT_BYTES = 52 * 1024 * 1024

TM = 512
TQ = 256
ATTN_UNIT_Q = 128
ATTN_KEY_CHUNK = 256
ATTN_HEAD_STACK = 2
TP = 256
TMM = 256
TS = 256
TC = 256
ROW_SUB = 8
SCATTER_DMA_PRIORITY = 1
PACK_SUB = D // 2 // 128
INFO_SUB = PACK_SUB
P_PAD = T + N_CLASSES * TMM
NT_E = P_PAD // TMM


def _params(sem):
  return pltpu.CompilerParams(dimension_semantics=sem, vmem_limit_bytes=VMEM_LIMIT_BYTES)


def _seg(i, tm):
  npt = T_P // tm
  return jnp.where(i < npt, 0, (i - npt) // (SEQ_S // tm) + 1)


def _norm_mod(x, g, shift, scale):
  r = lax.rsqrt(jnp.mean(x * x, axis=-1, keepdims=True) + EPS)
  return ((x * r) * g) * (1.0 + scale) + shift


def _silu(x):
  return x * (1.0 / (1.0 + jnp.exp(-x)))


def _adaln_kernel(c_ref, w_ref, b_ref, o_ref):
  s = _silu(c_ref[...]).astype(BF16)
  o_ref[...] = jnp.dot(s, w_ref[...].astype(BF16), preferred_element_type=F32) + b_ref[...]


def _adaln(cond8, w_ada, b_ada):
  tn = 1536
  return pl.pallas_call(
      _adaln_kernel,
      grid=(DEPTH, 6 * D // tn),
      in_specs=[
          pl.BlockSpec((N_SEG, D), lambda l, j: (0, 0)),
          pl.BlockSpec((None, D, tn), lambda l, j: (l, 0, j)),
          pl.BlockSpec((None, 1, tn), lambda l, j: (l, 0, j)),
      ],
      out_specs=pl.BlockSpec((None, N_SEG, tn), lambda l, j: (l, 0, j)),
      out_shape=jax.ShapeDtypeStruct((DEPTH, N_SEG, 6 * D), F32),
      compiler_params=_params(("arbitrary", "arbitrary")),
      name="adaln",
  )(cond8, w_ada, b_ada.reshape(DEPTH, 1, 6 * D))


def _token_rows(x, tm, width):
  npt = T_P // tm
  if isinstance(x, tuple):
    arrays = x
    latent_map = lambda i: (jnp.maximum(i - npt, 0), 0)
  else:
    arrays = (x, x)
    latent_map = lambda i: (jnp.maximum(i, npt), 0)
  specs = [pl.BlockSpec((tm, width), lambda i: (jnp.minimum(i, npt - 1), 0)),
           pl.BlockSpec((tm, width), latent_map)]
  return arrays, specs


def _pick_rows(p_ref, s_ref, tm):
  return jnp.where(pl.program_id(0) < T_P // tm, p_ref[...], s_ref[...])


def _nm_matmul_kernel(xp_ref, xs_ref, g_ref, sh_ref, sc_ref, w_ref, o_ref, *, tm, n_chunk):
  x = _pick_rows(xp_ref, xs_ref, tm)
  h = _norm_mod(x, g_ref[...], sh_ref[...], sc_ref[...]).astype(BF16)
  n = w_ref.shape[1]
  for c in range(0, n, n_chunk):
    o_ref[:, c:c + n_chunk] = jnp.dot(
        h, w_ref[:, c:c + n_chunk], preferred_element_type=F32).astype(o_ref.dtype)


def _nm_matmul(x, g, shift, scale, w, tm, name):
  n = w.shape[1]
  x_arrays, x_specs = _token_rows(x, tm, D)
  return pl.pallas_call(
      functools.partial(_nm_matmul_kernel, tm=tm, n_chunk=512),
      grid=(T // tm,),
      in_specs=x_specs + [
          pl.BlockSpec((1, D), lambda i: (0, 0)),
          pl.BlockSpec((None, 1, D), lambda i: (_seg(i, tm), 0, 0)),
          pl.BlockSpec((None, 1, D), lambda i: (_seg(i, tm), 0, 0)),
          pl.BlockSpec((D, n), lambda i: (0, 0)),
      ],
      out_specs=pl.BlockSpec((tm, n), lambda i: (i, 0)),
      out_shape=jax.ShapeDtypeStruct((T, n), BF16),
      compiler_params=_params(("arbitrary",)),
      name=name,
  )(*x_arrays, g, shift, scale, w)


def _mm_res_route_kernel(ap_ref, as_ref, xp_ref, xs_ref, w_ref, gate_ref, *rest):
  n_route_in = len(_ROUTE_IN_SPECS)
  route_in, (o_ref, info_ref, cnt_ref, carry_s) = rest[:n_route_in], rest[n_route_in:]
  a = _pick_rows(ap_ref, as_ref, TM)
  x = _pick_rows(xp_ref, xs_ref, TM)
  x_new = x + gate_ref[...] * jnp.dot(a, w_ref[...], preferred_element_type=F32)
  o_ref[...] = x_new
  _route_tile(x_new, *route_in, info_ref, cnt_ref, carry_s)


def _mm_res_route(a, w, x, gate, route_args, name):
  k = w.shape[0]
  a_arrays, a_specs = _token_rows(a, TM, k)
  x_arrays, x_specs = _token_rows(x, TM, D)
  return pl.pallas_call(
      _mm_res_route_kernel,
      grid=(T // TM,),
      in_specs=a_specs + x_specs + [
          pl.BlockSpec((k, D), lambda i: (0, 0)),
          pl.BlockSpec((None, 1, D), lambda i: (_seg(i, TM), 0, 0)),
      ] + _ROUTE_IN_SPECS,
      out_specs=(pl.BlockSpec((TM, D), lambda i: (i, 0)),) + _ROUTE_OUT_SPECS,
      out_shape=(jax.ShapeDtypeStruct((T, D), F32),) + _ROUTE_OUT_SHAPE,
      scratch_shapes=_ROUTE_SCRATCH,
      compiler_params=_params(("arbitrary",)),
      name=name,
  )(*a_arrays, *x_arrays, w, gate, *route_args)


def _rope(x, c, a, b):
  return x * c + pltpu.roll(x, 96, 1) * a + pltpu.roll(x, 32, 1) * b


def _head_norm(x, w):
  return (x * lax.rsqrt(jnp.mean(x * x, axis=-1, keepdims=True) + EPS)) * w


def _attn_latent_kernel(q_ref, kn_ref, vn_ref, ck_ref, cv_ref, qw_ref, kw_ref,
                        cq_ref, aq_ref, bq_ref, ckk_ref, akk_ref, bkk_ref,
                        o_ref, k_s, vt_s, *st_refs, tq):
  @pl.when(pl.program_id(2) == 0)
  def _():
    k = _rope(_head_norm(kn_ref[...].astype(F32), kw_ref[...]), ckk_ref[...], akk_ref[...], bkk_ref[...])
    k_s[0:PAST, :] = ck_ref[...].astype(BF16)
    k_s[PAST:, :] = k.astype(BF16)
    vt_s[:, 0:PAST] = cv_ref[...].T.astype(BF16)
    vt_s[:, PAST:] = vn_ref[...].astype(F32).T.astype(BF16)

  qa = q_ref[...].astype(F32)
  lk = k_s.shape[0]

  def prep_q(h):
    qh = _head_norm(qa[:, h * HEAD_DIM:(h + 1) * HEAD_DIM], qw_ref[...])
    qh = _rope(qh, cq_ref[...], aq_ref[...], bq_ref[...])
    return (qh * (HEAD_DIM ** -0.5 * LOG2E)).astype(BF16)

  nq = ATTN_HEAD_STACK * ATTN_UNIT_Q
  chunks = [(c0, min(c0 + ATTN_KEY_CHUNK, lk)) for c0 in range(0, lk, ATTN_KEY_CHUNK)]
  qh_all = [prep_q(h) for h in range(KV_GROUP)]
  units = [(h0, r0) for r0 in range(0, tq, ATTN_UNIT_Q) for h0 in range(0, KV_GROUP, ATTN_HEAD_STACK)]

  def score_chunk(u, c0, c1, m):
    h0, r0 = units[u]
    qp = jnp.concatenate([qh_all[h0 + d][r0:r0 + ATTN_UNIT_Q] for d in range(ATTN_HEAD_STACK)], axis=0)
    st = lax.dot_general(k_s[c0:c1, :], qp, (((1,), (1,)), ((), ())),
                         preferred_element_type=F32)
    st_refs[u % 2][c0:c1, :] = st
    return jnp.maximum(m, jnp.max(st, axis=0, keepdims=True))

  def value_chunk(u, c0, c1, m, l, acc):
    pt = jnp.exp2(st_refs[u % 2][c0:c1, :] - m)
    return (l + jnp.sum(pt, axis=0, keepdims=True),
            acc + jnp.dot(vt_s[:, c0:c1], pt.astype(BF16), preferred_element_type=F32))

  def finish(u, l, acc):
    h0, r0 = units[u]
    o = acc / l
    for d in range(ATTN_HEAD_STACK):
      h = h0 + d
      o_ref[r0:r0 + ATTN_UNIT_Q, h * HEAD_DIM:(h + 1) * HEAD_DIM] = (
          o[:, d * ATTN_UNIT_Q:(d + 1) * ATTN_UNIT_Q].T.astype(BF16))

  m_prev = None
  for u in range(len(units) + 1):
    m = jnp.full((1, nq), NEG, F32)
    l = jnp.zeros((1, nq), F32)
    acc = jnp.zeros((HEAD_DIM, nq), F32)
    for c0, c1 in chunks:
      if u < len(units):
        m = score_chunk(u, c0, c1, m)
      if u > 0:
        l, acc = value_chunk(u - 1, c0, c1, m_prev, l, acc)
    if u > 0:
      finish(u - 1, l, acc)
    m_prev = m


def _attn_scratch(lk):
  return ([pltpu.VMEM((lk, HEAD_DIM), BF16), pltpu.VMEM((HEAD_DIM, lk), BF16)]
          + [pltpu.VMEM((lk, ATTN_HEAD_STACK * ATTN_UNIT_Q), F32) for _ in range(2)])


def _rope_tables():
  rows = SEQ_S // GRID_W
  t_row = jnp.broadcast_to(jnp.arange(rows)[:, None], (rows, GRID_W)).reshape(-1)
  t_col = jnp.broadcast_to(jnp.arange(GRID_W)[None, :], (rows, GRID_W)).reshape(-1)
  nf = HEAD_DIM // 4
  inv = ROPE_THETA ** (-jnp.arange(nf, dtype=F32) / nf)
  ang_r = t_row.astype(F32)[:, None] * inv[None, :]
  ang_c = t_col.astype(F32)[:, None] * inv[None, :]
  cr, sr, cc, sc = jnp.cos(ang_r), jnp.sin(ang_r), jnp.cos(ang_c), jnp.sin(ang_c)
  z = jnp.zeros_like(sr)
  c = jnp.concatenate([cr, cr, cc, cc], axis=1)
  a = jnp.concatenate([-sr, z, -sc, z], axis=1)
  b = jnp.concatenate([z, sr, z, sc], axis=1)
  return c, a, b


def _attn_prompt_kernel(qkv_ref, qw_ref, kw_ref, o_ref, ko_ref, vo_ref):
  nq = N_HEADS * HEAD_DIM
  nk = N_KV * HEAD_DIM
  for kh in range(N_KV):
    kcols = slice(nq + kh * HEAD_DIM, nq + (kh + 1) * HEAD_DIM)
    vcols = slice(nq + nk + kh * HEAD_DIM, nq + nk + (kh + 1) * HEAD_DIM)
    k = _head_norm(qkv_ref[:, kcols].astype(F32), kw_ref[...])
    v = qkv_ref[:, vcols].astype(F32)
    ko_ref[pl.ds(kh, SEQ_P, stride=N_KV), :] = k
    vo_ref[pl.ds(kh, SEQ_P, stride=N_KV), :] = v
    kb = k.astype(BF16)
    vt = v.T.astype(BF16)
    for h in range(kh * KV_GROUP, (kh + 1) * KV_GROUP):
      hcols = slice(h * HEAD_DIM, (h + 1) * HEAD_DIM)
      qh = _head_norm(qkv_ref[:, hcols].astype(F32), qw_ref[...])
      qh = (qh * (HEAD_DIM ** -0.5 * LOG2E)).astype(BF16)
      st = lax.dot_general(kb, qh, (((1,), (1,)), ((), ())), preferred_element_type=F32)
      pt = jnp.exp2(st - jnp.max(st, axis=0, keepdims=True))
      l = jnp.sum(pt, axis=0, keepdims=True)
      acc = jnp.dot(vt, pt.astype(BF16), preferred_element_type=F32)
      o_ref[:, hcols] = (acc / l).T.astype(BF16)


def _attn_prompt(qkv, q_norm, k_norm):
  width = (N_HEADS + 2 * N_KV) * HEAD_DIM
  out_shapes = (
      jax.ShapeDtypeStruct((T_P, N_HEADS * HEAD_DIM), BF16),
      jax.ShapeDtypeStruct((NB_P, SEQ_P * N_KV, HEAD_DIM), F32),
      jax.ShapeDtypeStruct((NB_P, SEQ_P * N_KV, HEAD_DIM), F32),
  )
  return pl.pallas_call(
      _attn_prompt_kernel,
      grid=(NB_P,),
      in_specs=[
          pl.BlockSpec((SEQ_P, width), lambda b: (b, 0)),
          pl.BlockSpec((1, HEAD_DIM), lambda b: (0, 0)),
          pl.BlockSpec((1, HEAD_DIM), lambda b: (0, 0)),
      ],
      out_specs=(
          pl.BlockSpec((SEQ_P, N_HEADS * HEAD_DIM), lambda b: (b, 0)),
          pl.BlockSpec((None, SEQ_P * N_KV, HEAD_DIM), lambda b: (b, 0, 0)),
          pl.BlockSpec((None, SEQ_P * N_KV, HEAD_DIM), lambda b: (b, 0, 0)),
      ),
      out_shape=out_shapes,
      compiler_params=_params(("arbitrary",)),
      name="attn_prompt",
  )(qkv, q_norm, k_norm)


def _attn_latent(qkv, cache_k, cache_v, q_norm, k_norm, rope):
  kcol = N_HEADS
  vcol = kcol + N_KV
  nq = SEQ_S // TQ
  row0 = T_P // TQ
  seq0 = T_P // SEQ_S
  lk = PAST + SEQ_S
  c, a, b = rope
  tab_q = pl.BlockSpec((TQ, HEAD_DIM), lambda bb, h, i: (i, 0))
  tab_k = pl.BlockSpec((SEQ_S, HEAD_DIM), lambda bb, h, i: (0, 0))
  return pl.pallas_call(
      functools.partial(_attn_latent_kernel, tq=TQ),
      grid=(NB_S, N_KV, nq),
      in_specs=[
          pl.BlockSpec((TQ, KV_GROUP * HEAD_DIM), lambda bb, h, i: (row0 + bb * nq + i, h)),
          pl.BlockSpec((SEQ_S, HEAD_DIM), lambda bb, h, i: (seq0 + bb, kcol + h)),
          pl.BlockSpec((SEQ_S, HEAD_DIM), lambda bb, h, i: (seq0 + bb, vcol + h)),
          pl.BlockSpec((None, PAST, HEAD_DIM), lambda bb, h, i: (bb, 0, h)),
          pl.BlockSpec((None, PAST, HEAD_DIM), lambda bb, h, i: (bb, 0, h)),
          pl.BlockSpec((1, HEAD_DIM), lambda bb, h, i: (0, 0)),
          pl.BlockSpec((1, HEAD_DIM), lambda bb, h, i: (0, 0)),
          tab_q, tab_q, tab_q, tab_k, tab_k, tab_k,
      ],
      out_specs=pl.BlockSpec((TQ, KV_GROUP * HEAD_DIM), lambda bb, h, i: (bb * nq + i, h)),
      out_shape=jax.ShapeDtypeStruct((T_S, N_HEADS * HEAD_DIM), BF16),
      scratch_shapes=_attn_scratch(lk),
      compiler_params=_params(("arbitrary", "arbitrary", "arbitrary")),
      name="attn_latent",
  )(qkv, qkv, qkv, cache_k, cache_v, q_norm, k_norm, c, a, b, c, a, b)


def _pool_kernel(x_ref, xp_ref, xn_ref, g_ref, sh_ref, sc_ref, gate_ref, w_ref, ps_ref, o_ref):
  t = pl.program_id(0)
  npt = T_P // TP
  tiles_s = SEQ_S // TP
  is_p = t < npt
  pos = jnp.where(is_p, 0, (t - npt) % tiles_s)
  ntile = jnp.where(is_p, SEQ_P // TP, tiles_s)
  seq_len = ntile * TP
  keep_prev = jnp.where(pos == 0, 0.0, 1.0)
  keep_next = jnp.where(pos == ntile - 1, 0.0, 1.0)

  g, sh, sc = g_ref[...], sh_ref[...], sc_ref[...]
  x = x_ref[...]
  h = _norm_mod(x, g, sh, sc)
  hp = _norm_mod(xp_ref[...], g, sh, sc) * keep_prev
  hn = _norm_mod(xn_ref[...], g, sh, sc) * keep_next
  ext = jnp.concatenate([hp, h, hn], axis=0)
  n_ext = TP + 2 * POOL_HALO
  tseq = pos * TP + lax.broadcasted_iota(jnp.int32, (TP, 128), 0)

  outs = []
  for gi, win in enumerate(POOL_WINDOWS):
    lo, hi = gi * POOL_GROUP, (gi + 1) * POOL_GROUP
    acc = ext[:, lo:hi]
    span = 1
    while span < win:
      acc = acc + pltpu.roll(acc, n_ext - span, 0)
      span *= 2
    start = POOL_HALO - win // 2
    if start:
      acc = pltpu.roll(acc, n_ext - start, 0)
    ssum = acc[0:TP]
    cnt = (jnp.minimum(tseq + win // 2, seq_len) - jnp.maximum(tseq - win // 2, 0)).astype(F32)
    cnt = jnp.concatenate([cnt] * (POOL_GROUP // 128), axis=1)
    dlt = (ssum / cnt - h[:, lo:hi]).astype(BF16)
    outs.append(jnp.dot(dlt, w_ref[gi], preferred_element_type=F32))
  y = jnp.concatenate(outs, axis=1) * ps_ref[...]
  o_ref[...] = x + gate_ref[...] * y


def _pool_layer(x, g, shift, scale, gate, w, pscale):
  hb = TP // POOL_HALO
  last = T // POOL_HALO - 1
  seg = lambda i: _seg(i, TP)
  return pl.pallas_call(
      _pool_kernel,
      grid=(T // TP,),
      in_specs=[
          pl.BlockSpec((TP, D), lambda i: (i, 0)),
          pl.BlockSpec((POOL_HALO, D), lambda i: (jnp.maximum(i * hb - 1, 0), 0)),
          pl.BlockSpec((POOL_HALO, D), lambda i: (jnp.minimum((i + 1) * hb, last), 0)),
          pl.BlockSpec((1, D), lambda i: (0, 0)),
          pl.BlockSpec((None, 1, D), lambda i: (seg(i), 0, 0)),
          pl.BlockSpec((None, 1, D), lambda i: (seg(i), 0, 0)),
          pl.BlockSpec((None, 1, D), lambda i: (seg(i), 0, 0)),
          pl.BlockSpec((4, POOL_GROUP, POOL_GROUP), lambda i: (0, 0, 0)),
          pl.BlockSpec((1, D), lambda i: (0, 0)),
      ],
      out_specs=pl.BlockSpec((TP, D), lambda i: (i, 0)),
      out_shape=jax.ShapeDtypeStruct((T, D), F32),
      compiler_params=_params(("arbitrary",)),
      name="pool",
  )(x, x, x, g, shift, scale, gate, w, pscale)


def _ret_kernel(*refs, seq_len, has_s0, nh):
  it = iter(refs)
  q_ref, k_ref, v_ref, gate_ref, dl_ref, ng_ref = [next(it) for _ in range(6)]
  s0_ref = next(it) if has_s0 else None
  y_ref = next(it)
  so_ref = None if has_s0 else next(it)
  u_s = next(it)
  dec_s, cdec_s = next(it), next(it)
  c = RET_CHUNK
  n = seq_len // c
  kscale = RET_DK ** -0.5
  nt = (((1,), (1,)), ((), ()))
  tn = (((0,), (0,)), ((), ()))

  @pl.when(pl.program_id(1) == 0)
  def _():
    ri = lax.broadcasted_iota(jnp.int32, (c, c), 0).astype(F32)
    ci = lax.broadcasted_iota(jnp.int32, (c, c), 1).astype(F32)
    ri2 = lax.broadcasted_iota(jnp.int32, (c, RET_DV), 0).astype(F32)
    for hh in range(nh):
      lg = -jnp.log1p(jnp.exp(-dl_ref[hh]))
      lgf, lgb = lg[0], lg[1]
      lgf2 = jnp.concatenate([lgf, lgf], axis=1)
      lgb2 = jnp.concatenate([lgb, lgb], axis=1)
      dec_s[hh, 0] = jnp.concatenate(
          [jnp.exp((c - 1.0 - ri) * lgf), jnp.exp(ri * lgb)], axis=1) * kscale
      intra = (jnp.where(ri >= ci, jnp.exp((ri - ci) * lgf), 0.0)
               + jnp.where(ri <= ci, jnp.exp((ci - ri) * lgb), 0.0)) * kscale
      dec_s[hh, 1] = jnp.concatenate([intra, intra], axis=1)
      dec_s[hh, 2] = jnp.exp((ri2 + 1.0) * lgf2)
      dec_s[hh, 3] = jnp.exp((c - ri2) * lgb2)
      cdec_s[hh, 0] = jnp.exp(c * lgf2)
      cdec_s[hh, 1] = jnp.exp(c * lgb2)

  for hh in range(nh):
    qcols = slice(hh * RET_DK, (hh + 1) * RET_DK)
    vcols = slice(hh * RET_DV, (hh + 1) * RET_DV)
    kd2 = dec_s[hh, 0]

    for j in range(n):
      rows = slice(j * c, (j + 1) * c)
      kc = k_ref[rows, qcols].astype(F32)
      k2 = (jnp.concatenate([kc, kc], axis=1) * kd2).astype(BF16)
      u = lax.dot_general(k2, v_ref[rows, vcols], tn, preferred_element_type=F32)
      u_s[0, hh, j] = u[0:RET_DK]
      u_s[1, hh, j] = u[RET_DK:]

    for d, order in ((0, range(n)), (1, reversed(range(n)))):
      cdec = cdec_s[hh, d]
      st = s0_ref[d, hh] if has_s0 else jnp.zeros((RET_DK, RET_DV), F32)
      for j in order:
        u = u_s[d, hh, j]
        u_s[d, hh, j] = st
        st = st * cdec + u
      if not has_s0:
        so_ref[d, hh] = st

    intra = dec_s[hh, 1][:, :c]
    qd_f = dec_s[hh, 2]
    qd_b = dec_s[hh, 3]
    for j in range(n):
      rows = slice(j * c, (j + 1) * c)
      qc = q_ref[rows, qcols]
      vc = v_ref[rows, vcols]
      s = lax.dot_general(qc, k_ref[rows, qcols], nt, preferred_element_type=F32) * intra
      st2 = jnp.concatenate([u_s[0, hh, j], u_s[1, hh, j]], axis=1).astype(BF16)
      inter = jnp.dot(qc, st2, preferred_element_type=F32)
      o = (jnp.dot(s.astype(BF16), vc, preferred_element_type=F32)
           + inter[:, :RET_DV] * qd_f + inter[:, RET_DV:] * qd_b)
      mu = jnp.mean(o, axis=-1, keepdims=True)
      dv = o - mu
      var = jnp.mean(dv * dv, axis=-1, keepdims=True)
      on = (dv * lax.rsqrt(var + EPS)) * ng_ref[:, vcols]
      y_ref[rows, vcols] = (on * _silu(gate_ref[rows, vcols].astype(F32))).astype(BF16)


def _retention(p, dl, ng, state0, prompt):
  seq_len = SEQ_P if prompt else SEQ_S
  nb = NB_P if prompt else NB_S
  nh = RET_HEADS_PER_STEP_PROMPT if prompt else RET_HEADS_PER_STEP_LATENT
  row0 = 0 if prompt else T_P // SEQ_S
  hb = RET_HEADS // nh
  kcol = hb
  vcol = (2 * RET_HEADS * RET_DK) // (nh * RET_DV)
  gcol = vcol + hb
  in_specs = [
      pl.BlockSpec((seq_len, nh * RET_DK), lambda h, b: (row0 + b, h)),
      pl.BlockSpec((seq_len, nh * RET_DK), lambda h, b: (row0 + b, kcol + h)),
      pl.BlockSpec((seq_len, nh * RET_DV), lambda h, b: (row0 + b, vcol + h)),
      pl.BlockSpec((seq_len, nh * RET_DV), lambda h, b: (row0 + b, gcol + h)),
      pl.BlockSpec((nh, 2, 1, RET_DK), lambda h, b: (h, 0, 0, 0)),
      pl.BlockSpec((1, nh * RET_DV), lambda h, b: (0, h)),
  ]
  args = [p, p, p, p, dl, ng]
  y_spec = pl.BlockSpec((seq_len, nh * RET_DV), lambda h, b: (b, h))
  y_shape = jax.ShapeDtypeStruct((nb * seq_len, RET_HEADS * RET_DV), BF16)
  state_spec = pl.BlockSpec((None, 2, nh, RET_DK, RET_DV), lambda h, b: (b, 0, h, 0, 0))
  if prompt:
    out_specs = (y_spec, state_spec)
    out_shape = (y_shape, jax.ShapeDtypeStruct((NB_P, 2, RET_HEADS, RET_DK, RET_DV), F32))
  else:
    in_specs.append(state_spec)
    args.append(state0)
    out_specs = y_spec
    out_shape = y_shape
  return pl.pallas_call(
      functools.partial(_ret_kernel, seq_len=seq_len, has_s0=not prompt, nh=nh),
      grid=(hb, nb),
      in_specs=in_specs,
      out_specs=out_specs,
      out_shape=out_shape,
      scratch_shapes=[
          pltpu.VMEM((2, nh, seq_len // RET_CHUNK, RET_DK, RET_DV), F32),
          pltpu.VMEM((nh, 4, RET_CHUNK, RET_DV), F32),
          pltpu.VMEM((nh, 2, 1, RET_DV), F32),
      ],
      compiler_params=_params(("arbitrary", "arbitrary")),
      name="ret_prompt" if prompt else "ret_latent",
  )(*args)


def _router_kernel(x_ref, *route_refs):
  _route_tile(x_ref[...], *route_refs)


def _route_tile(x, g_ref, sh_ref, sc_ref, wrh_ref, wrl_ref, br_ref, triu_ref,
                info_ref, cnt_ref, carry_s):
  @pl.when(pl.program_id(0) == 0)
  def _():
    carry_s[...] = jnp.zeros_like(carry_s)

  h = _norm_mod(x, g_ref[...], sh_ref[...], sc_ref[...])
  logits = _router_logits(h, wrh_ref, wrl_ref, br_ref)
  lt = logits.T[0:ROUTER_ROWS]
  row = lax.broadcasted_iota(jnp.int32, lt.shape, 0)
  big = jnp.int32(ROUTER_ROWS)

  def first_max(v):
    m = jnp.max(v, axis=0, keepdims=True)
    return jnp.min(jnp.where(v == m, row, big), axis=0, keepdims=True)

  gidx = first_max(jnp.where(row < N_GROUPS, lt, NEG))
  lo = N_GROUPS + EPG * gidx
  le = jnp.where((row >= lo) & (row < lo + EPG), lt, NEG)
  i1 = first_max(le)
  i2 = first_max(jnp.where(row == i1, NEG, le))
  e_lo = jnp.minimum(i1, i2) - N_GROUPS
  e_hi = jnp.maximum(i1, i2) - N_GROUPS
  a = e_lo - EPG * gidx
  b = e_hi - EPG * gidx
  pair_base = jnp.where(a == 0, 0, jnp.where(a == 1, 3, 5))
  cls = N_PAIRS * gidx + pair_base + (b - a - 1)

  onehot = jnp.where(row == cls, 1.0, 0.0)
  before = jnp.dot(onehot.astype(BF16), triu_ref[...], preferred_element_type=F32) + carry_s[...]
  rank = jnp.sum(jnp.where(row == cls, before, 0.0), axis=0, keepdims=True)
  carry_s[...] = carry_s[...] + jnp.sum(onehot, axis=1, keepdims=True)
  cnt_ref[...] = carry_s[...]

  row8 = lax.broadcasted_iota(jnp.int32, info_ref.shape, 0)
  info_ref[...] = jnp.where(row8 == 0, cls.astype(F32), jnp.where(row8 == 1, rank, 0.0))


_ROUTE_IN_SPECS = [
    pl.BlockSpec((1, D), lambda i: (0, 0)),
    pl.BlockSpec((None, 1, D), lambda i: (_seg(i, TM), 0, 0)),
    pl.BlockSpec((None, 1, D), lambda i: (_seg(i, TM), 0, 0)),
    pl.BlockSpec((D, 128), lambda i: (0, 0)),
    pl.BlockSpec((D, 128), lambda i: (0, 0)),
    pl.BlockSpec((1, 128), lambda i: (0, 0)),
    pl.BlockSpec((TM, TM), lambda i: (0, 0)),
]
_ROUTE_OUT_SPECS = (
    pl.BlockSpec((None, 8, TM), lambda i: (i, 0, 0)),
    pl.BlockSpec((ROUTER_ROWS, TM), lambda i: (0, 0)),
)
_ROUTE_OUT_SHAPE = (
    jax.ShapeDtypeStruct((T // TM, 8, TM), F32),
    jax.ShapeDtypeStruct((ROUTER_ROWS, TM), F32),
)
_ROUTE_SCRATCH = [pltpu.VMEM((ROUTER_ROWS, TM), F32)]


def _router(x, route_args):
  return pl.pallas_call(
      _router_kernel,
      grid=(T // TM,),
      in_specs=[pl.BlockSpec((TM, D), lambda i: (i, 0))] + _ROUTE_IN_SPECS,
      out_specs=_ROUTE_OUT_SPECS,
      out_shape=_ROUTE_OUT_SHAPE,
      scratch_shapes=_ROUTE_SCRATCH,
      compiler_params=_params(("arbitrary",)),
      name="router",
  )(x, *route_args)


def _slot_of(slot_ref, token):
  return slot_ref[token]


def _tile_rows(ref, row):
  return ref.at[pl.ds(pl.multiple_of(row * ROW_SUB, ROW_SUB), ROW_SUB), :]


def _scatter_copy(slot_ref, step, r, stage, buf, hs_hbm, sem):
  return pltpu.make_async_copy(
      stage.at[buf, pl.ds(r * ROW_SUB, ROW_SUB), :],
      _tile_rows(hs_hbm, _slot_of(slot_ref, step * TS + r)), sem.at[buf])


def _scatter_kernel(slot_ref, pstart_ref, pn_ref, nv_ref,
                    x_ref, g_ref, sh_ref, sc_ref, hs_hbm,
                    stage, zero_s, sem, zsem):
  i = pl.program_id(0)
  n = pl.num_programs(0)
  buf = i % 2

  def pad_copy(c, r):
    return pltpu.make_async_copy(zero_s, _tile_rows(hs_hbm, pstart_ref[c] + r), zsem.at[0])

  def tail_copy(t):
    rows = TMM * ROW_SUB
    return pltpu.make_async_copy(
        stage.at[1], hs_hbm.at[pl.ds(pl.multiple_of(t * rows, rows), rows), :], zsem.at[1])

  @pl.when(i == 0)
  def _():
    stage[...] = jnp.zeros_like(stage)
    zero_s[...] = jnp.zeros_like(zero_s)
    for c in range(N_CLASSES):
      def start(r, carry, c=c):
        pad_copy(c, r).start()
        return carry
      lax.fori_loop(0, pn_ref[c], start, 0)

    def tail_start(t, carry):
      tail_copy(t).start()
      return carry
    lax.fori_loop(nv_ref[0], NT_E, tail_start, 0)
    for c in range(N_CLASSES):
      def wait(r, carry, c=c):
        pad_copy(c, r).wait()
        return carry
      lax.fori_loop(0, pn_ref[c], wait, 0)

    def tail_wait(t, carry):
      tail_copy(t).wait()
      return carry
    lax.fori_loop(nv_ref[0], NT_E, tail_wait, 0)

  h = _norm_mod(x_ref[...], g_ref[...], sh_ref[...], sc_ref[...])
  for s in range(ROW_SUB):
    stage[buf, pl.ds(s, TS, stride=ROW_SUB), :] = h[:, s * 128:(s + 1) * 128]

  def start(r, carry):
    _scatter_copy(slot_ref, i, r, stage, buf, hs_hbm, sem).start(priority=SCATTER_DMA_PRIORITY)
    return carry
  lax.fori_loop(0, TS, start, 0, unroll=8)

  @pl.when(i > 0)
  def _():
    def wait(r, carry):
      _scatter_copy(slot_ref, i - 1, r, stage, 1 - buf, hs_hbm, sem).wait()
      return carry
    lax.fori_loop(0, TS, wait, 0, unroll=8)

  @pl.when(i == n - 1)
  def _():
    def wait(r, carry):
      _scatter_copy(slot_ref, i, r, stage, buf, hs_hbm, sem).wait()
      return carry
    lax.fori_loop(0, TS, wait, 0, unroll=8)


def _scatter_rows(slot, pad_start, pad_n, n_valid, x, g, shift, scale):
  assert TS == TMM
  grid_spec = pltpu.PrefetchScalarGridSpec(
      num_scalar_prefetch=4,
      grid=(T // TS,),
      in_specs=[
          pl.BlockSpec((TS, D), lambda i, *_: (i, 0)),
          pl.BlockSpec((1, D), lambda i, *_: (0, 0)),
          pl.BlockSpec((None, 1, D), lambda i, *_: (_seg(i, TS), 0, 0)),
          pl.BlockSpec((None, 1, D), lambda i, *_: (_seg(i, TS), 0, 0)),
      ],
      out_specs=pl.BlockSpec(memory_space=pl.ANY),
      scratch_shapes=[
          pltpu.VMEM((2, TS * ROW_SUB, 128), F32),
          pltpu.VMEM((ROW_SUB, 128), F32),
          pltpu.SemaphoreType.DMA((2,)),
          pltpu.SemaphoreType.DMA((2,)),
      ],
  )
  return pl.pallas_call(
      _scatter_kernel,
      grid_spec=grid_spec,
      out_shape=jax.ShapeDtypeStruct((P_PAD * ROW_SUB, 128), F32),
      compiler_params=_params(("arbitrary",)),
      name="scatter_rows",
  )(slot, pad_start, pad_n, n_valid, x, g, shift, scale)


def _router_logits(h, wrh_ref, wrl_ref, br_ref):
  h_hi = h.astype(BF16)
  h_lo = (h - h_hi.astype(F32)).astype(BF16)
  dot = functools.partial(jnp.dot, preferred_element_type=F32)
  return (dot(h_hi, wrh_ref[...]) + dot(h_lo, wrh_ref[...]) + dot(h_hi, wrl_ref[...])
          + br_ref[...])


def _expert_kernel(elo_ref, ehi_ref, nv_ref, hs_ref, wrh_ref, br_ref,
                   wg_lo, wg_hi, wu_lo, wu_hi, wd_lo, wd_hi, y_ref):
  i = pl.program_id(0)

  @pl.when(i < nv_ref[0])
  def _():
    h = jnp.concatenate(
        [hs_ref[pl.ds(s, TMM, stride=ROW_SUB), :] for s in range(ROW_SUB)], axis=1)
    hb = h.astype(BF16)
    logits = jnp.dot(hb, wrh_ref[...], preferred_element_type=F32) + br_ref[...]
    lane = lax.broadcasted_iota(jnp.int32, logits.shape, 1)
    elo, ehi = elo_ref[i], ehi_ref[i]

    def pick(idx):
      return jnp.sum(jnp.where(lane == idx, logits, 0.0), axis=-1, keepdims=True)
    l_g, l_lo, l_hi = pick(elo // EPG), pick(N_GROUPS + elo), pick(N_GROUPS + ehi)
    p_top = 1.0 / jnp.sum(jnp.where(lane < N_GROUPS, jnp.exp(logits - l_g), 0.0),
                          axis=-1, keepdims=True)
    wl = jnp.broadcast_to(p_top / (1.0 + jnp.exp(l_hi - l_lo)), (TMM, D_EXPERT))
    wh = jnp.broadcast_to(p_top / (1.0 + jnp.exp(l_lo - l_hi)), (TMM, D_EXPERT))
    dot = functools.partial(jnp.dot, preferred_element_type=F32)
    a_lo = (_silu(dot(hb, wg_lo[...])) * dot(hb, wu_lo[...])) * wl
    a_hi = (_silu(dot(hb, wg_hi[...])) * dot(hb, wu_hi[...])) * wh
    a_lo, a_hi = a_lo.astype(BF16), a_hi.astype(BF16)
    nb = 2 * 128
    for c in range(0, D, nb):
      y = dot(a_lo, wd_lo[:, c:c + nb]) + dot(a_hi, wd_hi[:, c:c + nb])
      for s in range(c // 128, (c + nb) // 128):
        y_ref[pl.ds(s, TMM, stride=ROW_SUB), :] = y[:, s * 128 - c:(s + 1) * 128 - c]

  @pl.when(i >= nv_ref[0])
  def _():
    y_ref[...] = jnp.zeros_like(y_ref)


def _experts(layer, tile_elo, tile_ehi, n_valid, hs, wr_hi, br, w_gate, w_up, w_down):
  up_spec_lo = pl.BlockSpec((None, None, D, D_EXPERT), lambda i, elo, ehi, nv: (layer, elo[i], 0, 0))
  up_spec_hi = pl.BlockSpec((None, None, D, D_EXPERT), lambda i, elo, ehi, nv: (layer, ehi[i], 0, 0))
  dn_spec_lo = pl.BlockSpec((None, None, D_EXPERT, D), lambda i, elo, ehi, nv: (layer, elo[i], 0, 0))
  dn_spec_hi = pl.BlockSpec((None, None, D_EXPERT, D), lambda i, elo, ehi, nv: (layer, ehi[i], 0, 0))
  grid_spec = pltpu.PrefetchScalarGridSpec(
      num_scalar_prefetch=3,
      grid=(NT_E,),
      in_specs=[
          pl.BlockSpec((TMM * ROW_SUB, 128), lambda i, elo, ehi, nv: (i, 0)),
          pl.BlockSpec((D, 128), lambda i, elo, ehi, nv: (0, 0)),
          pl.BlockSpec((1, 128), lambda i, elo, ehi, nv: (0, 0)),
          up_spec_lo, up_spec_hi, up_spec_lo, up_spec_hi, dn_spec_lo, dn_spec_hi,
      ],
      out_specs=pl.BlockSpec((TMM * ROW_SUB, 128), lambda i, elo, ehi, nv: (i, 0)),
  )
  return pl.pallas_call(
      _expert_kernel,
      grid_spec=grid_spec,
      out_shape=jax.ShapeDtypeStruct((P_PAD * ROW_SUB, 128), F32),
      compiler_params=_params(("arbitrary",)),
      name="experts",
  )(tile_elo, tile_ehi, n_valid, hs, wr_hi, br,
    w_gate, w_gate, w_up, w_up, w_down, w_down)


def _gather_copy(slot_ref, step, r, y_hbm, ybuf, buf, sem):
  return pltpu.make_async_copy(
      _tile_rows(y_hbm, _slot_of(slot_ref, step * TC + r)),
      ybuf.at[buf, pl.ds(r * ROW_SUB, ROW_SUB), :], sem.at[buf])


def _combine_kernel(slot_ref, x_ref, gate_ref, nf_ref, y_hbm, *rest, final):
  if final:
    op_ref, os_ref, ybuf, sem = rest
  else:
    o_ref, ybuf, sem = rest
  i = pl.program_id(0)
  n = pl.num_programs(0)
  buf = i % 2

  def start(step, b):
    def body(r2, carry):
      for q in range(2):
        _gather_copy(slot_ref, step, 2 * r2 + q, y_hbm, ybuf, b, sem).start(priority=q)
      return carry
    lax.fori_loop(0, TC // 2, body, 0, unroll=4)

  @pl.when(i == 0)
  def _():
    start(0, 0)

  @pl.when(i + 1 < n)
  def _():
    start(i + 1, 1 - buf)

  def wait(r, carry):
    _gather_copy(slot_ref, i, r, y_hbm, ybuf, buf, sem).wait()
    return carry
  lax.fori_loop(0, TC, wait, 0, unroll=8)

  y = jnp.concatenate(
      [ybuf[buf, pl.ds(s, TC, stride=ROW_SUB), :] for s in range(ROW_SUB)], axis=1)
  x = x_ref[...] + gate_ref[...] * y
  if not final:
    o_ref[...] = x
    return
  x = (x * lax.rsqrt(jnp.mean(x * x, axis=-1, keepdims=True) + EPS)) * nf_ref[...]
  is_prompt = i < T_P // TC

  @pl.when(is_prompt)
  def _():
    op_ref[...] = x

  @pl.when(jnp.logical_not(is_prompt))
  def _():
    os_ref[...] = x


def _combine(slot, x, gate, norm_f, y_sorted, final):
  npt = T_P // TC
  if final:
    out_specs = (pl.BlockSpec((TC, D), lambda i, s: (jnp.minimum(i, npt - 1), 0)),
                 pl.BlockSpec((TC, D), lambda i, s: (jnp.maximum(i - npt, 0), 0)))
    out_shape = (jax.ShapeDtypeStruct((T_P, D), F32), jax.ShapeDtypeStruct((T_S, D), F32))
  else:
    out_specs = pl.BlockSpec((TC, D), lambda i, s: (i, 0))
    out_shape = jax.ShapeDtypeStruct((T, D), F32)
  grid_spec = pltpu.PrefetchScalarGridSpec(
      num_scalar_prefetch=1,
      grid=(T // TC,),
      in_specs=[
          pl.BlockSpec((TC, D), lambda i, s: (i, 0)),
          pl.BlockSpec((None, 1, D), lambda i, s: (_seg(i, TC), 0, 0)),
          pl.BlockSpec((1, D), lambda i, s: (0, 0)),
          pl.BlockSpec(memory_space=pl.ANY),
      ],
      out_specs=out_specs,
      scratch_shapes=[pltpu.VMEM((2, TC * ROW_SUB, 128), F32), pltpu.SemaphoreType.DMA((2,))],
  )
  return pl.pallas_call(
      functools.partial(_combine_kernel, final=final),
      grid_spec=grid_spec,
      out_shape=out_shape,
      compiler_params=_params(("arbitrary",)),
      name="combine_final" if final else "combine",
  )(slot, x, gate, norm_f, y_sorted)


def _class_experts():
  lo, hi = [], []
  for g in range(N_GROUPS):
    for a in range(EPG):
      for b in range(a + 1, EPG):
        lo.append(g * EPG + a)
        hi.append(g * EPG + b)
  return np.asarray(lo, np.int32), np.asarray(hi, np.int32)


def _moe_layer(layer, x, routing, route_args, gate, w_gate, w_up, w_down, norm_f):
  final = layer == DEPTH - 1
  g, shift, scale, wr_hi, _, br, _ = route_args
  info, cnt = routing

  cls = info[:, 0, :].astype(jnp.int32)
  rank = info[:, 1, :].astype(jnp.int32)
  counts = cnt[:N_CLASSES, 0].astype(jnp.int32)
  tiles = (counts + TMM - 1) // TMM
  tile_end = jnp.cumsum(tiles)
  offs = (tile_end - tiles) * TMM
  slot = rank
  for k in range(N_CLASSES):
    slot = slot + jnp.where(cls == k, offs[k], 0)
  slot = slot.reshape(T)
  n_valid = tile_end[-1]
  tile_ids = jnp.minimum(jnp.arange(NT_E, dtype=jnp.int32), n_valid - 1)
  tile_cls = jnp.sum((tile_ids[:, None] >= tile_end[None, :]).astype(jnp.int32), axis=1)
  tile_cls = jnp.minimum(tile_cls, N_CLASSES - 1)
  cls_lo, cls_hi = _class_experts()
  tile_elo = jnp.asarray(cls_lo)[tile_cls]
  tile_ehi = jnp.asarray(cls_hi)[tile_cls]

  n_valid = n_valid.reshape(1)
  hs = _scatter_rows(slot, offs + counts, tiles * TMM - counts, n_valid, x, g, shift, scale)
  y_sorted = _experts(layer, tile_elo, tile_ehi, n_valid, hs, wr_hi, br, w_gate, w_up, w_down)
  return _combine(slot, x, gate, norm_f, y_sorted, final)


def kernel(x_prompt, x_sample, cache_k, cache_v, state_ret, c, c_ctx, norm1, norm2, w_ada, b_ada, attn_w_qkv, attn_q_norm, attn_k_norm, attn_w_o, pool_w, pool_scale, ret_w_in, ret_decay_logit, ret_norm, ret_w_out, moe_w_router_g, moe_b_router_g, moe_w_router_e, moe_b_router_e, moe_w_gate, moe_w_up, moe_w_down, norm_f):
  x = (x_prompt.reshape(T_P, D), x_sample.reshape(T_S, D))
  cond8 = jnp.concatenate([c_ctx[None, :], c, jnp.zeros((N_SEG - 1 - NB_S, D), F32)], axis=0)
  mods = _adaln(cond8, w_ada, b_ada)
  mods = mods.reshape(DEPTH, N_SEG, 6, 1, D).transpose(0, 2, 1, 3, 4)
  rope = _rope_tables()
  tri = jnp.triu(jnp.ones((TM, TM), BF16), 1)
  pad_r = 128 - N_GROUPS - N_EXPERTS
  norm_f2 = norm_f.reshape(1, D)

  w_gate, w_up, w_down = moe_w_gate.astype(BF16), moe_w_up.astype(BF16), moe_w_down.astype(BF16)

  new_k, new_v, new_s = [], [], []
  for i in range(DEPTH):
    kind, j = i % 3, i // 3
    m = mods[i]
    g1 = norm1[i].reshape(1, D)
    wr = jnp.concatenate([moe_w_router_g[i], moe_w_router_e[i], jnp.zeros((D, pad_r), F32)], axis=1)
    br = jnp.concatenate([moe_b_router_g[i], moe_b_router_e[i], jnp.zeros((pad_r,), F32)]).reshape(1, 128)
    wr_hi = wr.astype(BF16)
    wr_lo = (wr - wr_hi.astype(F32)).astype(BF16)
    route_args = (norm2[i].reshape(1, D), m[3], m[4], wr_hi, wr_lo, br, tri)
    if kind == 0:
      qkv = _nm_matmul(x, g1, m[0], m[1], attn_w_qkv[j].astype(BF16), TM, "qkv_proj")
      qn = attn_q_norm[j].reshape(1, HEAD_DIM)
      kn = attn_k_norm[j].reshape(1, HEAD_DIM)
      o_p, kc, vc = _attn_prompt(qkv, qn, kn)
      ck = cache_k[:, j].reshape(NB_S, PAST, N_KV * HEAD_DIM)
      cv = cache_v[:, j].reshape(NB_S, PAST, N_KV * HEAD_DIM)
      o_s = _attn_latent(qkv, ck, cv, qn, kn, rope)
      new_k.append(kc.reshape(NB_P, SEQ_P, N_KV, HEAD_DIM))
      new_v.append(vc.reshape(NB_P, SEQ_P, N_KV, HEAD_DIM))
      x, *routing = _mm_res_route((o_p, o_s), attn_w_o[j].astype(BF16), x, m[2], route_args, "attn_out")
    elif kind == 1:
      x = _pool_layer(x, g1, m[0], m[1], m[2], pool_w[j].astype(BF16), pool_scale[j].reshape(1, D))
      routing = _router(x, route_args)
    else:
      p = _nm_matmul(x, g1, m[0], m[1], ret_w_in[j].astype(BF16), 256, "ret_proj")
      dl = jnp.broadcast_to(ret_decay_logit[j].T[:, :, None, None], (RET_HEADS, 2, 1, RET_DK))
      ng = ret_norm[j].reshape(1, RET_HEADS * RET_DV)
      y_p, s_new = _retention(p, dl, ng, None, True)
      y_s = _retention(p, dl, ng, state_ret[:, j], False)
      new_s.append(s_new)
      x, *routing = _mm_res_route((y_p, y_s), ret_w_out[j].astype(BF16), x, m[2], route_args, "ret_out")
    x = _moe_layer(i, x, routing, route_args, m[5], w_gate, w_up, w_down, norm_f2)

  y_prompt = x[0].reshape(NB_P, SEQ_P, D)
  y_sample = x[1].reshape(NB_S, SEQ_S, D)
  new_cache_k = jnp.stack(new_k, axis=1)
  new_cache_v = jnp.stack(new_v, axis=1)
  assert len(new_s) == 1
  new_state_ret = new_s[0].reshape(NB_P, 1, 2, RET_HEADS, RET_DK, RET_DV)
  return (y_prompt, y_sample, new_cache_k, new_cache_v, new_state_ret)
```

```python
import functools

import jax
import jax.numpy as jnp
import numpy as np
from jax import lax
from jax.experimental import pallas as pl
from jax.experimental.pallas import tpu as pltpu

F32 = jnp.float32
BF16 = jnp.bfloat16

D = 1024
NB_P, SEQ_P = 32, 256
NB_S, SEQ_S = 4, 2048
T_P = NB_P * SEQ_P
T_S = NB_S * SEQ_S
T = T_P + T_S
DEPTH = 4
GRID_W = 64
HEAD_DIM = 128
N_HEADS = 8
N_KV = 2
KV_GROUP = N_HEADS // N_KV
PAST = 256
ROPE_THETA = 10000.0
POOL_WINDOWS = (2, 4, 8, 16)
POOL_GROUP = D // 4
POOL_HALO = 8
RET_HEADS = 8
RET_DK = 128
RET_DV = 256
RET_CHUNK = 128
RET_HEADS_PER_STEP_PROMPT = 4
RET_HEADS_PER_STEP_LATENT = 2
N_GROUPS = 4
EPG = 4
N_EXPERTS = 16
D_EXPERT = 256
N_PAIRS = 6
N_CLASSES = N_GROUPS * N_PAIRS
ROUTER_ROWS = 32
EPS = 1e-6
LOG2E = 1.4426950408889634
NEG = -1e30
N_SEG = 8

VMEM_LIMIT_BYTES = 52 * 1024 * 1024

TM = 512
TQ = 512
ATTN_UNIT_Q = 128
ATTN_KEY_CHUNK = 256
ATTN_HEAD_STACK = 2
TP = 256
TMM = 256
TS = 512
TC = 256
ROW_SUB = 8
PACK_SUB = D // 2 // 128
INFO_SUB = PACK_SUB
P_PAD = T + N_CLASSES * TMM
NT_E = P_PAD // TMM


def _params(sem):
  return pltpu.CompilerParams(dimension_semantics=sem, vmem_limit_bytes=VMEM_LIMIT_BYTES)


def _seg(i, tm):
  npt = T_P // tm
  return jnp.where(i < npt, 0, (i - npt) // (SEQ_S // tm) + 1)


def _norm_mod(x, g, shift, scale):
  r = lax.rsqrt(jnp.mean(x * x, axis=-1, keepdims=True) + EPS)
  return ((x * r) * g) * (1.0 + scale) + shift


def _silu(x):
  return x * (1.0 / (1.0 + jnp.exp(-x)))


def _adaln_kernel(c_ref, w_ref, b_ref, o_ref):
  s = _silu(c_ref[...]).astype(BF16)
  o_ref[...] = jnp.dot(s, w_ref[...].astype(BF16), preferred_element_type=F32) + b_ref[...]


def _adaln(cond8, w_ada, b_ada):
  tn = 1536
  return pl.pallas_call(
      _adaln_kernel,
      grid=(DEPTH, 6 * D // tn),
      in_specs=[
          pl.BlockSpec((N_SEG, D), lambda l, j: (0, 0)),
          pl.BlockSpec((None, D, tn), lambda l, j: (l, 0, j)),
          pl.BlockSpec((None, 1, tn), lambda l, j: (l, 0, j)),
      ],
      out_specs=pl.BlockSpec((None, N_SEG, tn), lambda l, j: (l, 0, j)),
      out_shape=jax.ShapeDtypeStruct((DEPTH, N_SEG, 6 * D), F32),
      compiler_params=_params(("arbitrary", "arbitrary")),
      name="adaln",
  )(cond8, w_ada, b_ada.reshape(DEPTH, 1, 6 * D))


def _token_rows(x, tm, width):
  npt = T_P // tm
  if isinstance(x, tuple):
    arrays = x
    latent_map = lambda i: (jnp.maximum(i - npt, 0), 0)
  else:
    arrays = (x, x)
    latent_map = lambda i: (jnp.maximum(i, npt), 0)
  specs = [pl.BlockSpec((tm, width), lambda i: (jnp.minimum(i, npt - 1), 0)),
           pl.BlockSpec((tm, width), latent_map)]
  return arrays, specs


def _pick_rows(p_ref, s_ref, tm):
  return jnp.where(pl.program_id(0) < T_P // tm, p_ref[...], s_ref[...])


def _nm_matmul_kernel(xp_ref, xs_ref, g_ref, sh_ref, sc_ref, w_ref, o_ref, *, tm, n_chunk):
  x = _pick_rows(xp_ref, xs_ref, tm)
  h = _norm_mod(x, g_ref[...], sh_ref[...], sc_ref[...]).astype(BF16)
  n = w_ref.shape[1]
  for c in range(0, n, n_chunk):
    o_ref[:, c:c + n_chunk] = jnp.dot(
        h, w_ref[:, c:c + n_chunk], preferred_element_type=F32).astype(o_ref.dtype)


def _nm_matmul(x, g, shift, scale, w, tm, name):
  n = w.shape[1]
  x_arrays, x_specs = _token_rows(x, tm, D)
  return pl.pallas_call(
      functools.partial(_nm_matmul_kernel, tm=tm, n_chunk=512),
      grid=(T // tm,),
      in_specs=x_specs + [
          pl.BlockSpec((1, D), lambda i: (0, 0)),
          pl.BlockSpec((None, 1, D), lambda i: (_seg(i, tm), 0, 0)),
          pl.BlockSpec((None, 1, D), lambda i: (_seg(i, tm), 0, 0)),
          pl.BlockSpec((D, n), lambda i: (0, 0)),
      ],
      out_specs=pl.BlockSpec((tm, n), lambda i: (i, 0)),
      out_shape=jax.ShapeDtypeStruct((T, n), BF16),
      compiler_params=_params(("arbitrary",)),
      name=name,
  )(*x_arrays, g, shift, scale, w)


def _mm_res_route_kernel(ap_ref, as_ref, xp_ref, xs_ref, w_ref, gate_ref, *rest):
  n_route_in = len(_ROUTE_IN_SPECS)
  route_in, (o_ref, info_ref, cnt_ref, carry_s) = rest[:n_route_in], rest[n_route_in:]
  a = _pick_rows(ap_ref, as_ref, TM)
  x = _pick_rows(xp_ref, xs_ref, TM)
  x_new = x + gate_ref[...] * jnp.dot(a, w_ref[...], preferred_element_type=F32)
  o_ref[...] = x_new
  _route_tile(x_new, *route_in, info_ref, cnt_ref, carry_s)


def _mm_res_route(a, w, x, gate, route_args, name):
  k = w.shape[0]
  a_arrays, a_specs = _token_rows(a, TM, k)
  x_arrays, x_specs = _token_rows(x, TM, D)
  return pl.pallas_call(
      _mm_res_route_kernel,
      grid=(T // TM,),
      in_specs=a_specs + x_specs + [
          pl.BlockSpec((k, D), lambda i: (0, 0)),
          pl.BlockSpec((None, 1, D), lambda i: (_seg(i, TM), 0, 0)),
      ] + _ROUTE_IN_SPECS,
      out_specs=(pl.BlockSpec((TM, D), lambda i: (i, 0)),) + _ROUTE_OUT_SPECS,
      out_shape=(jax.ShapeDtypeStruct((T, D), F32),) + _ROUTE_OUT_SHAPE,
      scratch_shapes=_ROUTE_SCRATCH,
      compiler_params=_params(("arbitrary",)),
      name=name,
  )(*a_arrays, *x_arrays, w, gate, *route_args)


def _rope(x, c, a, b):
  return x * c + pltpu.roll(x, 96, 1) * a + pltpu.roll(x, 32, 1) * b


def _head_norm(x, w):
  return (x * lax.rsqrt(jnp.mean(x * x, axis=-1, keepdims=True) + EPS)) * w


def _attn_latent_kernel(q_ref, kn_ref, vn_ref, ck_ref, cv_ref, qw_ref, kw_ref,
                        cq_ref, aq_ref, bq_ref, ckk_ref, akk_ref, bkk_ref,
                        o_ref, k_s, vt_s, *st_refs, tq):
  @pl.when(pl.program_id(2) == 0)
  def _():
    k = _rope(_head_norm(kn_ref[...].astype(F32), kw_ref[...]), ckk_ref[...], akk_ref[...], bkk_ref[...])
    k_s[0:PAST, :] = ck_ref[...].astype(BF16)
    k_s[PAST:, :] = k.astype(BF16)
    vt_s[:, 0:PAST] = cv_ref[...].T.astype(BF16)
    vt_s[:, PAST:] = vn_ref[...].astype(F32).T.astype(BF16)

  qa = q_ref[...].astype(F32)
  lk = k_s.shape[0]

  def prep_q(h):
    qh = _head_norm(qa[:, h * HEAD_DIM:(h + 1) * HEAD_DIM], qw_ref[...])
    qh = _rope(qh, cq_ref[...], aq_ref[...], bq_ref[...])
    return (qh * (HEAD_DIM ** -0.5 * LOG2E)).astype(BF16)

  nq = ATTN_HEAD_STACK * ATTN_UNIT_Q
  chunks = [(c0, min(c0 + ATTN_KEY_CHUNK, lk)) for c0 in range(0, lk, ATTN_KEY_CHUNK)]
  qh_all = [prep_q(h) for h in range(KV_GROUP)]
  units = [(h0, r0) for r0 in range(0, tq, ATTN_UNIT_Q) for h0 in range(0, KV_GROUP, ATTN_HEAD_STACK)]

  def score_chunk(u, c0, c1, m):
    h0, r0 = units[u]
    qp = jnp.concatenate([qh_all[h0 + d][r0:r0 + ATTN_UNIT_Q] for d in range(ATTN_HEAD_STACK)], axis=0)
    st = lax.dot_general(k_s[c0:c1, :], qp, (((1,), (1,)), ((), ())),
                         preferred_element_type=F32)
    st_refs[u % 2][c0:c1, :] = st
    return jnp.maximum(m, jnp.max(st, axis=0, keepdims=True))

  def value_chunk(u, c0, c1, m, l, acc):
    pt = jnp.exp2(st_refs[u % 2][c0:c1, :] - m)
    return (l + jnp.sum(pt, axis=0, keepdims=True),
            acc + jnp.dot(vt_s[:, c0:c1], pt.astype(BF16), preferred_element_type=F32))

  def finish(u, l, acc):
    h0, r0 = units[u]
    o = acc / l
    for d in range(ATTN_HEAD_STACK):
      h = h0 + d
      o_ref[r0:r0 + ATTN_UNIT_Q, h * HEAD_DIM:(h + 1) * HEAD_DIM] = (
          o[:, d * ATTN_UNIT_Q:(d + 1) * ATTN_UNIT_Q].T.astype(BF16))

  m_prev = None
  for u in range(len(units) + 1):
    m = jnp.full((1, nq), NEG, F32)
    l = jnp.zeros((1, nq), F32)
    acc = jnp.zeros((HEAD_DIM, nq), F32)
    for c0, c1 in chunks:
      if u < len(units):
        m = score_chunk(u, c0, c1, m)
      if u > 0:
        l, acc = value_chunk(u - 1, c0, c1, m_prev, l, acc)
    if u > 0:
      finish(u - 1, l, acc)
    m_prev = m


def _attn_scratch(lk):
  return ([pltpu.VMEM((lk, HEAD_DIM), BF16), pltpu.VMEM((HEAD_DIM, lk), BF16)]
          + [pltpu.VMEM((lk, ATTN_HEAD_STACK * ATTN_UNIT_Q), F32) for _ in range(2)])


def _rope_tables():
  rows = SEQ_S // GRID_W
  t_row = jnp.broadcast_to(jnp.arange(rows)[:, None], (rows, GRID_W)).reshape(-1)
  t_col = jnp.broadcast_to(jnp.arange(GRID_W)[None, :], (rows, GRID_W)).reshape(-1)
  nf = HEAD_DIM // 4
  inv = ROPE_THETA ** (-jnp.arange(nf, dtype=F32) / nf)
  ang_r = t_row.astype(F32)[:, None] * inv[None, :]
  ang_c = t_col.astype(F32)[:, None] * inv[None, :]
  cr, sr, cc, sc = jnp.cos(ang_r), jnp.sin(ang_r), jnp.cos(ang_c), jnp.sin(ang_c)
  z = jnp.zeros_like(sr)
  c = jnp.concatenate([cr, cr, cc, cc], axis=1)
  a = jnp.concatenate([-sr, z, -sc, z], axis=1)
  b = jnp.concatenate([z, sr, z, sc], axis=1)
  return c, a, b


def _attn_prompt_kernel(qkv_ref, qw_ref, kw_ref, o_ref, ko_ref, vo_ref):
  nq = N_HEADS * HEAD_DIM
  nk = N_KV * HEAD_DIM
  for kh in range(N_KV):
    kcols = slice(nq + kh * HEAD_DIM, nq + (kh + 1) * HEAD_DIM)
    vcols = slice(nq + nk + kh * HEAD_DIM, nq + nk + (kh + 1) * HEAD_DIM)
    k = _head_norm(qkv_ref[:, kcols].astype(F32), kw_ref[...])
    v = qkv_ref[:, vcols].astype(F32)
    ko_ref[pl.ds(kh, SEQ_P, stride=N_KV), :] = k
    vo_ref[pl.ds(kh, SEQ_P, stride=N_KV), :] = v
    kb = k.astype(BF16)
    vt = v.T.astype(BF16)
    for h in range(kh * KV_GROUP, (kh + 1) * KV_GROUP):
      hcols = slice(h * HEAD_DIM, (h + 1) * HEAD_DIM)
      qh = _head_norm(qkv_ref[:, hcols].astype(F32), qw_ref[...])
      qh = (qh * (HEAD_DIM ** -0.5 * LOG2E)).astype(BF16)
      st = lax.dot_general(kb, qh, (((1,), (1,)), ((), ())), preferred_element_type=F32)
      pt = jnp.exp2(st - jnp.max(st, axis=0, keepdims=True))
      l = jnp.sum(pt, axis=0, keepdims=True)
      acc = jnp.dot(vt, pt.astype(BF16), preferred_element_type=F32)
      o_ref[:, hcols] = (acc / l).T.astype(BF16)


def _attn_prompt(qkv, q_norm, k_norm):
  width = (N_HEADS + 2 * N_KV) * HEAD_DIM
  out_shapes = (
      jax.ShapeDtypeStruct((T_P, N_HEADS * HEAD_DIM), BF16),
      jax.ShapeDtypeStruct((NB_P, SEQ_P * N_KV, HEAD_DIM), F32),
      jax.ShapeDtypeStruct((NB_P, SEQ_P * N_KV, HEAD_DIM), F32),
  )
  return pl.pallas_call(
      _attn_prompt_kernel,
      grid=(NB_P,),
      in_specs=[
          pl.BlockSpec((SEQ_P, width), lambda b: (b, 0)),
          pl.BlockSpec((1, HEAD_DIM), lambda b: (0, 0)),
          pl.BlockSpec((1, HEAD_DIM), lambda b: (0, 0)),
      ],
      out_specs=(
          pl.BlockSpec((SEQ_P, N_HEADS * HEAD_DIM), lambda b: (b, 0)),
          pl.BlockSpec((None, SEQ_P * N_KV, HEAD_DIM), lambda b: (b, 0, 0)),
          pl.BlockSpec((None, SEQ_P * N_KV, HEAD_DIM), lambda b: (b, 0, 0)),
      ),
      out_shape=out_shapes,
      compiler_params=_params(("arbitrary",)),
      name="attn_prompt",
  )(qkv, q_norm, k_norm)


def _attn_latent(qkv, cache_k, cache_v, q_norm, k_norm, rope):
  kcol = N_HEADS
  vcol = kcol + N_KV
  nq = SEQ_S // TQ
  row0 = T_P // TQ
  seq0 = T_P // SEQ_S
  lk = PAST + SEQ_S
  c, a, b = rope
  tab_q = pl.BlockSpec((TQ, HEAD_DIM), lambda bb, h, i: (i, 0))
  tab_k = pl.BlockSpec((SEQ_S, HEAD_DIM), lambda bb, h, i: (0, 0))
  return pl.pallas_call(
      functools.partial(_attn_latent_kernel, tq=TQ),
      grid=(NB_S, N_KV, nq),
      in_specs=[
          pl.BlockSpec((TQ, KV_GROUP * HEAD_DIM), lambda bb, h, i: (row0 + bb * nq + i, h)),
          pl.BlockSpec((SEQ_S, HEAD_DIM), lambda bb, h, i: (seq0 + bb, kcol + h)),
          pl.BlockSpec((SEQ_S, HEAD_DIM), lambda bb, h, i: (seq0 + bb, vcol + h)),
          pl.BlockSpec((None, PAST, HEAD_DIM), lambda bb, h, i: (bb, 0, h)),
          pl.BlockSpec((None, PAST, HEAD_DIM), lambda bb, h, i: (bb, 0, h)),
          pl.BlockSpec((1, HEAD_DIM), lambda bb, h, i: (0, 0)),
          pl.BlockSpec((1, HEAD_DIM), lambda bb, h, i: (0, 0)),
          tab_q, tab_q, tab_q, tab_k, tab_k, tab_k,
      ],
      out_specs=pl.BlockSpec((TQ, KV_GROUP * HEAD_DIM), lambda bb, h, i: (bb * nq + i, h)),
      out_shape=jax.ShapeDtypeStruct((T_S, N_HEADS * HEAD_DIM), BF16),
      scratch_shapes=_attn_scratch(lk),
      compiler_params=_params(("arbitrary", "arbitrary", "arbitrary")),
      name="attn_latent",
  )(qkv, qkv, qkv, cache_k, cache_v, q_norm, k_norm, c, a, b, c, a, b)


def _pool_kernel(x_ref, xp_ref, xn_ref, g_ref, sh_ref, sc_ref, gate_ref, w_ref, ps_ref, o_ref):
  t = pl.program_id(0)
  npt = T_P // TP
  tiles_s = SEQ_S // TP
  is_p = t < npt
  pos = jnp.where(is_p, 0, (t - npt) % tiles_s)
  ntile = jnp.where(is_p, SEQ_P // TP, tiles_s)
  seq_len = ntile * TP
  keep_prev = jnp.where(pos == 0, 0.0, 1.0)
  keep_next = jnp.where(pos == ntile - 1, 0.0, 1.0)

  g, sh, sc = g_ref[...], sh_ref[...], sc_ref[...]
  x = x_ref[...]
  h = _norm_mod(x, g, sh, sc)
  hp = _norm_mod(xp_ref[...], g, sh, sc) * keep_prev
  hn = _norm_mod(xn_ref[...], g, sh, sc) * keep_next
  ext = jnp.concatenate([hp, h, hn], axis=0)
  n_ext = TP + 2 * POOL_HALO
  tseq = pos * TP + lax.broadcasted_iota(jnp.int32, (TP, 128), 0)

  outs = []
  for gi, win in enumerate(POOL_WINDOWS):
    lo, hi = gi * POOL_GROUP, (gi + 1) * POOL_GROUP
    acc = ext[:, lo:hi]
    span = 1
    while span < win:
      acc = acc + pltpu.roll(acc, n_ext - span, 0)
      span *= 2
    start = POOL_HALO - win // 2
    if start:
      acc = pltpu.roll(acc, n_ext - start, 0)
    ssum = acc[0:TP]
    cnt = (jnp.minimum(tseq + win // 2, seq_len) - jnp.maximum(tseq - win // 2, 0)).astype(F32)
    cnt = jnp.concatenate([cnt] * (POOL_GROUP // 128), axis=1)
    dlt = (ssum / cnt - h[:, lo:hi]).astype(BF16)
    outs.append(jnp.dot(dlt, w_ref[gi], preferred_element_type=F32))
  y = jnp.concatenate(outs, axis=1) * ps_ref[...]
  o_ref[...] = x + gate_ref[...] * y


def _pool_layer(x, g, shift, scale, gate, w, pscale):
  hb = TP // POOL_HALO
  last = T // POOL_HALO - 1
  seg = lambda i: _seg(i, TP)
  return pl.pallas_call(
      _pool_kernel,
      grid=(T // TP,),
      in_specs=[
          pl.BlockSpec((TP, D), lambda i: (i, 0)),
          pl.BlockSpec((POOL_HALO, D), lambda i: (jnp.maximum(i * hb - 1, 0), 0)),
          pl.BlockSpec((POOL_HALO, D), lambda i: (jnp.minimum((i + 1) * hb, last), 0)),
          pl.BlockSpec((1, D), lambda i: (0, 0)),
          pl.BlockSpec((None, 1, D), lambda i: (seg(i), 0, 0)),
          pl.BlockSpec((None, 1, D), lambda i: (seg(i), 0, 0)),
          pl.BlockSpec((None, 1, D), lambda i: (seg(i), 0, 0)),
          pl.BlockSpec((4, POOL_GROUP, POOL_GROUP), lambda i: (0, 0, 0)),
          pl.BlockSpec((1, D), lambda i: (0, 0)),
      ],
      out_specs=pl.BlockSpec((TP, D), lambda i: (i, 0)),
      out_shape=jax.ShapeDtypeStruct((T, D), F32),
      compiler_params=_params(("arbitrary",)),
      name="pool",
  )(x, x, x, g, shift, scale, gate, w, pscale)


def _ret_kernel(*refs, seq_len, has_s0, nh):
  it = iter(refs)
  q_ref, k_ref, v_ref, gate_ref, dl_ref, ng_ref = [next(it) for _ in range(6)]
  s0_ref = next(it) if has_s0 else None
  y_ref = next(it)
  so_ref = None if has_s0 else next(it)
  u_s = next(it)
  dec_s, cdec_s = next(it), next(it)
  c = RET_CHUNK
  n = seq_len // c
  kscale = RET_DK ** -0.5
  nt = (((1,), (1,)), ((), ()))
  tn = (((0,), (0,)), ((), ()))

  @pl.when(pl.program_id(1) == 0)
  def _():
    ri = lax.broadcasted_iota(jnp.int32, (c, c), 0).astype(F32)
    ci = lax.broadcasted_iota(jnp.int32, (c, c), 1).astype(F32)
    ri2 = lax.broadcasted_iota(jnp.int32, (c, RET_DV), 0).astype(F32)
    for hh in range(nh):
      lg = -jnp.log1p(jnp.exp(-dl_ref[hh]))
      lgf, lgb = lg[0], lg[1]
      lgf2 = jnp.concatenate([lgf, lgf], axis=1)
      lgb2 = jnp.concatenate([lgb, lgb], axis=1)
      dec_s[hh, 0] = jnp.concatenate(
          [jnp.exp((c - 1.0 - ri) * lgf), jnp.exp(ri * lgb)], axis=1) * kscale
      intra = (jnp.where(ri >= ci, jnp.exp((ri - ci) * lgf), 0.0)
               + jnp.where(ri <= ci, jnp.exp((ci - ri) * lgb), 0.0)) * kscale
      dec_s[hh, 1] = jnp.concatenate([intra, intra], axis=1)
      dec_s[hh, 2] = jnp.exp((ri2 + 1.0) * lgf2)
      dec_s[hh, 3] = jnp.exp((c - ri2) * lgb2)
      cdec_s[hh, 0] = jnp.exp(c * lgf2)
      cdec_s[hh, 1] = jnp.exp(c * lgb2)

  for hh in range(nh):
    qcols = slice(hh * RET_DK, (hh + 1) * RET_DK)
    vcols = slice(hh * RET_DV, (hh + 1) * RET_DV)
    kd2 = dec_s[hh, 0]

    for j in range(n):
      rows = slice(j * c, (j + 1) * c)
      kc = k_ref[rows, qcols].astype(F32)
      k2 = (jnp.concatenate([kc, kc], axis=1) * kd2).astype(BF16)
      u = lax.dot_general(k2, v_ref[rows, vcols], tn, preferred_element_type=F32)
      u_s[0, hh, j] = u[0:RET_DK]
      u_s[1, hh, j] = u[RET_DK:]

    for d, order in ((0, range(n)), (1, reversed(range(n)))):
      cdec = cdec_s[hh, d]
      st = s0_ref[d, hh] if has_s0 else jnp.zeros((RET_DK, RET_DV), F32)
      for j in order:
        u = u_s[d, hh, j]
        u_s[d, hh, j] = st
        st = st * cdec + u
      if not has_s0:
        so_ref[d, hh] = st

    intra = dec_s[hh, 1][:, :c]
    qd_f = dec_s[hh, 2]
    qd_b = dec_s[hh, 3]
    for j in range(n):
      rows = slice(j * c, (j + 1) * c)
      qc = q_ref[rows, qcols]
      vc = v_ref[rows, vcols]
      s = lax.dot_general(qc, k_ref[rows, qcols], nt, preferred_element_type=F32) * intra
      st2 = jnp.concatenate([u_s[0, hh, j], u_s[1, hh, j]], axis=1).astype(BF16)
      inter = jnp.dot(qc, st2, preferred_element_type=F32)
      o = (jnp.dot(s.astype(BF16), vc, preferred_element_type=F32)
           + inter[:, :RET_DV] * qd_f + inter[:, RET_DV:] * qd_b)
      mu = jnp.mean(o, axis=-1, keepdims=True)
      dv = o - mu
      var = jnp.mean(dv * dv, axis=-1, keepdims=True)
      on = (dv * lax.rsqrt(var + EPS)) * ng_ref[:, vcols]
      y_ref[rows, vcols] = (on * _silu(gate_ref[rows, vcols].astype(F32))).astype(BF16)


def _retention(p, dl, ng, state0, prompt):
  seq_len = SEQ_P if prompt else SEQ_S
  nb = NB_P if prompt else NB_S
  nh = RET_HEADS_PER_STEP_PROMPT if prompt else RET_HEADS_PER_STEP_LATENT
  row0 = 0 if prompt else T_P // SEQ_S
  hb = RET_HEADS // nh
  kcol = hb
  vcol = (2 * RET_HEADS * RET_DK) // (nh * RET_DV)
  gcol = vcol + hb
  in_specs = [
      pl.BlockSpec((seq_len, nh * RET_DK), lambda h, b: (row0 + b, h)),
      pl.BlockSpec((seq_len, nh * RET_DK), lambda h, b: (row0 + b, kcol + h)),
      pl.BlockSpec((seq_len, nh * RET_DV), lambda h, b: (row0 + b, vcol + h)),
      pl.BlockSpec((seq_len, nh * RET_DV), lambda h, b: (row0 + b, gcol + h)),
      pl.BlockSpec((nh, 2, 1, RET_DK), lambda h, b: (h, 0, 0, 0)),
      pl.BlockSpec((1, nh * RET_DV), lambda h, b: (0, h)),
  ]
  args = [p, p, p, p, dl, ng]
  y_spec = pl.BlockSpec((seq_len, nh * RET_DV), lambda h, b: (b, h))
  y_shape = jax.ShapeDtypeStruct((nb * seq_len, RET_HEADS * RET_DV), BF16)
  state_spec = pl.BlockSpec((None, 2, nh, RET_DK, RET_DV), lambda h, b: (b, 0, h, 0, 0))
  if prompt:
    out_specs = (y_spec, state_spec)
    out_shape = (y_shape, jax.ShapeDtypeStruct((NB_P, 2, RET_HEADS, RET_DK, RET_DV), F32))
  else:
    in_specs.append(state_spec)
    args.append(state0)
    out_specs = y_spec
    out_shape = y_shape
  return pl.pallas_call(
      functools.partial(_ret_kernel, seq_len=seq_len, has_s0=not prompt, nh=nh),
      grid=(hb, nb),
      in_specs=in_specs,
      out_specs=out_specs,
      out_shape=out_shape,
      scratch_shapes=[
          pltpu.VMEM((2, nh, seq_len // RET_CHUNK, RET_DK, RET_DV), F32),
          pltpu.VMEM((nh, 4, RET_CHUNK, RET_DV), F32),
          pltpu.VMEM((nh, 2, 1, RET_DV), F32),
      ],
      compiler_params=_params(("arbitrary", "arbitrary")),
      name="ret_prompt" if prompt else "ret_latent",
  )(*args)


def _router_kernel(x_ref, *route_refs):
  _route_tile(x_ref[...], *route_refs)


def _route_tile(x, g_ref, sh_ref, sc_ref, wrh_ref, wrl_ref, br_ref, triu_ref,
                info_ref, cnt_ref, carry_s):
  @pl.when(pl.program_id(0) == 0)
  def _():
    carry_s[...] = jnp.zeros_like(carry_s)

  h = _norm_mod(x, g_ref[...], sh_ref[...], sc_ref[...])
  logits = _router_logits(h, wrh_ref, wrl_ref, br_ref)
  lt = logits.T[0:ROUTER_ROWS]
  row = lax.broadcasted_iota(jnp.int32, lt.shape, 0)
  big = jnp.int32(ROUTER_ROWS)

  def first_max(v):
    m = jnp.max(v, axis=0, keepdims=True)
    return jnp.min(jnp.where(v == m, row, big), axis=0, keepdims=True)

  gidx = first_max(jnp.where(row < N_GROUPS, lt, NEG))
  lo = N_GROUPS + EPG * gidx
  le = jnp.where((row >= lo) & (row < lo + EPG), lt, NEG)
  i1 = first_max(le)
  i2 = first_max(jnp.where(row == i1, NEG, le))
  e_lo = jnp.minimum(i1, i2) - N_GROUPS
  e_hi = jnp.maximum(i1, i2) - N_GROUPS
  a = e_lo - EPG * gidx
  b = e_hi - EPG * gidx
  pair_base = jnp.where(a == 0, 0, jnp.where(a == 1, 3, 5))
  cls = N_PAIRS * gidx + pair_base + (b - a - 1)

  onehot = jnp.where(row == cls, 1.0, 0.0)
  before = jnp.dot(onehot.astype(BF16), triu_ref[...], preferred_element_type=F32) + carry_s[...]
  rank = jnp.sum(jnp.where(row == cls, before, 0.0), axis=0, keepdims=True)
  carry_s[...] = carry_s[...] + jnp.sum(onehot, axis=1, keepdims=True)
  cnt_ref[...] = carry_s[...]

  row8 = lax.broadcasted_iota(jnp.int32, info_ref.shape, 0)
  info_ref[...] = jnp.where(row8 == 0, cls.astype(F32), jnp.where(row8 == 1, rank, 0.0))


_ROUTE_IN_SPECS = [
    pl.BlockSpec((1, D), lambda i: (0, 0)),
    pl.BlockSpec((None, 1, D), lambda i: (_seg(i, TM), 0, 0)),
    pl.BlockSpec((None, 1, D), lambda i: (_seg(i, TM), 0, 0)),
    pl.BlockSpec((D, 128), lambda i: (0, 0)),
    pl.BlockSpec((D, 128), lambda i: (0, 0)),
    pl.BlockSpec((1, 128), lambda i: (0, 0)),
    pl.BlockSpec((TM, TM), lambda i: (0, 0)),
]
_ROUTE_OUT_SPECS = (
    pl.BlockSpec((None, 8, TM), lambda i: (i, 0, 0)),
    pl.BlockSpec((ROUTER_ROWS, TM), lambda i: (0, 0)),
)
_ROUTE_OUT_SHAPE = (
    jax.ShapeDtypeStruct((T // TM, 8, TM), F32),
    jax.ShapeDtypeStruct((ROUTER_ROWS, TM), F32),
)
_ROUTE_SCRATCH = [pltpu.VMEM((ROUTER_ROWS, TM), F32)]


def _router(x, route_args):
  return pl.pallas_call(
      _router_kernel,
      grid=(T // TM,),
      in_specs=[pl.BlockSpec((TM, D), lambda i: (i, 0))] + _ROUTE_IN_SPECS,
      out_specs=_ROUTE_OUT_SPECS,
      out_shape=_ROUTE_OUT_SHAPE,
      scratch_shapes=_ROUTE_SCRATCH,
      compiler_params=_params(("arbitrary",)),
      name="router",
  )(x, *route_args)


def _slot_of(slot_ref, token):
  return slot_ref[token]


def _tile_rows(ref, row):
  return ref.at[pl.ds(pl.multiple_of(row * ROW_SUB, ROW_SUB), ROW_SUB), :]


RUN_SIZES = tuple(1 << b for b in reversed(range(TS.bit_length())))


def _run_copies(hist_ref, lstart_ref, dst_ref, step, stage, buf, hs_hbm, sem, apply):
  for c in range(N_CLASSES):
    k = step * N_CLASSES + c
    n_rows, src, dst = hist_ref[k], lstart_ref[k], dst_ref[k]
    for size in RUN_SIZES:
      take = n_rows & size

      @pl.when(take != 0)
      def _(src=src, dst=dst, size=size):
        apply(pltpu.make_async_copy(
            stage.at[buf, pl.ds(pl.multiple_of(src * ROW_SUB, ROW_SUB), size * ROW_SUB), :],
            hs_hbm.at[pl.ds(pl.multiple_of(dst * ROW_SUB, ROW_SUB), size * ROW_SUB), :],
            sem.at[buf]))
      src, dst = src + take, dst + take


def _scatter_kernel(hist_ref, lstart_ref, dst_ref, pstart_ref, pn_ref, nv_ref,
                    x_ref, g_ref, sh_ref, sc_ref, pos_ref, hs_hbm,
                    stage, zero_s, sem, zsem):
  i = pl.program_id(0)
  n = pl.num_programs(0)
  buf = i % 2

  def pad_copy(c, r):
    return pltpu.make_async_copy(zero_s, _tile_rows(hs_hbm, pstart_ref[c] + r), zsem.at[0])

  def tail_copy(t):
    rows = TMM * ROW_SUB
    return pltpu.make_async_copy(
        stage.at[1, pl.ds(0, rows), :],
        hs_hbm.at[pl.ds(pl.multiple_of(t * rows, rows), rows), :], zsem.at[1])

  @pl.when(i == 0)
  def _():
    stage[...] = jnp.zeros_like(stage)
    zero_s[...] = jnp.zeros_like(zero_s)
    for c in range(N_CLASSES):
      def start(r, carry, c=c):
        pad_copy(c, r).start()
        return carry
      lax.fori_loop(0, pn_ref[c], start, 0)

    def tail_start(t, carry):
      tail_copy(t).start()
      return carry
    lax.fori_loop(nv_ref[0], NT_E, tail_start, 0)
    for c in range(N_CLASSES):
      def wait(r, carry, c=c):
        pad_copy(c, r).wait()
        return carry
      lax.fori_loop(0, pn_ref[c], wait, 0)

    def tail_wait(t, carry):
      tail_copy(t).wait()
      return carry
    lax.fori_loop(nv_ref[0], NT_E, tail_wait, 0)

  h = _norm_mod(x_ref[...], g_ref[...], sh_ref[...], sc_ref[...])
  pos = pos_ref[0:1, :]
  row = lax.broadcasted_iota(jnp.int32, (TS, TS), 0).astype(F32)
  perm = jnp.where(row == pos, 1.0, 0.0).astype(BF16)
  hsort = jnp.dot(perm, h.astype(BF16), preferred_element_type=F32)
  for s in range(ROW_SUB):
    stage[buf, pl.ds(s, TS, stride=ROW_SUB), :] = hsort[:, s * 128:(s + 1) * 128]

  runs = functools.partial(_run_copies, hist_ref, lstart_ref, dst_ref)
  runs(i, stage, buf, hs_hbm, sem, lambda cp: cp.start())
  pl.when(i > 0)(lambda: runs(i - 1, stage, 1 - buf, hs_hbm, sem, lambda cp: cp.wait()))
  pl.when(i == n - 1)(lambda: runs(i, stage, buf, hs_hbm, sem, lambda cp: cp.wait()))


def _scatter_rows(hist, lstart, dst, pad_start, pad_n, n_valid, x, g, shift, scale, pos):
  assert TS == TM and TS >= TMM
  grid_spec = pltpu.PrefetchScalarGridSpec(
      num_scalar_prefetch=6,
      grid=(T // TS,),
      in_specs=[
          pl.BlockSpec((TS, D), lambda i, *_: (i, 0)),
          pl.BlockSpec((1, D), lambda i, *_: (0, 0)),
          pl.BlockSpec((None, 1, D), lambda i, *_: (_seg(i, TS), 0, 0)),
          pl.BlockSpec((None, 1, D), lambda i, *_: (_seg(i, TS), 0, 0)),
          pl.BlockSpec((None, 8, TS), lambda i, *_: (i, 0, 0)),
      ],
      out_specs=pl.BlockSpec(memory_space=pl.ANY),
      scratch_shapes=[
          pltpu.VMEM((2, TS * ROW_SUB, 128), F32),
          pltpu.VMEM((ROW_SUB, 128), F32),
          pltpu.SemaphoreType.DMA((2,)),
          pltpu.SemaphoreType.DMA((2,)),
      ],
  )
  return pl.pallas_call(
      _scatter_kernel,
      grid_spec=grid_spec,
      out_shape=jax.ShapeDtypeStruct((P_PAD * ROW_SUB, 128), F32),
      compiler_params=_params(("arbitrary",)),
      name="scatter_rows",
  )(hist, lstart, dst, pad_start, pad_n, n_valid, x, g, shift, scale, pos)


def _router_logits(h, wrh_ref, wrl_ref, br_ref):
  h_hi = h.astype(BF16)
  h_lo = (h - h_hi.astype(F32)).astype(BF16)
  dot = functools.partial(jnp.dot, preferred_element_type=F32)
  return (dot(h_hi, wrh_ref[...]) + dot(h_lo, wrh_ref[...]) + dot(h_hi, wrl_ref[...])
          + br_ref[...])


def _expert_kernel(elo_ref, ehi_ref, nv_ref, hs_ref, wrh_ref, br_ref,
                   wg_lo, wg_hi, wu_lo, wu_hi, wd_lo, wd_hi, y_ref):
  i = pl.program_id(0)

  @pl.when(i < nv_ref[0])
  def _():
    h = jnp.concatenate(
        [hs_ref[pl.ds(s, TMM, stride=ROW_SUB), :] for s in range(ROW_SUB)], axis=1)
    hb = h.astype(BF16)
    logits = jnp.dot(hb, wrh_ref[...], preferred_element_type=F32) + br_ref[...]
    lane = lax.broadcasted_iota(jnp.int32, logits.shape, 1)
    elo, ehi = elo_ref[i], ehi_ref[i]

    def pick(idx):
      return jnp.sum(jnp.where(lane == idx, logits, 0.0), axis=-1, keepdims=True)
    l_g, l_lo, l_hi = pick(elo // EPG), pick(N_GROUPS + elo), pick(N_GROUPS + ehi)
    p_top = 1.0 / jnp.sum(jnp.where(lane < N_GROUPS, jnp.exp(logits - l_g), 0.0),
                          axis=-1, keepdims=True)
    wl = jnp.broadcast_to(p_top / (1.0 + jnp.exp(l_hi - l_lo)), (TMM, D_EXPERT))
    wh = jnp.broadcast_to(p_top / (1.0 + jnp.exp(l_lo - l_hi)), (TMM, D_EXPERT))
    dot = functools.partial(jnp.dot, preferred_element_type=F32)
    a_lo = (_silu(dot(hb, wg_lo[...])) * dot(hb, wu_lo[...])) * wl
    a_hi = (_silu(dot(hb, wg_hi[...])) * dot(hb, wu_hi[...])) * wh
    a_lo, a_hi = a_lo.astype(BF16), a_hi.astype(BF16)
    nb = 2 * 128
    for c in range(0, D, nb):
      y = dot(a_lo, wd_lo[:, c:c + nb]) + dot(a_hi, wd_hi[:, c:c + nb])
      for s in range(c // 128, (c + nb) // 128):
        y_ref[pl.ds(s, TMM, stride=ROW_SUB), :] = y[:, s * 128 - c:(s + 1) * 128 - c]

  @pl.when(i >= nv_ref[0])
  def _():
    y_ref[...] = jnp.zeros_like(y_ref)


def _experts(layer, tile_elo, tile_ehi, n_valid, hs, wr_hi, br, w_gate, w_up, w_down):
  up_spec_lo = pl.BlockSpec((None, None, D, D_EXPERT), lambda i, elo, ehi, nv: (layer, elo[i], 0, 0))
  up_spec_hi = pl.BlockSpec((None, None, D, D_EXPERT), lambda i, elo, ehi, nv: (layer, ehi[i], 0, 0))
  dn_spec_lo = pl.BlockSpec((None, None, D_EXPERT, D), lambda i, elo, ehi, nv: (layer, elo[i], 0, 0))
  dn_spec_hi = pl.BlockSpec((None, None, D_EXPERT, D), lambda i, elo, ehi, nv: (layer, ehi[i], 0, 0))
  grid_spec = pltpu.PrefetchScalarGridSpec(
      num_scalar_prefetch=3,
      grid=(NT_E,),
      in_specs=[
          pl.BlockSpec((TMM * ROW_SUB, 128), lambda i, elo, ehi, nv: (i, 0)),
          pl.BlockSpec((D, 128), lambda i, elo, ehi, nv: (0, 0)),
          pl.BlockSpec((1, 128), lambda i, elo, ehi, nv: (0, 0)),
          up_spec_lo, up_spec_hi, up_spec_lo, up_spec_hi, dn_spec_lo, dn_spec_hi,
      ],
      out_specs=pl.BlockSpec((TMM * ROW_SUB, 128), lambda i, elo, ehi, nv: (i, 0)),
  )
  return pl.pallas_call(
      _expert_kernel,
      grid_spec=grid_spec,
      out_shape=jax.ShapeDtypeStruct((P_PAD * ROW_SUB, 128), F32),
      compiler_params=_params(("arbitrary",)),
      name="experts",
  )(tile_elo, tile_ehi, n_valid, hs, wr_hi, br,
    w_gate, w_gate, w_up, w_up, w_down, w_down)


def _gather_copy(slot_ref, step, r, y_hbm, ybuf, buf, sem):
  return pltpu.make_async_copy(
      _tile_rows(y_hbm, _slot_of(slot_ref, step * TC + r)),
      ybuf.at[buf, pl.ds(r * ROW_SUB, ROW_SUB), :], sem.at[buf])


def _combine_kernel(slot_ref, x_ref, gate_ref, nf_ref, y_hbm, *rest, final):
  if final:
    op_ref, os_ref, ybuf, sem = rest
  else:
    o_ref, ybuf, sem = rest
  i = pl.program_id(0)
  n = pl.num_programs(0)
  buf = i % 2

  def start(step, b):
    def body(r2, carry):
      for q in range(2):
        _gather_copy(slot_ref, step, 2 * r2 + q, y_hbm, ybuf, b, sem).start(priority=q)
      return carry
    lax.fori_loop(0, TC // 2, body, 0, unroll=4)

  @pl.when(i == 0)
  def _():
    start(0, 0)

  @pl.when(i + 1 < n)
  def _():
    start(i + 1, 1 - buf)

  def wait(r, carry):
    _gather_copy(slot_ref, i, r, y_hbm, ybuf, buf, sem).wait()
    return carry
  lax.fori_loop(0, TC, wait, 0, unroll=8)

  y = jnp.concatenate(
      [ybuf[buf, pl.ds(s, TC, stride=ROW_SUB), :] for s in range(ROW_SUB)], axis=1)
  x = x_ref[...] + gate_ref[...] * y
  if not final:
    o_ref[...] = x
    return
  x = (x * lax.rsqrt(jnp.mean(x * x, axis=-1, keepdims=True) + EPS)) * nf_ref[...]
  is_prompt = i < T_P // TC

  @pl.when(is_prompt)
  def _():
    op_ref[...] = x

  @pl.when(jnp.logical_not(is_prompt))
  def _():
    os_ref[...] = x


def _combine(slot, x, gate, norm_f, y_sorted, final):
  npt = T_P // TC
  if final:
    out_specs = (pl.BlockSpec((TC, D), lambda i, s: (jnp.minimum(i, npt - 1), 0)),
                 pl.BlockSpec((TC, D), lambda i, s: (jnp.maximum(i - npt, 0), 0)))
    out_shape = (jax.ShapeDtypeStruct((T_P, D), F32), jax.ShapeDtypeStruct((T_S, D), F32))
  else:
    out_specs = pl.BlockSpec((TC, D), lambda i, s: (i, 0))
    out_shape = jax.ShapeDtypeStruct((T, D), F32)
  grid_spec = pltpu.PrefetchScalarGridSpec(
      num_scalar_prefetch=1,
      grid=(T // TC,),
      in_specs=[
          pl.BlockSpec((TC, D), lambda i, s: (i, 0)),
          pl.BlockSpec((None, 1, D), lambda i, s: (_seg(i, TC), 0, 0)),
          pl.BlockSpec((1, D), lambda i, s: (0, 0)),
          pl.BlockSpec(memory_space=pl.ANY),
      ],
      out_specs=out_specs,
      scratch_shapes=[pltpu.VMEM((2, TC * ROW_SUB, 128), F32), pltpu.SemaphoreType.DMA((2,))],
  )
  return pl.pallas_call(
      functools.partial(_combine_kernel, final=final),
      grid_spec=grid_spec,
      out_shape=out_shape,
      compiler_params=_params(("arbitrary",)),
      name="combine_final" if final else "combine",
  )(slot, x, gate, norm_f, y_sorted)


def _class_experts():
  lo, hi = [], []
  for g in range(N_GROUPS):
    for a in range(EPG):
      for b in range(a + 1, EPG):
        lo.append(g * EPG + a)
        hi.append(g * EPG + b)
  return np.asarray(lo, np.int32), np.asarray(hi, np.int32)


def _moe_layer(layer, x, routing, route_args, gate, w_gate, w_up, w_down, norm_f):
  final = layer == DEPTH - 1
  g, shift, scale, wr_hi, _, br, _ = route_args
  info, cnt = routing

  cls = info[:, 0, :].astype(jnp.int32)
  rank = info[:, 1, :].astype(jnp.int32)
  counts = cnt[:N_CLASSES, 0].astype(jnp.int32)
  tiles = (counts + TMM - 1) // TMM
  tile_end = jnp.cumsum(tiles)
  offs = (tile_end - tiles) * TMM
  hist = jnp.stack([jnp.sum((cls == k).astype(jnp.int32), axis=1) for k in range(N_CLASSES)], axis=1)
  before = jnp.cumsum(hist, axis=0) - hist
  lstart = jnp.cumsum(hist, axis=1) - hist
  dst = offs[None, :] + before
  slot = rank
  pos = rank
  for k in range(N_CLASSES):
    hit = cls == k
    slot = slot + jnp.where(hit, offs[k], 0)
    pos = pos + jnp.where(hit, (lstart[:, k] - before[:, k])[:, None], 0)
  slot = slot.reshape(T)
  pos = jnp.broadcast_to(pos.astype(F32)[:, None, :], (T // TM, 8, TM))
  n_valid = tile_end[-1]
  tile_ids = jnp.minimum(jnp.arange(NT_E, dtype=jnp.int32), n_valid - 1)
  tile_cls = jnp.sum((tile_ids[:, None] >= tile_end[None, :]).astype(jnp.int32), axis=1)
  tile_cls = jnp.minimum(tile_cls, N_CLASSES - 1)
  cls_lo, cls_hi = _class_experts()
  tile_elo = jnp.asarray(cls_lo)[tile_cls]
  tile_ehi = jnp.asarray(cls_hi)[tile_cls]

  n_valid = n_valid.reshape(1)
  hs = _scatter_rows(hist.reshape(-1), lstart.reshape(-1), dst.reshape(-1),
                     offs + counts, tiles * TMM - counts, n_valid, x, g, shift, scale, pos)
  y_sorted = _experts(layer, tile_elo, tile_ehi, n_valid, hs, wr_hi, br, w_gate, w_up, w_down)
  return _combine(slot, x, gate, norm_f, y_sorted, final)


def kernel(x_prompt, x_sample, cache_k, cache_v, state_ret, c, c_ctx, norm1, norm2, w_ada, b_ada, attn_w_qkv, attn_q_norm, attn_k_norm, attn_w_o, pool_w, pool_scale, ret_w_in, ret_decay_logit, ret_norm, ret_w_out, moe_w_router_g, moe_b_router_g, moe_w_router_e, moe_b_router_e, moe_w_gate, moe_w_up, moe_w_down, norm_f):
  x = (x_prompt.reshape(T_P, D), x_sample.reshape(T_S, D))
  cond8 = jnp.concatenate([c_ctx[None, :], c, jnp.zeros((N_SEG - 1 - NB_S, D), F32)], axis=0)
  mods = _adaln(cond8, w_ada, b_ada)
  mods = mods.reshape(DEPTH, N_SEG, 6, 1, D).transpose(0, 2, 1, 3, 4)
  rope = _rope_tables()
  tri = jnp.triu(jnp.ones((TM, TM), BF16), 1)
  pad_r = 128 - N_GROUPS - N_EXPERTS
  norm_f2 = norm_f.reshape(1, D)

  w_gate, w_up, w_down = moe_w_gate.astype(BF16), moe_w_up.astype(BF16), moe_w_down.astype(BF16)

  new_k, new_v, new_s = [], [], []
  for i in range(DEPTH):
    kind, j = i % 3, i // 3
    m = mods[i]
    g1 = norm1[i].reshape(1, D)
    wr = jnp.concatenate([moe_w_router_g[i], moe_w_router_e[i], jnp.zeros((D, pad_r), F32)], axis=1)
    br = jnp.concatenate([moe_b_router_g[i], moe_b_router_e[i], jnp.zeros((pad_r,), F32)]).reshape(1, 128)
    wr_hi = wr.astype(BF16)
    wr_lo = (wr - wr_hi.astype(F32)).astype(BF16)
    route_args = (norm2[i].reshape(1, D), m[3], m[4], wr_hi, wr_lo, br, tri)
    if kind == 0:
      qkv = _nm_matmul(x, g1, m[0], m[1], attn_w_qkv[j].astype(BF16), TM, "qkv_proj")
      qn = attn_q_norm[j].reshape(1, HEAD_DIM)
      kn = attn_k_norm[j].reshape(1, HEAD_DIM)
      o_p, kc, vc = _attn_prompt(qkv, qn, kn)
      ck = cache_k[:, j].reshape(NB_S, PAST, N_KV * HEAD_DIM)
      cv = cache_v[:, j].reshape(NB_S, PAST, N_KV * HEAD_DIM)
      o_s = _attn_latent(qkv, ck, cv, qn, kn, rope)
      new_k.append(kc.reshape(NB_P, SEQ_P, N_KV, HEAD_DIM))
      new_v.append(vc.reshape(NB_P, SEQ_P, N_KV, HEAD_DIM))
      x, *routing = _mm_res_route((o_p, o_s), attn_w_o[j].astype(BF16), x, m[2], route_args, "attn_out")
    elif kind == 1:
      x = _pool_layer(x, g1, m[0], m[1], m[2], pool_w[j].astype(BF16), pool_scale[j].reshape(1, D))
      routing = _router(x, route_args)
    else:
      p = _nm_matmul(x, g1, m[0], m[1], ret_w_in[j].astype(BF16), 256, "ret_proj")
      dl = jnp.broadcast_to(ret_decay_logit[j].T[:, :, None, None], (RET_HEADS, 2, 1, RET_DK))
      ng = ret_norm[j].reshape(1, RET_HEADS * RET_DV)
      y_p, s_new = _retention(p, dl, ng, None, True)
      y_s = _retention(p, dl, ng, state_ret[:, j], False)
      new_s.append(s_new)
      x, *routing = _mm_res_route((y_p, y_s), ret_w_out[j].astype(BF16), x, m[2], route_args, "ret_out")
    x = _moe_layer(i, x, routing, route_args, m[5], w_gate, w_up, w_down, norm_f2)

  y_prompt = x[0].reshape(NB_P, SEQ_P, D)
  y_sample = x[1].reshape(NB_S, SEQ_S, D)
  new_cache_k = jnp.stack(new_k, axis=1)
  new_cache_v = jnp.stack(new_v, axis=1)
  assert len(new_s) == 1
  new_state_ret = new_s[0].reshape(NB_P, 1, 2, RET_HEADS, RET_DK, RET_DV)
  return (y_prompt, y_sample, new_cache_k, new_cache_v, new_state_ret)
```

```python
import functools

import jax
import jax.numpy as jnp
import numpy as np
from jax import lax
from jax.experimental import pallas as pl
from jax.experimental.pallas import tpu as pltpu

F32 = jnp.float32
BF16 = jnp.bfloat16

D = 1024
NB_P, SEQ_P = 32, 256
NB_S, SEQ_S = 4, 2048
T_P = NB_P * SEQ_P
T_S = NB_S * SEQ_S
T = T_P + T_S
DEPTH = 4
GRID_W = 64
HEAD_DIM = 128
N_HEADS = 8
N_KV = 2
KV_GROUP = N_HEADS // N_KV
PAST = 256
ROPE_THETA = 10000.0
POOL_WINDOWS = (2, 4, 8, 16)
POOL_GROUP = D // 4
POOL_HALO = 8
RET_HEADS = 8
RET_DK = 128
RET_DV = 256
RET_CHUNK = 128
RET_HEADS_PER_STEP_PROMPT = 4
RET_HEADS_PER_STEP_LATENT = 2
N_GROUPS = 4
EPG = 4
N_EXPERTS = 16
D_EXPERT = 256
N_PAIRS = 6
N_CLASSES = N_GROUPS * N_PAIRS
ROUTER_ROWS = 32
EPS = 1e-6
LOG2E = 1.4426950408889634
NEG = -1e30
N_SEG = 8

VMEM_LIMIT_BYTES = 52 * 1024 * 1024

TM = 512
TQ = 512
ATTN_UNIT_Q = 128
ATTN_KEY_CHUNK = 256
ATTN_HEAD_STACK = 2
TP = 256
TMM = 256
TS = 512
TC = 256
ROW_SUB = 8
PACK_SUB = D // 2 // 128
INFO_SUB = PACK_SUB
P_PAD = T + N_CLASSES * TMM
NT_E = P_PAD // TMM


def _params(sem):
  return pltpu.CompilerParams(dimension_semantics=sem, vmem_limit_bytes=VMEM_LIMIT_BYTES)


def _seg(i, tm):
  npt = T_P // tm
  return jnp.where(i < npt, 0, (i - npt) // (SEQ_S // tm) + 1)


def _norm_mod(x, g, shift, scale):
  r = lax.rsqrt(jnp.mean(x * x, axis=-1, keepdims=True) + EPS)
  return ((x * r) * g) * (1.0 + scale) + shift


def _silu(x):
  return x * (1.0 / (1.0 + jnp.exp(-x)))


def _adaln_kernel(c_ref, w_ref, b_ref, o_ref):
  s = _silu(c_ref[...]).astype(BF16)
  o_ref[...] = jnp.dot(s, w_ref[...].astype(BF16), preferred_element_type=F32) + b_ref[...]


def _adaln(cond8, w_ada, b_ada):
  tn = 1536
  return pl.pallas_call(
      _adaln_kernel,
      grid=(DEPTH, 6 * D // tn),
      in_specs=[
          pl.BlockSpec((N_SEG, D), lambda l, j: (0, 0)),
          pl.BlockSpec((None, D, tn), lambda l, j: (l, 0, j)),
          pl.BlockSpec((None, 1, tn), lambda l, j: (l, 0, j)),
      ],
      out_specs=pl.BlockSpec((None, N_SEG, tn), lambda l, j: (l, 0, j)),
      out_shape=jax.ShapeDtypeStruct((DEPTH, N_SEG, 6 * D), F32),
      compiler_params=_params(("arbitrary", "arbitrary")),
      name="adaln",
  )(cond8, w_ada, b_ada.reshape(DEPTH, 1, 6 * D))


def _token_rows(x, tm, width):
  npt = T_P // tm
  if isinstance(x, tuple):
    arrays = x
    latent_map = lambda i: (jnp.maximum(i - npt, 0), 0)
  else:
    arrays = (x, x)
    latent_map = lambda i: (jnp.maximum(i, npt), 0)
  specs = [pl.BlockSpec((tm, width), lambda i: (jnp.minimum(i, npt - 1), 0)),
           pl.BlockSpec((tm, width), latent_map)]
  return arrays, specs


def _pick_rows(p_ref, s_ref, tm):
  return jnp.where(pl.program_id(0) < T_P // tm, p_ref[...], s_ref[...])


def _nm_matmul_kernel(xp_ref, xs_ref, g_ref, sh_ref, sc_ref, w_ref, o_ref, *, tm, n_chunk):
  x = _pick_rows(xp_ref, xs_ref, tm)
  h = _norm_mod(x, g_ref[...], sh_ref[...], sc_ref[...]).astype(BF16)
  n = w_ref.shape[1]
  for c in range(0, n, n_chunk):
    o_ref[:, c:c + n_chunk] = jnp.dot(
        h, w_ref[:, c:c + n_chunk], preferred_element_type=F32).astype(o_ref.dtype)


def _nm_matmul(x, g, shift, scale, w, tm, name):
  n = w.shape[1]
  x_arrays, x_specs = _token_rows(x, tm, D)
  return pl.pallas_call(
      functools.partial(_nm_matmul_kernel, tm=tm, n_chunk=512),
      grid=(T // tm,),
      in_specs=x_specs + [
          pl.BlockSpec((1, D), lambda i: (0, 0)),
          pl.BlockSpec((None, 1, D), lambda i: (_seg(i, tm), 0, 0)),
          pl.BlockSpec((None, 1, D), lambda i: (_seg(i, tm), 0, 0)),
          pl.BlockSpec((D, n), lambda i: (0, 0)),
      ],
      out_specs=pl.BlockSpec((tm, n), lambda i: (i, 0)),
      out_shape=jax.ShapeDtypeStruct((T, n), BF16),
      compiler_params=_params(("arbitrary",)),
      name=name,
  )(*x_arrays, g, shift, scale, w)


def _mm_res_route_kernel(ap_ref, as_ref, xp_ref, xs_ref, w_ref, gate_ref, *rest):
  n_route_in = len(_ROUTE_IN_SPECS)
  route_in, (o_ref, *route_out) = rest[:n_route_in], rest[n_route_in:]
  a = _pick_rows(ap_ref, as_ref, TM)
  x = _pick_rows(xp_ref, xs_ref, TM)
  x_new = x + gate_ref[...] * jnp.dot(a, w_ref[...], preferred_element_type=F32)
  o_ref[...] = x_new
  _route_tile(x_new, *route_in, *route_out)


def _mm_res_route(a, w, x, gate, route_args, name):
  k = w.shape[0]
  a_arrays, a_specs = _token_rows(a, TM, k)
  x_arrays, x_specs = _token_rows(x, TM, D)
  return pl.pallas_call(
      _mm_res_route_kernel,
      grid=(T // TM,),
      in_specs=a_specs + x_specs + [
          pl.BlockSpec((k, D), lambda i: (0, 0)),
          pl.BlockSpec((None, 1, D), lambda i: (_seg(i, TM), 0, 0)),
      ] + _ROUTE_IN_SPECS,
      out_specs=(pl.BlockSpec((TM, D), lambda i: (i, 0)),) + _ROUTE_OUT_SPECS,
      out_shape=(jax.ShapeDtypeStruct((T, D), F32),) + _ROUTE_OUT_SHAPE,
      scratch_shapes=_ROUTE_SCRATCH,
      compiler_params=_params(("arbitrary",)),
      name=name,
  )(*a_arrays, *x_arrays, w, gate, *route_args)


def _rope(x, c, a, b):
  return x * c + pltpu.roll(x, 96, 1) * a + pltpu.roll(x, 32, 1) * b


def _head_norm(x, w):
  return (x * lax.rsqrt(jnp.mean(x * x, axis=-1, keepdims=True) + EPS)) * w


def _attn_latent_kernel(q_ref, kn_ref, vn_ref, ck_ref, cv_ref, qw_ref, kw_ref,
                        cq_ref, aq_ref, bq_ref, ckk_ref, akk_ref, bkk_ref,
                        o_ref, k_s, vt_s, *st_refs, tq):
  @pl.when(pl.program_id(2) == 0)
  def _():
    k = _rope(_head_norm(kn_ref[...].astype(F32), kw_ref[...]), ckk_ref[...], akk_ref[...], bkk_ref[...])
    k_s[0:PAST, :] = ck_ref[...].astype(BF16)
    k_s[PAST:, :] = k.astype(BF16)
    vt_s[:, 0:PAST] = cv_ref[...].T.astype(BF16)
    vt_s[:, PAST:] = vn_ref[...].astype(F32).T.astype(BF16)

  qa = q_ref[...].astype(F32)
  lk = k_s.shape[0]

  def prep_q(h):
    qh = _head_norm(qa[:, h * HEAD_DIM:(h + 1) * HEAD_DIM], qw_ref[...])
    qh = _rope(qh, cq_ref[...], aq_ref[...], bq_ref[...])
    return (qh * (HEAD_DIM ** -0.5 * LOG2E)).astype(BF16)

  nq = ATTN_HEAD_STACK * ATTN_UNIT_Q
  chunks = [(c0, min(c0 + ATTN_KEY_CHUNK, lk)) for c0 in range(0, lk, ATTN_KEY_CHUNK)]
  qh_all = [prep_q(h) for h in range(KV_GROUP)]
  units = [(h0, r0) for r0 in range(0, tq, ATTN_UNIT_Q) for h0 in range(0, KV_GROUP, ATTN_HEAD_STACK)]

  def score_chunk(u, c0, c1, m):
    h0, r0 = units[u]
    qp = jnp.concatenate([qh_all[h0 + d][r0:r0 + ATTN_UNIT_Q] for d in range(ATTN_HEAD_STACK)], axis=0)
    st = lax.dot_general(k_s[c0:c1, :], qp, (((1,), (1,)), ((), ())),
                         preferred_element_type=F32)
    st_refs[u % 2][c0:c1, :] = st
    return jnp.maximum(m, jnp.max(st, axis=0, keepdims=True))

  def value_chunk(u, c0, c1, m, l, acc):
    pt = jnp.exp2(st_refs[u % 2][c0:c1, :] - m)
    return (l + jnp.sum(pt, axis=0, keepdims=True),
            acc + jnp.dot(vt_s[:, c0:c1], pt.astype(BF16), preferred_element_type=F32))

  def finish(u, l, acc):
    h0, r0 = units[u]
    o = acc / l
    for d in range(ATTN_HEAD_STACK):
      h = h0 + d
      o_ref[r0:r0 + ATTN_UNIT_Q, h * HEAD_DIM:(h + 1) * HEAD_DIM] = (
          o[:, d * ATTN_UNIT_Q:(d + 1) * ATTN_UNIT_Q].T.astype(BF16))

  m_prev = None
  for u in range(len(units) + 1):
    m = jnp.full((1, nq), NEG, F32)
    l = jnp.zeros((1, nq), F32)
    acc = jnp.zeros((HEAD_DIM, nq), F32)
    for c0, c1 in chunks:
      if u < len(units):
        m = score_chunk(u, c0, c1, m)
      if u > 0:
        l, acc = value_chunk(u - 1, c0, c1, m_prev, l, acc)
    if u > 0:
      finish(u - 1, l, acc)
    m_prev = m


def _attn_scratch(lk):
  return ([pltpu.VMEM((lk, HEAD_DIM), BF16), pltpu.VMEM((HEAD_DIM, lk), BF16)]
          + [pltpu.VMEM((lk, ATTN_HEAD_STACK * ATTN_UNIT_Q), F32) for _ in range(2)])


def _rope_tables():
  rows = SEQ_S // GRID_W
  t_row = jnp.broadcast_to(jnp.arange(rows)[:, None], (rows, GRID_W)).reshape(-1)
  t_col = jnp.broadcast_to(jnp.arange(GRID_W)[None, :], (rows, GRID_W)).reshape(-1)
  nf = HEAD_DIM // 4
  inv = ROPE_THETA ** (-jnp.arange(nf, dtype=F32) / nf)
  ang_r = t_row.astype(F32)[:, None] * inv[None, :]
  ang_c = t_col.astype(F32)[:, None] * inv[None, :]
  cr, sr, cc, sc = jnp.cos(ang_r), jnp.sin(ang_r), jnp.cos(ang_c), jnp.sin(ang_c)
  z = jnp.zeros_like(sr)
  c = jnp.concatenate([cr, cr, cc, cc], axis=1)
  a = jnp.concatenate([-sr, z, -sc, z], axis=1)
  b = jnp.concatenate([z, sr, z, sc], axis=1)
  return c, a, b


def _attn_prompt_kernel(qkv_ref, qw_ref, kw_ref, o_ref, ko_ref, vo_ref):
  nq = N_HEADS * HEAD_DIM
  nk = N_KV * HEAD_DIM
  for kh in range(N_KV):
    kcols = slice(nq + kh * HEAD_DIM, nq + (kh + 1) * HEAD_DIM)
    vcols = slice(nq + nk + kh * HEAD_DIM, nq + nk + (kh + 1) * HEAD_DIM)
    k = _head_norm(qkv_ref[:, kcols].astype(F32), kw_ref[...])
    v = qkv_ref[:, vcols].astype(F32)
    ko_ref[pl.ds(kh, SEQ_P, stride=N_KV), :] = k
    vo_ref[pl.ds(kh, SEQ_P, stride=N_KV), :] = v
    kb = k.astype(BF16)
    vt = v.T.astype(BF16)
    for h in range(kh * KV_GROUP, (kh + 1) * KV_GROUP):
      hcols = slice(h * HEAD_DIM, (h + 1) * HEAD_DIM)
      qh = _head_norm(qkv_ref[:, hcols].astype(F32), qw_ref[...])
      qh = (qh * (HEAD_DIM ** -0.5 * LOG2E)).astype(BF16)
      st = lax.dot_general(kb, qh, (((1,), (1,)), ((), ())), preferred_element_type=F32)
      pt = jnp.exp2(st - jnp.max(st, axis=0, keepdims=True))
      l = jnp.sum(pt, axis=0, keepdims=True)
      acc = jnp.dot(vt, pt.astype(BF16), preferred_element_type=F32)
      o_ref[:, hcols] = (acc / l).T.astype(BF16)


def _attn_prompt(qkv, q_norm, k_norm):
  width = (N_HEADS + 2 * N_KV) * HEAD_DIM
  out_shapes = (
      jax.ShapeDtypeStruct((T_P, N_HEADS * HEAD_DIM), BF16),
      jax.ShapeDtypeStruct((NB_P, SEQ_P * N_KV, HEAD_DIM), F32),
      jax.ShapeDtypeStruct((NB_P, SEQ_P * N_KV, HEAD_DIM), F32),
  )
  return pl.pallas_call(
      _attn_prompt_kernel,
      grid=(NB_P,),
      in_specs=[
          pl.BlockSpec((SEQ_P, width), lambda b: (b, 0)),
          pl.BlockSpec((1, HEAD_DIM), lambda b: (0, 0)),
          pl.BlockSpec((1, HEAD_DIM), lambda b: (0, 0)),
      ],
      out_specs=(
          pl.BlockSpec((SEQ_P, N_HEADS * HEAD_DIM), lambda b: (b, 0)),
          pl.BlockSpec((None, SEQ_P * N_KV, HEAD_DIM), lambda b: (b, 0, 0)),
          pl.BlockSpec((None, SEQ_P * N_KV, HEAD_DIM), lambda b: (b, 0, 0)),
      ),
      out_shape=out_shapes,
      compiler_params=_params(("arbitrary",)),
      name="attn_prompt",
  )(qkv, q_norm, k_norm)


def _attn_latent(qkv, cache_k, cache_v, q_norm, k_norm, rope):
  kcol = N_HEADS
  vcol = kcol + N_KV
  nq = SEQ_S // TQ
  row0 = T_P // TQ
  seq0 = T_P // SEQ_S
  lk = PAST + SEQ_S
  c, a, b = rope
  tab_q = pl.BlockSpec((TQ, HEAD_DIM), lambda bb, h, i: (i, 0))
  tab_k = pl.BlockSpec((SEQ_S, HEAD_DIM), lambda bb, h, i: (0, 0))
  return pl.pallas_call(
      functools.partial(_attn_latent_kernel, tq=TQ),
      grid=(NB_S, N_KV, nq),
      in_specs=[
          pl.BlockSpec((TQ, KV_GROUP * HEAD_DIM), lambda bb, h, i: (row0 + bb * nq + i, h)),
          pl.BlockSpec((SEQ_S, HEAD_DIM), lambda bb, h, i: (seq0 + bb, kcol + h)),
          pl.BlockSpec((SEQ_S, HEAD_DIM), lambda bb, h, i: (seq0 + bb, vcol + h)),
          pl.BlockSpec((None, PAST, HEAD_DIM), lambda bb, h, i: (bb, 0, h)),
          pl.BlockSpec((None, PAST, HEAD_DIM), lambda bb, h, i: (bb, 0, h)),
          pl.BlockSpec((1, HEAD_DIM), lambda bb, h, i: (0, 0)),
          pl.BlockSpec((1, HEAD_DIM), lambda bb, h, i: (0, 0)),
          tab_q, tab_q, tab_q, tab_k, tab_k, tab_k,
      ],
      out_specs=pl.BlockSpec((TQ, KV_GROUP * HEAD_DIM), lambda bb, h, i: (bb * nq + i, h)),
      out_shape=jax.ShapeDtypeStruct((T_S, N_HEADS * HEAD_DIM), BF16),
      scratch_shapes=_attn_scratch(lk),
      compiler_params=_params(("arbitrary", "arbitrary", "arbitrary")),
      name="attn_latent",
  )(qkv, qkv, qkv, cache_k, cache_v, q_norm, k_norm, c, a, b, c, a, b)


def _pool_kernel(x_ref, xp_ref, xn_ref, g_ref, sh_ref, sc_ref, gate_ref, w_ref, ps_ref, o_ref):
  t = pl.program_id(0)
  npt = T_P // TP
  tiles_s = SEQ_S // TP
  is_p = t < npt
  pos = jnp.where(is_p, 0, (t - npt) % tiles_s)
  ntile = jnp.where(is_p, SEQ_P // TP, tiles_s)
  seq_len = ntile * TP
  keep_prev = jnp.where(pos == 0, 0.0, 1.0)
  keep_next = jnp.where(pos == ntile - 1, 0.0, 1.0)

  g, sh, sc = g_ref[...], sh_ref[...], sc_ref[...]
  x = x_ref[...]
  h = _norm_mod(x, g, sh, sc)
  hp = _norm_mod(xp_ref[...], g, sh, sc) * keep_prev
  hn = _norm_mod(xn_ref[...], g, sh, sc) * keep_next
  ext = jnp.concatenate([hp, h, hn], axis=0)
  n_ext = TP + 2 * POOL_HALO
  tseq = pos * TP + lax.broadcasted_iota(jnp.int32, (TP, 128), 0)

  outs = []
  for gi, win in enumerate(POOL_WINDOWS):
    lo, hi = gi * POOL_GROUP, (gi + 1) * POOL_GROUP
    acc = ext[:, lo:hi]
    span = 1
    while span < win:
      acc = acc + pltpu.roll(acc, n_ext - span, 0)
      span *= 2
    start = POOL_HALO - win // 2
    if start:
      acc = pltpu.roll(acc, n_ext - start, 0)
    ssum = acc[0:TP]
    cnt = (jnp.minimum(tseq + win // 2, seq_len) - jnp.maximum(tseq - win // 2, 0)).astype(F32)
    cnt = jnp.concatenate([cnt] * (POOL_GROUP // 128), axis=1)
    dlt = (ssum / cnt - h[:, lo:hi]).astype(BF16)
    outs.append(jnp.dot(dlt, w_ref[gi], preferred_element_type=F32))
  y = jnp.concatenate(outs, axis=1) * ps_ref[...]
  o_ref[...] = x + gate_ref[...] * y


def _pool_layer(x, g, shift, scale, gate, w, pscale):
  hb = TP // POOL_HALO
  last = T // POOL_HALO - 1
  seg = lambda i: _seg(i, TP)
  return pl.pallas_call(
      _pool_kernel,
      grid=(T // TP,),
      in_specs=[
          pl.BlockSpec((TP, D), lambda i: (i, 0)),
          pl.BlockSpec((POOL_HALO, D), lambda i: (jnp.maximum(i * hb - 1, 0), 0)),
          pl.BlockSpec((POOL_HALO, D), lambda i: (jnp.minimum((i + 1) * hb, last), 0)),
          pl.BlockSpec((1, D), lambda i: (0, 0)),
          pl.BlockSpec((None, 1, D), lambda i: (seg(i), 0, 0)),
          pl.BlockSpec((None, 1, D), lambda i: (seg(i), 0, 0)),
          pl.BlockSpec((None, 1, D), lambda i: (seg(i), 0, 0)),
          pl.BlockSpec((4, POOL_GROUP, POOL_GROUP), lambda i: (0, 0, 0)),
          pl.BlockSpec((1, D), lambda i: (0, 0)),
      ],
      out_specs=pl.BlockSpec((TP, D), lambda i: (i, 0)),
      out_shape=jax.ShapeDtypeStruct((T, D), F32),
      compiler_params=_params(("arbitrary",)),
      name="pool",
  )(x, x, x, g, shift, scale, gate, w, pscale)


def _ret_kernel(*refs, seq_len, has_s0, nh):
  it = iter(refs)
  q_ref, k_ref, v_ref, gate_ref, dl_ref, ng_ref = [next(it) for _ in range(6)]
  s0_ref = next(it) if has_s0 else None
  y_ref = next(it)
  so_ref = None if has_s0 else next(it)
  u_s = next(it)
  dec_s, cdec_s = next(it), next(it)
  c = RET_CHUNK
  n = seq_len // c
  kscale = RET_DK ** -0.5
  nt = (((1,), (1,)), ((), ()))
  tn = (((0,), (0,)), ((), ()))

  @pl.when(pl.program_id(1) == 0)
  def _():
    ri = lax.broadcasted_iota(jnp.int32, (c, c), 0).astype(F32)
    ci = lax.broadcasted_iota(jnp.int32, (c, c), 1).astype(F32)
    ri2 = lax.broadcasted_iota(jnp.int32, (c, RET_DV), 0).astype(F32)
    for hh in range(nh):
      lg = -jnp.log1p(jnp.exp(-dl_ref[hh]))
      lgf, lgb = lg[0], lg[1]
      lgf2 = jnp.concatenate([lgf, lgf], axis=1)
      lgb2 = jnp.concatenate([lgb, lgb], axis=1)
      dec_s[hh, 0] = jnp.concatenate(
          [jnp.exp((c - 1.0 - ri) * lgf), jnp.exp(ri * lgb)], axis=1) * kscale
      intra = (jnp.where(ri >= ci, jnp.exp((ri - ci) * lgf), 0.0)
               + jnp.where(ri <= ci, jnp.exp((ci - ri) * lgb), 0.0)) * kscale
      dec_s[hh, 1] = jnp.concatenate([intra, intra], axis=1)
      dec_s[hh, 2] = jnp.exp((ri2 + 1.0) * lgf2)
      dec_s[hh, 3] = jnp.exp((c - ri2) * lgb2)
      cdec_s[hh, 0] = jnp.exp(c * lgf2)
      cdec_s[hh, 1] = jnp.exp(c * lgb2)

  for hh in range(nh):
    qcols = slice(hh * RET_DK, (hh + 1) * RET_DK)
    vcols = slice(hh * RET_DV, (hh + 1) * RET_DV)
    kd2 = dec_s[hh, 0]

    for j in range(n):
      rows = slice(j * c, (j + 1) * c)
      kc = k_ref[rows, qcols].astype(F32)
      k2 = (jnp.concatenate([kc, kc], axis=1) * kd2).astype(BF16)
      u = lax.dot_general(k2, v_ref[rows, vcols], tn, preferred_element_type=F32)
      u_s[0, hh, j] = u[0:RET_DK]
      u_s[1, hh, j] = u[RET_DK:]

    for d, order in ((0, range(n)), (1, reversed(range(n)))):
      cdec = cdec_s[hh, d]
      st = s0_ref[d, hh] if has_s0 else jnp.zeros((RET_DK, RET_DV), F32)
      for j in order:
        u = u_s[d, hh, j]
        u_s[d, hh, j] = st
        st = st * cdec + u
      if not has_s0:
        so_ref[d, hh] = st

    intra = dec_s[hh, 1][:, :c]
    qd_f = dec_s[hh, 2]
    qd_b = dec_s[hh, 3]
    for j in range(n):
      rows = slice(j * c, (j + 1) * c)
      qc = q_ref[rows, qcols]
      vc = v_ref[rows, vcols]
      s = lax.dot_general(qc, k_ref[rows, qcols], nt, preferred_element_type=F32) * intra
      st2 = jnp.concatenate([u_s[0, hh, j], u_s[1, hh, j]], axis=1).astype(BF16)
      inter = jnp.dot(qc, st2, preferred_element_type=F32)
      o = (jnp.dot(s.astype(BF16), vc, preferred_element_type=F32)
           + inter[:, :RET_DV] * qd_f + inter[:, RET_DV:] * qd_b)
      mu = jnp.mean(o, axis=-1, keepdims=True)
      dv = o - mu
      var = jnp.mean(dv * dv, axis=-1, keepdims=True)
      on = (dv * lax.rsqrt(var + EPS)) * ng_ref[:, vcols]
      y_ref[rows, vcols] = (on * _silu(gate_ref[rows, vcols].astype(F32))).astype(BF16)


def _retention(p, dl, ng, state0, prompt):
  seq_len = SEQ_P if prompt else SEQ_S
  nb = NB_P if prompt else NB_S
  nh = RET_HEADS_PER_STEP_PROMPT if prompt else RET_HEADS_PER_STEP_LATENT
  row0 = 0 if prompt else T_P // SEQ_S
  hb = RET_HEADS // nh
  kcol = hb
  vcol = (2 * RET_HEADS * RET_DK) // (nh * RET_DV)
  gcol = vcol + hb
  in_specs = [
      pl.BlockSpec((seq_len, nh * RET_DK), lambda h, b: (row0 + b, h)),
      pl.BlockSpec((seq_len, nh * RET_DK), lambda h, b: (row0 + b, kcol + h)),
      pl.BlockSpec((seq_len, nh * RET_DV), lambda h, b: (row0 + b, vcol + h)),
      pl.BlockSpec((seq_len, nh * RET_DV), lambda h, b: (row0 + b, gcol + h)),
      pl.BlockSpec((nh, 2, 1, RET_DK), lambda h, b: (h, 0, 0, 0)),
      pl.BlockSpec((1, nh * RET_DV), lambda h, b: (0, h)),
  ]
  args = [p, p, p, p, dl, ng]
  y_spec = pl.BlockSpec((seq_len, nh * RET_DV), lambda h, b: (b, h))
  y_shape = jax.ShapeDtypeStruct((nb * seq_len, RET_HEADS * RET_DV), BF16)
  state_spec = pl.BlockSpec((None, 2, nh, RET_DK, RET_DV), lambda h, b: (b, 0, h, 0, 0))
  if prompt:
    out_specs = (y_spec, state_spec)
    out_shape = (y_shape, jax.ShapeDtypeStruct((NB_P, 2, RET_HEADS, RET_DK, RET_DV), F32))
  else:
    in_specs.append(state_spec)
    args.append(state0)
    out_specs = y_spec
    out_shape = y_shape
  return pl.pallas_call(
      functools.partial(_ret_kernel, seq_len=seq_len, has_s0=not prompt, nh=nh),
      grid=(hb, nb),
      in_specs=in_specs,
      out_specs=out_specs,
      out_shape=out_shape,
      scratch_shapes=[
          pltpu.VMEM((2, nh, seq_len // RET_CHUNK, RET_DK, RET_DV), F32),
          pltpu.VMEM((nh, 4, RET_CHUNK, RET_DV), F32),
          pltpu.VMEM((nh, 2, 1, RET_DV), F32),
      ],
      compiler_params=_params(("arbitrary", "arbitrary")),
      name="ret_prompt" if prompt else "ret_latent",
  )(*args)


def _router_kernel(x_ref, *route_refs):
  _route_tile(x_ref[...], *route_refs)


def _route_tile(x, g_ref, sh_ref, sc_ref, wrh_ref, wrl_ref, br_ref, triu_ref,
                info_ref, cnt_ref, before_ref, carry_s):
  @pl.when(pl.program_id(0) == 0)
  def _():
    carry_s[...] = jnp.zeros_like(carry_s)

  h = _norm_mod(x, g_ref[...], sh_ref[...], sc_ref[...])
  logits = _router_logits(h, wrh_ref, wrl_ref, br_ref)
  lt = logits.T[0:ROUTER_ROWS]
  row = lax.broadcasted_iota(jnp.int32, lt.shape, 0)
  big = jnp.int32(ROUTER_ROWS)

  def first_max(v):
    m = jnp.max(v, axis=0, keepdims=True)
    return jnp.min(jnp.where(v == m, row, big), axis=0, keepdims=True)

  gidx = first_max(jnp.where(row < N_GROUPS, lt, NEG))
  lo = N_GROUPS + EPG * gidx
  le = jnp.where((row >= lo) & (row < lo + EPG), lt, NEG)
  i1 = first_max(le)
  i2 = first_max(jnp.where(row == i1, NEG, le))
  e_lo = jnp.minimum(i1, i2) - N_GROUPS
  e_hi = jnp.maximum(i1, i2) - N_GROUPS
  a = e_lo - EPG * gidx
  b = e_hi - EPG * gidx
  pair_base = jnp.where(a == 0, 0, jnp.where(a == 1, 3, 5))
  cls = N_PAIRS * gidx + pair_base + (b - a - 1)

  onehot = jnp.where(row == cls, 1.0, 0.0)
  before = jnp.dot(onehot.astype(BF16), triu_ref[...], preferred_element_type=F32) + carry_s[...]
  rank = jnp.sum(jnp.where(row == cls, before, 0.0), axis=0, keepdims=True)
  before_ref[...] = carry_s[:, 0:128]
  carry_s[...] = carry_s[...] + jnp.sum(onehot, axis=1, keepdims=True)
  cnt_ref[...] = carry_s[...]

  row8 = lax.broadcasted_iota(jnp.int32, info_ref.shape, 0)
  info_ref[...] = jnp.where(row8 == 0, cls.astype(F32), jnp.where(row8 == 1, rank, 0.0))


_ROUTE_IN_SPECS = [
    pl.BlockSpec((1, D), lambda i: (0, 0)),
    pl.BlockSpec((None, 1, D), lambda i: (_seg(i, TM), 0, 0)),
    pl.BlockSpec((None, 1, D), lambda i: (_seg(i, TM), 0, 0)),
    pl.BlockSpec((D, 128), lambda i: (0, 0)),
    pl.BlockSpec((D, 128), lambda i: (0, 0)),
    pl.BlockSpec((1, 128), lambda i: (0, 0)),
    pl.BlockSpec((TM, TM), lambda i: (0, 0)),
]
_ROUTE_OUT_SPECS = (
    pl.BlockSpec((None, 8, TM), lambda i: (i, 0, 0)),
    pl.BlockSpec((ROUTER_ROWS, TM), lambda i: (0, 0)),
    pl.BlockSpec((None, ROUTER_ROWS, 128), lambda i: (i, 0, 0)),
)
_ROUTE_OUT_SHAPE = (
    jax.ShapeDtypeStruct((T // TM, 8, TM), F32),
    jax.ShapeDtypeStruct((ROUTER_ROWS, TM), F32),
    jax.ShapeDtypeStruct((T // TM, ROUTER_ROWS, 128), F32),
)
_ROUTE_SCRATCH = [pltpu.VMEM((ROUTER_ROWS, TM), F32)]


def _router(x, route_args):
  return pl.pallas_call(
      _router_kernel,
      grid=(T // TM,),
      in_specs=[pl.BlockSpec((TM, D), lambda i: (i, 0))] + _ROUTE_IN_SPECS,
      out_specs=_ROUTE_OUT_SPECS,
      out_shape=_ROUTE_OUT_SHAPE,
      scratch_shapes=_ROUTE_SCRATCH,
      compiler_params=_params(("arbitrary",)),
      name="router",
  )(x, *route_args)


def _slot_of(slot_ref, token):
  return slot_ref[token]


def _tile_rows(ref, row):
  return ref.at[pl.ds(pl.multiple_of(row * ROW_SUB, ROW_SUB), ROW_SUB), :]


RUN_CHUNK = 8


def _run_copy(stage, buf, hs_hbm, sem, src, dst, rows):
  return pltpu.make_async_copy(
      stage.at[buf, pl.ds(pl.multiple_of(src * ROW_SUB, ROW_SUB), rows * ROW_SUB), :],
      hs_hbm.at[pl.ds(pl.multiple_of(dst * ROW_SUB, ROW_SUB), rows * ROW_SUB), :],
      sem.at[buf])


def _start_run_copies(hist_ref, lstart_ref, dst_ref, step, stage, buf, hs_hbm, sem):
  shift = RUN_CHUNK.bit_length() - 1
  for c in range(N_CLASSES):
    k = step * N_CLASSES + c
    n_rows, src, dst = hist_ref[k], lstart_ref[k], dst_ref[k]

    def chunk(q, carry, src=src, dst=dst):
      _run_copy(stage, buf, hs_hbm, sem, src + q * RUN_CHUNK, dst + q * RUN_CHUNK, RUN_CHUNK).start()
      return carry
    n_chunks = lax.shift_right_logical(n_rows, jnp.int32(shift))
    lax.fori_loop(0, n_chunks, chunk, 0)
    src, dst = src + n_chunks * RUN_CHUNK, dst + n_chunks * RUN_CHUNK
    size = RUN_CHUNK // 2
    while size:
      take = n_rows & size

      @pl.when(take != 0)
      def _(src=src, dst=dst, size=size):
        _run_copy(stage, buf, hs_hbm, sem, src, dst, size).start()
      src, dst = src + take, dst + take
      size //= 2


def _wait_run_copies(stage, buf, hs_hbm, sem):
  pltpu.make_async_copy(stage.at[buf], hs_hbm.at[pl.ds(0, TS * ROW_SUB), :], sem.at[buf]).wait()


def _scatter_kernel(hist_ref, lstart_ref, dst_ref, pstart_ref, pn_ref, nv_ref,
                    x_ref, g_ref, sh_ref, sc_ref, pos_ref, hs_hbm,
                    stage, zero_s, sem, zsem):
  i = pl.program_id(0)
  n = pl.num_programs(0)
  buf = i % 2

  def pad_copy(c, r):
    return pltpu.make_async_copy(zero_s, _tile_rows(hs_hbm, pstart_ref[c] + r), zsem.at[0])

  def tail_copy(t):
    rows = TMM * ROW_SUB
    return pltpu.make_async_copy(
        stage.at[1, pl.ds(0, rows), :],
        hs_hbm.at[pl.ds(pl.multiple_of(t * rows, rows), rows), :], zsem.at[1])

  @pl.when(i == 0)
  def _():
    stage[...] = jnp.zeros_like(stage)
    zero_s[...] = jnp.zeros_like(zero_s)
    for c in range(N_CLASSES):
      def start(r, carry, c=c):
        pad_copy(c, r).start()
        return carry
      lax.fori_loop(0, pn_ref[c], start, 0)

    def tail_start(t, carry):
      tail_copy(t).start()
      return carry
    lax.fori_loop(nv_ref[0], NT_E, tail_start, 0)
    for c in range(N_CLASSES):
      def wait(r, carry, c=c):
        pad_copy(c, r).wait()
        return carry
      lax.fori_loop(0, pn_ref[c], wait, 0)

    def tail_wait(t, carry):
      tail_copy(t).wait()
      return carry
    lax.fori_loop(nv_ref[0], NT_E, tail_wait, 0)

  h = _norm_mod(x_ref[...], g_ref[...], sh_ref[...], sc_ref[...])
  pos = pos_ref[0:1, :]
  row = lax.broadcasted_iota(jnp.int32, (TS, TS), 0).astype(F32)
  perm = jnp.where(row == pos, 1.0, 0.0).astype(BF16)
  hsort = jnp.dot(perm, h.astype(BF16), preferred_element_type=F32)
  for s in range(ROW_SUB):
    stage[buf, pl.ds(s, TS, stride=ROW_SUB), :] = hsort[:, s * 128:(s + 1) * 128]

  _start_run_copies(hist_ref, lstart_ref, dst_ref, i, stage, buf, hs_hbm, sem)
  pl.when(i > 0)(lambda: _wait_run_copies(stage, 1 - buf, hs_hbm, sem))
  pl.when(i == n - 1)(lambda: _wait_run_copies(stage, buf, hs_hbm, sem))


def _scatter_rows(hist, lstart, dst, pad_start, pad_n, n_valid, x, g, shift, scale, pos):
  assert TS == TM and TS >= TMM
  grid_spec = pltpu.PrefetchScalarGridSpec(
      num_scalar_prefetch=6,
      grid=(T // TS,),
      in_specs=[
          pl.BlockSpec((TS, D), lambda i, *_: (i, 0)),
          pl.BlockSpec((1, D), lambda i, *_: (0, 0)),
          pl.BlockSpec((None, 1, D), lambda i, *_: (_seg(i, TS), 0, 0)),
          pl.BlockSpec((None, 1, D), lambda i, *_: (_seg(i, TS), 0, 0)),
          pl.BlockSpec((None, 8, TS), lambda i, *_: (i, 0, 0)),
      ],
      out_specs=pl.BlockSpec(memory_space=pl.ANY),
      scratch_shapes=[
          pltpu.VMEM((2, TS * ROW_SUB, 128), F32),
          pltpu.VMEM((ROW_SUB, 128), F32),
          pltpu.SemaphoreType.DMA((2,)),
          pltpu.SemaphoreType.DMA((2,)),
      ],
  )
  return pl.pallas_call(
      _scatter_kernel,
      grid_spec=grid_spec,
      out_shape=jax.ShapeDtypeStruct((P_PAD * ROW_SUB, 128), F32),
      compiler_params=_params(("arbitrary",)),
      name="scatter_rows",
  )(hist, lstart, dst, pad_start, pad_n, n_valid, x, g, shift, scale, pos)


def _router_logits(h, wrh_ref, wrl_ref, br_ref):
  h_hi = h.astype(BF16)
  h_lo = (h - h_hi.astype(F32)).astype(BF16)
  dot = functools.partial(jnp.dot, preferred_element_type=F32)
  return (dot(h_hi, wrh_ref[...]) + dot(h_lo, wrh_ref[...]) + dot(h_hi, wrl_ref[...])
          + br_ref[...])


def _expert_kernel(elo_ref, ehi_ref, nv_ref, hs_ref, wrh_ref, br_ref,
                   wg_lo, wg_hi, wu_lo, wu_hi, wd_lo, wd_hi, y_ref):
  i = pl.program_id(0)

  @pl.when(i < nv_ref[0])
  def _():
    h = jnp.concatenate(
        [hs_ref[pl.ds(s, TMM, stride=ROW_SUB), :] for s in range(ROW_SUB)], axis=1)
    hb = h.astype(BF16)
    logits = jnp.dot(hb, wrh_ref[...], preferred_element_type=F32) + br_ref[...]
    lane = lax.broadcasted_iota(jnp.int32, logits.shape, 1)
    elo, ehi = elo_ref[i], ehi_ref[i]

    def pick(idx):
      return jnp.sum(jnp.where(lane == idx, logits, 0.0), axis=-1, keepdims=True)
    l_g, l_lo, l_hi = pick(elo // EPG), pick(N_GROUPS + elo), pick(N_GROUPS + ehi)
    p_top = 1.0 / jnp.sum(jnp.where(lane < N_GROUPS, jnp.exp(logits - l_g), 0.0),
                          axis=-1, keepdims=True)
    wl = jnp.broadcast_to(p_top / (1.0 + jnp.exp(l_hi - l_lo)), (TMM, D_EXPERT))
    wh = jnp.broadcast_to(p_top / (1.0 + jnp.exp(l_lo - l_hi)), (TMM, D_EXPERT))
    dot = functools.partial(jnp.dot, preferred_element_type=F32)
    a_lo = (_silu(dot(hb, wg_lo[...])) * dot(hb, wu_lo[...])) * wl
    a_hi = (_silu(dot(hb, wg_hi[...])) * dot(hb, wu_hi[...])) * wh
    a_lo, a_hi = a_lo.astype(BF16), a_hi.astype(BF16)
    nb = 2 * 128
    for c in range(0, D, nb):
      y = dot(a_lo, wd_lo[:, c:c + nb]) + dot(a_hi, wd_hi[:, c:c + nb])
      for s in range(c // 128, (c + nb) // 128):
        y_ref[pl.ds(s, TMM, stride=ROW_SUB), :] = y[:, s * 128 - c:(s + 1) * 128 - c]

  @pl.when(i >= nv_ref[0])
  def _():
    y_ref[...] = jnp.zeros_like(y_ref)


def _experts(layer, tile_elo, tile_ehi, n_valid, hs, wr_hi, br, w_gate, w_up, w_down):
  up_spec_lo = pl.BlockSpec((None, None, D, D_EXPERT), lambda i, elo, ehi, nv: (layer, elo[i], 0, 0))
  up_spec_hi = pl.BlockSpec((None, None, D, D_EXPERT), lambda i, elo, ehi, nv: (layer, ehi[i], 0, 0))
  dn_spec_lo = pl.BlockSpec((None, None, D_EXPERT, D), lambda i, elo, ehi, nv: (layer, elo[i], 0, 0))
  dn_spec_hi = pl.BlockSpec((None, None, D_EXPERT, D), lambda i, elo, ehi, nv: (layer, ehi[i], 0, 0))
  grid_spec = pltpu.PrefetchScalarGridSpec(
      num_scalar_prefetch=3,
      grid=(NT_E,),
      in_specs=[
          pl.BlockSpec((TMM * ROW_SUB, 128), lambda i, elo, ehi, nv: (i, 0)),
          pl.BlockSpec((D, 128), lambda i, elo, ehi, nv: (0, 0)),
          pl.BlockSpec((1, 128), lambda i, elo, ehi, nv: (0, 0)),
          up_spec_lo, up_spec_hi, up_spec_lo, up_spec_hi, dn_spec_lo, dn_spec_hi,
      ],
      out_specs=pl.BlockSpec((TMM * ROW_SUB, 128), lambda i, elo, ehi, nv: (i, 0)),
  )
  return pl.pallas_call(
      _expert_kernel,
      grid_spec=grid_spec,
      out_shape=jax.ShapeDtypeStruct((P_PAD * ROW_SUB, 128), F32),
      compiler_params=_params(("arbitrary",)),
      name="experts",
  )(tile_elo, tile_ehi, n_valid, hs, wr_hi, br,
    w_gate, w_gate, w_up, w_up, w_down, w_down)


def _gather_copy(slot_ref, step, r, y_hbm, ybuf, buf, sem):
  return pltpu.make_async_copy(
      _tile_rows(y_hbm, _slot_of(slot_ref, step * TC + r)),
      ybuf.at[buf, pl.ds(r * ROW_SUB, ROW_SUB), :], sem.at[buf])


def _combine_kernel(slot_ref, x_ref, gate_ref, nf_ref, y_hbm, *rest, final):
  if final:
    op_ref, os_ref, ybuf, sem = rest
  else:
    o_ref, ybuf, sem = rest
  i = pl.program_id(0)
  n = pl.num_programs(0)
  buf = i % 2

  def start(step, b):
    def body(r2, carry):
      for q in range(2):
        _gather_copy(slot_ref, step, 2 * r2 + q, y_hbm, ybuf, b, sem).start(priority=q)
      return carry
    lax.fori_loop(0, TC // 2, body, 0, unroll=4)

  @pl.when(i == 0)
  def _():
    start(0, 0)

  @pl.when(i + 1 < n)
  def _():
    start(i + 1, 1 - buf)

  def wait(r, carry):
    _gather_copy(slot_ref, i, r, y_hbm, ybuf, buf, sem).wait()
    return carry
  lax.fori_loop(0, TC, wait, 0, unroll=8)

  y = jnp.concatenate(
      [ybuf[buf, pl.ds(s, TC, stride=ROW_SUB), :] for s in range(ROW_SUB)], axis=1)
  x = x_ref[...] + gate_ref[...] * y
  if not final:
    o_ref[...] = x
    return
  x = (x * lax.rsqrt(jnp.mean(x * x, axis=-1, keepdims=True) + EPS)) * nf_ref[...]
  is_prompt = i < T_P // TC

  @pl.when(is_prompt)
  def _():
    op_ref[...] = x

  @pl.when(jnp.logical_not(is_prompt))
  def _():
    os_ref[...] = x


def _combine(slot, x, gate, norm_f, y_sorted, final):
  npt = T_P // TC
  if final:
    out_specs = (pl.BlockSpec((TC, D), lambda i, s: (jnp.minimum(i, npt - 1), 0)),
                 pl.BlockSpec((TC, D), lambda i, s: (jnp.maximum(i - npt, 0), 0)))
    out_shape = (jax.ShapeDtypeStruct((T_P, D), F32), jax.ShapeDtypeStruct((T_S, D), F32))
  else:
    out_specs = pl.BlockSpec((TC, D), lambda i, s: (i, 0))
    out_shape = jax.ShapeDtypeStruct((T, D), F32)
  grid_spec = pltpu.PrefetchScalarGridSpec(
      num_scalar_prefetch=1,
      grid=(T // TC,),
      in_specs=[
          pl.BlockSpec((TC, D), lambda i, s: (i, 0)),
          pl.BlockSpec((None, 1, D), lambda i, s: (_seg(i, TC), 0, 0)),
          pl.BlockSpec((1, D), lambda i, s: (0, 0)),
          pl.BlockSpec(memory_space=pl.ANY),
      ],
      out_specs=out_specs,
      scratch_shapes=[pltpu.VMEM((2, TC * ROW_SUB, 128), F32), pltpu.SemaphoreType.DMA((2,))],
  )
  return pl.pallas_call(
      functools.partial(_combine_kernel, final=final),
      grid_spec=grid_spec,
      out_shape=out_shape,
      compiler_params=_params(("arbitrary",)),
      name="combine_final" if final else "combine",
  )(slot, x, gate, norm_f, y_sorted)


def _class_experts():
  lo, hi = [], []
  for g in range(N_GROUPS):
    for a in range(EPG):
      for b in range(a + 1, EPG):
        lo.append(g * EPG + a)
        hi.append(g * EPG + b)
  return np.asarray(lo, np.int32), np.asarray(hi, np.int32)


def _moe_layer(layer, x, routing, route_args, gate, w_gate, w_up, w_down, norm_f):
  final = layer == DEPTH - 1
  g, shift, scale, wr_hi, _, br, _ = route_args
  info, cnt, before_tiles = routing

  cls = info[:, 0, :].astype(jnp.int32)
  rank = info[:, 1, :].astype(jnp.int32)
  counts = cnt[:N_CLASSES, 0].astype(jnp.int32)
  tiles = (counts + TMM - 1) // TMM
  tile_end = jnp.cumsum(tiles)
  offs = (tile_end - tiles) * TMM
  before = before_tiles[:, :N_CLASSES, 0].astype(jnp.int32)
  hist = jnp.concatenate([before[1:], counts[None, :]], axis=0) - before
  lstart = jnp.cumsum(hist, axis=1) - hist
  dst = offs[None, :] + before
  slot = rank
  pos = rank
  for k in range(N_CLASSES):
    hit = cls == k
    slot = slot + jnp.where(hit, offs[k], 0)
    pos = pos + jnp.where(hit, (lstart[:, k] - before[:, k])[:, None], 0)
  slot = slot.reshape(T)
  pos = jnp.broadcast_to(pos.astype(F32)[:, None, :], (T // TM, 8, TM))
  n_valid = tile_end[-1]
  tile_ids = jnp.minimum(jnp.arange(NT_E, dtype=jnp.int32), n_valid - 1)
  tile_cls = jnp.sum((tile_ids[:, None] >= tile_end[None, :]).astype(jnp.int32), axis=1)
  tile_cls = jnp.minimum(tile_cls, N_CLASSES - 1)
  cls_lo, cls_hi = _class_experts()
  tile_elo = jnp.asarray(cls_lo)[tile_cls]
  tile_ehi = jnp.asarray(cls_hi)[tile_cls]

  n_valid = n_valid.reshape(1)
  hs = _scatter_rows(hist.reshape(-1), lstart.reshape(-1), dst.reshape(-1),
                     offs + counts, tiles * TMM - counts, n_valid, x, g, shift, scale, pos)
  y_sorted = _experts(layer, tile_elo, tile_ehi, n_valid, hs, wr_hi, br, w_gate, w_up, w_down)
  return _combine(slot, x, gate, norm_f, y_sorted, final)


def kernel(x_prompt, x_sample, cache_k, cache_v, state_ret, c, c_ctx, norm1, norm2, w_ada, b_ada, attn_w_qkv, attn_q_norm, attn_k_norm, attn_w_o, pool_w, pool_scale, ret_w_in, ret_decay_logit, ret_norm, ret_w_out, moe_w_router_g, moe_b_router_g, moe_w_router_e, moe_b_router_e, moe_w_gate, moe_w_up, moe_w_down, norm_f):
  x = (x_prompt.reshape(T_P, D), x_sample.reshape(T_S, D))
  cond8 = jnp.concatenate([c_ctx[None, :], c, jnp.zeros((N_SEG - 1 - NB_S, D), F32)], axis=0)
  mods = _adaln(cond8, w_ada, b_ada)
  mods = mods.reshape(DEPTH, N_SEG, 6, 1, D).transpose(0, 2, 1, 3, 4)
  rope = _rope_tables()
  tri = jnp.triu(jnp.ones((TM, TM), BF16), 1)
  pad_r = 128 - N_GROUPS - N_EXPERTS
  norm_f2 = norm_f.reshape(1, D)

  w_gate, w_up, w_down = moe_w_gate.astype(BF16), moe_w_up.astype(BF16), moe_w_down.astype(BF16)

  new_k, new_v, new_s = [], [], []
  for i in range(DEPTH):
    kind, j = i % 3, i // 3
    m = mods[i]
    g1 = norm1[i].reshape(1, D)
    wr = jnp.concatenate([moe_w_router_g[i], moe_w_router_e[i], jnp.zeros((D, pad_r), F32)], axis=1)
    br = jnp.concatenate([moe_b_router_g[i], moe_b_router_e[i], jnp.zeros((pad_r,), F32)]).reshape(1, 128)
    wr_hi = wr.astype(BF16)
    wr_lo = (wr - wr_hi.astype(F32)).astype(BF16)
    route_args = (norm2[i].reshape(1, D), m[3], m[4], wr_hi, wr_lo, br, tri)
    if kind == 0:
      qkv = _nm_matmul(x, g1, m[0], m[1], attn_w_qkv[j].astype(BF16), TM, "qkv_proj")
      qn = attn_q_norm[j].reshape(1, HEAD_DIM)
      kn = attn_k_norm[j].reshape(1, HEAD_DIM)
      o_p, kc, vc = _attn_prompt(qkv, qn, kn)
      ck = cache_k[:, j].reshape(NB_S, PAST, N_KV * HEAD_DIM)
      cv = cache_v[:, j].reshape(NB_S, PAST, N_KV * HEAD_DIM)
      o_s = _attn_latent(qkv, ck, cv, qn, kn, rope)
      new_k.append(kc.reshape(NB_P, SEQ_P, N_KV, HEAD_DIM))
      new_v.append(vc.reshape(NB_P, SEQ_P, N_KV, HEAD_DIM))
      x, *routing = _mm_res_route((o_p, o_s), attn_w_o[j].astype(BF16), x, m[2], route_args, "attn_out")
    elif kind == 1:
      x = _pool_layer(x, g1, m[0], m[1], m[2], pool_w[j].astype(BF16), pool_scale[j].reshape(1, D))
      routing = _router(x, route_args)
    else:
      p = _nm_matmul(x, g1, m[0], m[1], ret_w_in[j].astype(BF16), 256, "ret_proj")
      dl = jnp.broadcast_to(ret_decay_logit[j].T[:, :, None, None], (RET_HEADS, 2, 1, RET_DK))
      ng = ret_norm[j].reshape(1, RET_HEADS * RET_DV)
      y_p, s_new = _retention(p, dl, ng, None, True)
      y_s = _retention(p, dl, ng, state_ret[:, j], False)
      new_s.append(s_new)
      x, *routing = _mm_res_route((y_p, y_s), ret_w_out[j].astype(BF16), x, m[2], route_args, "ret_out")
    x = _moe_layer(i, x, routing, route_args, m[5], w_gate, w_up, w_down, norm_f2)

  y_prompt = x[0].reshape(NB_P, SEQ_P, D)
  y_sample = x[1].reshape(NB_S, SEQ_S, D)
  new_cache_k = jnp.stack(new_k, axis=1)
  new_cache_v = jnp.stack(new_v, axis=1)
  assert len(new_s) == 1
  new_state_ret = new_s[0].reshape(NB_P, 1, 2, RET_HEADS, RET_DK, RET_DV)
  return (y_prompt, y_sample, new_cache_k, new_cache_v, new_state_ret)
```

```python
import functools

import jax
import jax.numpy as jnp
import numpy as np
from jax import lax
from jax.experimental import pallas as pl
from jax.experimental.pallas import tpu as pltpu

F32 = jnp.float32
BF16 = jnp.bfloat16

D = 1024
NB_P, SEQ_P = 32, 256
NB_S, SEQ_S = 4, 2048
T_P = NB_P * SEQ_P
T_S = NB_S * SEQ_S
T = T_P + T_S
DEPTH = 4
GRID_W = 64
HEAD_DIM = 128
N_HEADS = 8
N_KV = 2
KV_GROUP = N_HEADS // N_KV
PAST = 256
ROPE_THETA = 10000.0
POOL_WINDOWS = (2, 4, 8, 16)
POOL_GROUP = D // 4
POOL_HALO = 8
RET_HEADS = 8
RET_DK = 128
RET_DV = 256
RET_CHUNK = 128
RET_HEADS_PER_STEP_PROMPT = 4
RET_HEADS_PER_STEP_LATENT = 2
N_GROUPS = 4
EPG = 4
N_EXPERTS = 16
D_EXPERT = 256
N_PAIRS = 6
N_CLASSES = N_GROUPS * N_PAIRS
ROUTER_ROWS = 32
EPS = 1e-6
LOG2E = 1.4426950408889634
NEG = -1e30
N_SEG = 8

VMEM_LIMIT_BYTES = 52 * 1024 * 1024

TM = 512
TQ = 1024
ATTN_UNIT_Q = 128
ATTN_KEY_CHUNK = 256
ATTN_HEAD_STACK = 2
TP = 256
TMM = 256
TS = 512
TC = 256
ROW_SUB = 8
PACK_SUB = D // 2 // 128
INFO_SUB = PACK_SUB
P_PAD = T + N_CLASSES * TMM
NT_E = P_PAD // TMM


def _params(sem):
  return pltpu.CompilerParams(dimension_semantics=sem, vmem_limit_bytes=VMEM_LIMIT_BYTES)


def _seg(i, tm):
  npt = T_P // tm
  return jnp.where(i < npt, 0, (i - npt) // (SEQ_S // tm) + 1)


def _norm_mod(x, g, shift, scale):
  r = lax.rsqrt(jnp.mean(x * x, axis=-1, keepdims=True) + EPS)
  return ((x * r) * g) * (1.0 + scale) + shift


def _silu(x):
  return x * (1.0 / (1.0 + jnp.exp(-x)))


def _adaln_kernel(c_ref, w_ref, b_ref, o_ref):
  s = _silu(c_ref[...]).astype(BF16)
  o_ref[...] = jnp.dot(s, w_ref[...].astype(BF16), preferred_element_type=F32) + b_ref[...]


def _adaln(cond8, w_ada, b_ada):
  tn = 1536
  return pl.pallas_call(
      _adaln_kernel,
      grid=(DEPTH, 6 * D // tn),
      in_specs=[
          pl.BlockSpec((N_SEG, D), lambda l, j: (0, 0)),
          pl.BlockSpec((None, D, tn), lambda l, j: (l, 0, j)),
          pl.BlockSpec((None, 1, tn), lambda l, j: (l, 0, j)),
      ],
      out_specs=pl.BlockSpec((None, N_SEG, tn), lambda l, j: (l, 0, j)),
      out_shape=jax.ShapeDtypeStruct((DEPTH, N_SEG, 6 * D), F32),
      compiler_params=_params(("arbitrary", "arbitrary")),
      name="adaln",
  )(cond8, w_ada, b_ada.reshape(DEPTH, 1, 6 * D))


def _token_rows(x, tm, width):
  npt = T_P // tm
  if isinstance(x, tuple):
    arrays = x
    latent_map = lambda i: (jnp.maximum(i - npt, 0), 0)
  else:
    arrays = (x, x)
    latent_map = lambda i: (jnp.maximum(i, npt), 0)
  specs = [pl.BlockSpec((tm, width), lambda i: (jnp.minimum(i, npt - 1), 0)),
           pl.BlockSpec((tm, width), latent_map)]
  return arrays, specs


def _pick_rows(p_ref, s_ref, tm):
  return jnp.where(pl.program_id(0) < T_P // tm, p_ref[...], s_ref[...])


def _nm_matmul_kernel(xp_ref, xs_ref, g_ref, sh_ref, sc_ref, w_ref, o_ref, *, tm, n_chunk):
  x = _pick_rows(xp_ref, xs_ref, tm)
  h = _norm_mod(x, g_ref[...], sh_ref[...], sc_ref[...]).astype(BF16)
  n = w_ref.shape[1]
  for c in range(0, n, n_chunk):
    o_ref[:, c:c + n_chunk] = jnp.dot(
        h, w_ref[:, c:c + n_chunk], preferred_element_type=F32).astype(o_ref.dtype)


def _nm_matmul(x, g, shift, scale, w, tm, name):
  n = w.shape[1]
  x_arrays, x_specs = _token_rows(x, tm, D)
  return pl.pallas_call(
      functools.partial(_nm_matmul_kernel, tm=tm, n_chunk=512),
      grid=(T // tm,),
      in_specs=x_specs + [
          pl.BlockSpec((1, D), lambda i: (0, 0)),
          pl.BlockSpec((None, 1, D), lambda i: (_seg(i, tm), 0, 0)),
          pl.BlockSpec((None, 1, D), lambda i: (_seg(i, tm), 0, 0)),
          pl.BlockSpec((D, n), lambda i: (0, 0)),
      ],
      out_specs=pl.BlockSpec((tm, n), lambda i: (i, 0)),
      out_shape=jax.ShapeDtypeStruct((T, n), BF16),
      compiler_params=_params(("arbitrary",)),
      name=name,
  )(*x_arrays, g, shift, scale, w)


def _mm_res_route_kernel(ap_ref, as_ref, xp_ref, xs_ref, w_ref, gate_ref, *rest):
  n_route_in = len(_ROUTE_IN_SPECS)
  route_in, (o_ref, *route_out) = rest[:n_route_in], rest[n_route_in:]
  a = _pick_rows(ap_ref, as_ref, TM)
  x = _pick_rows(xp_ref, xs_ref, TM)
  x_new = x + gate_ref[...] * jnp.dot(a, w_ref[...], preferred_element_type=F32)
  o_ref[...] = x_new
  _route_tile(x_new, *route_in, *route_out)


def _mm_res_route(a, w, x, gate, route_args, name):
  k = w.shape[0]
  a_arrays, a_specs = _token_rows(a, TM, k)
  x_arrays, x_specs = _token_rows(x, TM, D)
  return pl.pallas_call(
      _mm_res_route_kernel,
      grid=(T // TM,),
      in_specs=a_specs + x_specs + [
          pl.BlockSpec((k, D), lambda i: (0, 0)),
          pl.BlockSpec((None, 1, D), lambda i: (_seg(i, TM), 0, 0)),
      ] + _ROUTE_IN_SPECS,
      out_specs=(pl.BlockSpec((TM, D), lambda i: (i, 0)),) + _ROUTE_OUT_SPECS,
      out_shape=(jax.ShapeDtypeStruct((T, D), F32),) + _ROUTE_OUT_SHAPE,
      scratch_shapes=_ROUTE_SCRATCH,
      compiler_params=_params(("arbitrary",)),
      name=name,
  )(*a_arrays, *x_arrays, w, gate, *route_args)


def _rope(x, c, a, b):
  return x * c + pltpu.roll(x, 96, 1) * a + pltpu.roll(x, 32, 1) * b


def _head_norm(x, w):
  return (x * lax.rsqrt(jnp.mean(x * x, axis=-1, keepdims=True) + EPS)) * w


def _attn_latent_kernel(q_ref, kn_ref, vn_ref, ck_ref, cv_ref, qw_ref, kw_ref,
                        cq_ref, aq_ref, bq_ref, ckk_ref, akk_ref, bkk_ref,
                        o_ref, k_s, vt_s, *st_refs, tq):
  @pl.when(pl.program_id(2) == 0)
  def _():
    k = _rope(_head_norm(kn_ref[...].astype(F32), kw_ref[...]), ckk_ref[...], akk_ref[...], bkk_ref[...])
    k_s[0:PAST, :] = ck_ref[...].astype(BF16)
    k_s[PAST:, :] = k.astype(BF16)
    vt_s[:, 0:PAST] = cv_ref[...].T.astype(BF16)
    vt_s[:, PAST:] = vn_ref[...].astype(F32).T.astype(BF16)

  qa = q_ref[...].astype(F32)
  lk = k_s.shape[0]

  def prep_q(h):
    qh = _head_norm(qa[:, h * HEAD_DIM:(h + 1) * HEAD_DIM], qw_ref[...])
    qh = _rope(qh, cq_ref[...], aq_ref[...], bq_ref[...])
    return (qh * (HEAD_DIM ** -0.5 * LOG2E)).astype(BF16)

  nq = ATTN_HEAD_STACK * ATTN_UNIT_Q
  chunks = [(c0, min(c0 + ATTN_KEY_CHUNK, lk)) for c0 in range(0, lk, ATTN_KEY_CHUNK)]
  qh_all = [prep_q(h) for h in range(KV_GROUP)]
  units = [(h0, r0) for r0 in range(0, tq, ATTN_UNIT_Q) for h0 in range(0, KV_GROUP, ATTN_HEAD_STACK)]

  def score_chunk(u, c0, c1, m):
    h0, r0 = units[u]
    qp = jnp.concatenate([qh_all[h0 + d][r0:r0 + ATTN_UNIT_Q] for d in range(ATTN_HEAD_STACK)], axis=0)
    st = lax.dot_general(k_s[c0:c1, :], qp, (((1,), (1,)), ((), ())),
                         preferred_element_type=F32)
    st_refs[u % 2][c0:c1, :] = st
    return jnp.maximum(m, jnp.max(st, axis=0, keepdims=True))

  def value_chunk(u, c0, c1, m, l, acc):
    pt = jnp.exp2(st_refs[u % 2][c0:c1, :] - m)
    return (l + jnp.sum(pt, axis=0, keepdims=True),
            acc + jnp.dot(vt_s[:, c0:c1], pt.astype(BF16), preferred_element_type=F32))

  def finish(u, l, acc):
    h0, r0 = units[u]
    o = acc / l
    for d in range(ATTN_HEAD_STACK):
      h = h0 + d
      o_ref[r0:r0 + ATTN_UNIT_Q, h * HEAD_DIM:(h + 1) * HEAD_DIM] = (
          o[:, d * ATTN_UNIT_Q:(d + 1) * ATTN_UNIT_Q].T.astype(BF16))

  m_prev = None
  for u in range(len(units) + 1):
    m = jnp.full((1, nq), NEG, F32)
    l = jnp.zeros((1, nq), F32)
    acc = jnp.zeros((HEAD_DIM, nq), F32)
    for c0, c1 in chunks:
      if u < len(units):
        m = score_chunk(u, c0, c1, m)
      if u > 0:
        l, acc = value_chunk(u - 1, c0, c1, m_prev, l, acc)
    if u > 0:
      finish(u - 1, l, acc)
    m_prev = m


def _attn_scratch(lk):
  return ([pltpu.VMEM((lk, HEAD_DIM), BF16), pltpu.VMEM((HEAD_DIM, lk), BF16)]
          + [pltpu.VMEM((lk, ATTN_HEAD_STACK * ATTN_UNIT_Q), F32) for _ in range(2)])


def _rope_tables():
  rows = SEQ_S // GRID_W
  t_row = jnp.broadcast_to(jnp.arange(rows)[:, None], (rows, GRID_W)).reshape(-1)
  t_col = jnp.broadcast_to(jnp.arange(GRID_W)[None, :], (rows, GRID_W)).reshape(-1)
  nf = HEAD_DIM // 4
  inv = ROPE_THETA ** (-jnp.arange(nf, dtype=F32) / nf)
  ang_r = t_row.astype(F32)[:, None] * inv[None, :]
  ang_c = t_col.astype(F32)[:, None] * inv[None, :]
  cr, sr, cc, sc = jnp.cos(ang_r), jnp.sin(ang_r), jnp.cos(ang_c), jnp.sin(ang_c)
  z = jnp.zeros_like(sr)
  c = jnp.concatenate([cr, cr, cc, cc], axis=1)
  a = jnp.concatenate([-sr, z, -sc, z], axis=1)
  b = jnp.concatenate([z, sr, z, sc], axis=1)
  return c, a, b


def _attn_prompt_kernel(qkv_ref, qw_ref, kw_ref, o_ref, ko_ref, vo_ref):
  nq = N_HEADS * HEAD_DIM
  nk = N_KV * HEAD_DIM
  for kh in range(N_KV):
    kcols = slice(nq + kh * HEAD_DIM, nq + (kh + 1) * HEAD_DIM)
    vcols = slice(nq + nk + kh * HEAD_DIM, nq + nk + (kh + 1) * HEAD_DIM)
    k = _head_norm(qkv_ref[:, kcols].astype(F32), kw_ref[...])
    v = qkv_ref[:, vcols].astype(F32)
    ko_ref[pl.ds(kh, SEQ_P, stride=N_KV), :] = k
    vo_ref[pl.ds(kh, SEQ_P, stride=N_KV), :] = v
    kb = k.astype(BF16)
    vt = v.T.astype(BF16)
    for h in range(kh * KV_GROUP, (kh + 1) * KV_GROUP):
      hcols = slice(h * HEAD_DIM, (h + 1) * HEAD_DIM)
      qh = _head_norm(qkv_ref[:, hcols].astype(F32), qw_ref[...])
      qh = (qh * (HEAD_DIM ** -0.5 * LOG2E)).astype(BF16)
      st = lax.dot_general(kb, qh, (((1,), (1,)), ((), ())), preferred_element_type=F32)
      pt = jnp.exp2(st - jnp.max(st, axis=0, keepdims=True))
      l = jnp.sum(pt, axis=0, keepdims=True)
      acc = jnp.dot(vt, pt.astype(BF16), preferred_element_type=F32)
      o_ref[:, hcols] = (acc / l).T.astype(BF16)


def _attn_prompt(qkv, q_norm, k_norm):
  width = (N_HEADS + 2 * N_KV) * HEAD_DIM
  out_shapes = (
      jax.ShapeDtypeStruct((T_P, N_HEADS * HEAD_DIM), BF16),
      jax.ShapeDtypeStruct((NB_P, SEQ_P * N_KV, HEAD_DIM), F32),
      jax.ShapeDtypeStruct((NB_P, SEQ_P * N_KV, HEAD_DIM), F32),
  )
  return pl.pallas_call(
      _attn_prompt_kernel,
      grid=(NB_P,),
      in_specs=[
          pl.BlockSpec((SEQ_P, width), lambda b: (b, 0)),
          pl.BlockSpec((1, HEAD_DIM), lambda b: (0, 0)),
          pl.BlockSpec((1, HEAD_DIM), lambda b: (0, 0)),
      ],
      out_specs=(
          pl.BlockSpec((SEQ_P, N_HEADS * HEAD_DIM), lambda b: (b, 0)),
          pl.BlockSpec((None, SEQ_P * N_KV, HEAD_DIM), lambda b: (b, 0, 0)),
          pl.BlockSpec((None, SEQ_P * N_KV, HEAD_DIM), lambda b: (b, 0, 0)),
      ),
      out_shape=out_shapes,
      compiler_params=_params(("arbitrary",)),
      name="attn_prompt",
  )(qkv, q_norm, k_norm)


def _attn_latent(qkv, cache_k, cache_v, q_norm, k_norm, rope):
  kcol = N_HEADS
  vcol = kcol + N_KV
  nq = SEQ_S // TQ
  row0 = T_P // TQ
  seq0 = T_P // SEQ_S
  lk = PAST + SEQ_S
  c, a, b = rope
  tab_q = pl.BlockSpec((TQ, HEAD_DIM), lambda bb, h, i: (i, 0))
  tab_k = pl.BlockSpec((SEQ_S, HEAD_DIM), lambda bb, h, i: (0, 0))
  return pl.pallas_call(
      functools.partial(_attn_latent_kernel, tq=TQ),
      grid=(NB_S, N_KV, nq),
      in_specs=[
          pl.BlockSpec((TQ, KV_GROUP * HEAD_DIM), lambda bb, h, i: (row0 + bb * nq + i, h)),
          pl.BlockSpec((SEQ_S, HEAD_DIM), lambda bb, h, i: (seq0 + bb, kcol + h)),
          pl.BlockSpec((SEQ_S, HEAD_DIM), lambda bb, h, i: (seq0 + bb, vcol + h)),
          pl.BlockSpec((None, PAST, HEAD_DIM), lambda bb, h, i: (bb, 0, h)),
          pl.BlockSpec((None, PAST, HEAD_DIM), lambda bb, h, i: (bb, 0, h)),
          pl.BlockSpec((1, HEAD_DIM), lambda bb, h, i: (0, 0)),
          pl.BlockSpec((1, HEAD_DIM), lambda bb, h, i: (0, 0)),
          tab_q, tab_q, tab_q, tab_k, tab_k, tab_k,
      ],
      out_specs=pl.BlockSpec((TQ, KV_GROUP * HEAD_DIM), lambda bb, h, i: (bb * nq + i, h)),
      out_shape=jax.ShapeDtypeStruct((T_S, N_HEADS * HEAD_DIM), BF16),
      scratch_shapes=_attn_scratch(lk),
      compiler_params=_params(("arbitrary", "arbitrary", "arbitrary")),
      name="attn_latent",
  )(qkv, qkv, qkv, cache_k, cache_v, q_norm, k_norm, c, a, b, c, a, b)


def _pool_kernel(x_ref, xp_ref, xn_ref, g_ref, sh_ref, sc_ref, gate_ref, w_ref, ps_ref, o_ref):
  t = pl.program_id(0)
  npt = T_P // TP
  tiles_s = SEQ_S // TP
  is_p = t < npt
  pos = jnp.where(is_p, 0, (t - npt) % tiles_s)
  ntile = jnp.where(is_p, SEQ_P // TP, tiles_s)
  seq_len = ntile * TP
  keep_prev = jnp.where(pos == 0, 0.0, 1.0)
  keep_next = jnp.where(pos == ntile - 1, 0.0, 1.0)

  g, sh, sc = g_ref[...], sh_ref[...], sc_ref[...]
  x = x_ref[...]
  h = _norm_mod(x, g, sh, sc)
  hp = _norm_mod(xp_ref[...], g, sh, sc) * keep_prev
  hn = _norm_mod(xn_ref[...], g, sh, sc) * keep_next
  ext = jnp.concatenate([hp, h, hn], axis=0)
  n_ext = TP + 2 * POOL_HALO
  tseq = pos * TP + lax.broadcasted_iota(jnp.int32, (TP, 128), 0)

  outs = []
  for gi, win in enumerate(POOL_WINDOWS):
    lo, hi = gi * POOL_GROUP, (gi + 1) * POOL_GROUP
    acc = ext[:, lo:hi]
    span = 1
    while span < win:
      acc = acc + pltpu.roll(acc, n_ext - span, 0)
      span *= 2
    start = POOL_HALO - win // 2
    if start:
      acc = pltpu.roll(acc, n_ext - start, 0)
    ssum = acc[0:TP]
    cnt = (jnp.minimum(tseq + win // 2, seq_len) - jnp.maximum(tseq - win // 2, 0)).astype(F32)
    cnt = jnp.concatenate([cnt] * (POOL_GROUP // 128), axis=1)
    dlt = (ssum / cnt - h[:, lo:hi]).astype(BF16)
    outs.append(jnp.dot(dlt, w_ref[gi], preferred_element_type=F32))
  y = jnp.concatenate(outs, axis=1) * ps_ref[...]
  o_ref[...] = x + gate_ref[...] * y


def _pool_layer(x, g, shift, scale, gate, w, pscale):
  hb = TP // POOL_HALO
  last = T // POOL_HALO - 1
  seg = lambda i: _seg(i, TP)
  return pl.pallas_call(
      _pool_kernel,
      grid=(T // TP,),
      in_specs=[
          pl.BlockSpec((TP, D), lambda i: (i, 0)),
          pl.BlockSpec((POOL_HALO, D), lambda i: (jnp.maximum(i * hb - 1, 0), 0)),
          pl.BlockSpec((POOL_HALO, D), lambda i: (jnp.minimum((i + 1) * hb, last), 0)),
          pl.BlockSpec((1, D), lambda i: (0, 0)),
          pl.BlockSpec((None, 1, D), lambda i: (seg(i), 0, 0)),
          pl.BlockSpec((None, 1, D), lambda i: (seg(i), 0, 0)),
          pl.BlockSpec((None, 1, D), lambda i: (seg(i), 0, 0)),
          pl.BlockSpec((4, POOL_GROUP, POOL_GROUP), lambda i: (0, 0, 0)),
          pl.BlockSpec((1, D), lambda i: (0, 0)),
      ],
      out_specs=pl.BlockSpec((TP, D), lambda i: (i, 0)),
      out_shape=jax.ShapeDtypeStruct((T, D), F32),
      compiler_params=_params(("arbitrary",)),
      name="pool",
  )(x, x, x, g, shift, scale, gate, w, pscale)


def _ret_kernel(*refs, seq_len, has_s0, nh):
  it = iter(refs)
  q_ref, k_ref, v_ref, gate_ref, dl_ref, ng_ref = [next(it) for _ in range(6)]
  s0_ref = next(it) if has_s0 else None
  y_ref = next(it)
  so_ref = None if has_s0 else next(it)
  u_s = next(it)
  dec_s, cdec_s = next(it), next(it)
  c = RET_CHUNK
  n = seq_len // c
  kscale = RET_DK ** -0.5
  nt = (((1,), (1,)), ((), ()))
  tn = (((0,), (0,)), ((), ()))

  @pl.when(pl.program_id(1) == 0)
  def _():
    ri = lax.broadcasted_iota(jnp.int32, (c, c), 0).astype(F32)
    ci = lax.broadcasted_iota(jnp.int32, (c, c), 1).astype(F32)
    ri2 = lax.broadcasted_iota(jnp.int32, (c, RET_DV), 0).astype(F32)
    for hh in range(nh):
      lg = -jnp.log1p(jnp.exp(-dl_ref[hh]))
      lgf, lgb = lg[0], lg[1]
      lgf2 = jnp.concatenate([lgf, lgf], axis=1)
      lgb2 = jnp.concatenate([lgb, lgb], axis=1)
      dec_s[hh, 0] = jnp.concatenate(
          [jnp.exp((c - 1.0 - ri) * lgf), jnp.exp(ri * lgb)], axis=1) * kscale
      intra = (jnp.where(ri >= ci, jnp.exp((ri - ci) * lgf), 0.0)
               + jnp.where(ri <= ci, jnp.exp((ci - ri) * lgb), 0.0)) * kscale
      dec_s[hh, 1] = jnp.concatenate([intra, intra], axis=1)
      dec_s[hh, 2] = jnp.exp((ri2 + 1.0) * lgf2)
      dec_s[hh, 3] = jnp.exp((c - ri2) * lgb2)
      cdec_s[hh, 0] = jnp.exp(c * lgf2)
      cdec_s[hh, 1] = jnp.exp(c * lgb2)

  for hh in range(nh):
    qcols = slice(hh * RET_DK, (hh + 1) * RET_DK)
    vcols = slice(hh * RET_DV, (hh + 1) * RET_DV)
    kd2 = dec_s[hh, 0]

    for j in range(n):
      rows = slice(j * c, (j + 1) * c)
      kc = k_ref[rows, qcols].astype(F32)
      k2 = (jnp.concatenate([kc, kc], axis=1) * kd2).astype(BF16)
      u = lax.dot_general(k2, v_ref[rows, vcols], tn, preferred_element_type=F32)
      u_s[0, hh, j] = u[0:RET_DK]
      u_s[1, hh, j] = u[RET_DK:]

    for d, order in ((0, range(n)), (1, reversed(range(n)))):
      cdec = cdec_s[hh, d]
      st = s0_ref[d, hh] if has_s0 else jnp.zeros((RET_DK, RET_DV), F32)
      for j in order:
        u = u_s[d, hh, j]
        u_s[d, hh, j] = st
        st = st * cdec + u
      if not has_s0:
        so_ref[d, hh] = st

    intra = dec_s[hh, 1][:, :c]
    qd_f = dec_s[hh, 2]
    qd_b = dec_s[hh, 3]
    for j in range(n):
      rows = slice(j * c, (j + 1) * c)
      qc = q_ref[rows, qcols]
      vc = v_ref[rows, vcols]
      s = lax.dot_general(qc, k_ref[rows, qcols], nt, preferred_element_type=F32) * intra
      st2 = jnp.concatenate([u_s[0, hh, j], u_s[1, hh, j]], axis=1).astype(BF16)
      inter = jnp.dot(qc, st2, preferred_element_type=F32)
      o = (jnp.dot(s.astype(BF16), vc, preferred_element_type=F32)
           + inter[:, :RET_DV] * qd_f + inter[:, RET_DV:] * qd_b)
      mu = jnp.mean(o, axis=-1, keepdims=True)
      dv = o - mu
      var = jnp.mean(dv * dv, axis=-1, keepdims=True)
      on = (dv * lax.rsqrt(var + EPS)) * ng_ref[:, vcols]
      y_ref[rows, vcols] = (on * _silu(gate_ref[rows, vcols].astype(F32))).astype(BF16)


def _retention(p, dl, ng, state0, prompt):
  seq_len = SEQ_P if prompt else SEQ_S
  nb = NB_P if prompt else NB_S
  nh = RET_HEADS_PER_STEP_PROMPT if prompt else RET_HEADS_PER_STEP_LATENT
  row0 = 0 if prompt else T_P // SEQ_S
  hb = RET_HEADS // nh
  kcol = hb
  vcol = (2 * RET_HEADS * RET_DK) // (nh * RET_DV)
  gcol = vcol + hb
  in_specs = [
      pl.BlockSpec((seq_len, nh * RET_DK), lambda h, b: (row0 + b, h)),
      pl.BlockSpec((seq_len, nh * RET_DK), lambda h, b: (row0 + b, kcol + h)),
      pl.BlockSpec((seq_len, nh * RET_DV), lambda h, b: (row0 + b, vcol + h)),
      pl.BlockSpec((seq_len, nh * RET_DV), lambda h, b: (row0 + b, gcol + h)),
      pl.BlockSpec((nh, 2, 1, RET_DK), lambda h, b: (h, 0, 0, 0)),
      pl.BlockSpec((1, nh * RET_DV), lambda h, b: (0, h)),
  ]
  args = [p, p, p, p, dl, ng]
  y_spec = pl.BlockSpec((seq_len, nh * RET_DV), lambda h, b: (b, h))
  y_shape = jax.ShapeDtypeStruct((nb * seq_len, RET_HEADS * RET_DV), BF16)
  state_spec = pl.BlockSpec((None, 2, nh, RET_DK, RET_DV), lambda h, b: (b, 0, h, 0, 0))
  if prompt:
    out_specs = (y_spec, state_spec)
    out_shape = (y_shape, jax.ShapeDtypeStruct((NB_P, 2, RET_HEADS, RET_DK, RET_DV), F32))
  else:
    in_specs.append(state_spec)
    args.append(state0)
    out_specs = y_spec
    out_shape = y_shape
  return pl.pallas_call(
      functools.partial(_ret_kernel, seq_len=seq_len, has_s0=not prompt, nh=nh),
      grid=(hb, nb),
      in_specs=in_specs,
      out_specs=out_specs,
      out_shape=out_shape,
      scratch_shapes=[
          pltpu.VMEM((2, nh, seq_len // RET_CHUNK, RET_DK, RET_DV), F32),
          pltpu.VMEM((nh, 4, RET_CHUNK, RET_DV), F32),
          pltpu.VMEM((nh, 2, 1, RET_DV), F32),
      ],
      compiler_params=_params(("arbitrary", "arbitrary")),
      name="ret_prompt" if prompt else "ret_latent",
  )(*args)


def _router_kernel(x_ref, *route_refs):
  _route_tile(x_ref[...], *route_refs)


def _route_tile(x, g_ref, sh_ref, sc_ref, wrh_ref, wrl_ref, br_ref, triu_ref,
                info_ref, cnt_ref, before_ref, carry_s):
  @pl.when(pl.program_id(0) == 0)
  def _():
    carry_s[...] = jnp.zeros_like(carry_s)

  h = _norm_mod(x, g_ref[...], sh_ref[...], sc_ref[...])
  logits = _router_logits(h, wrh_ref, wrl_ref, br_ref)
  lt = logits.T[0:ROUTER_ROWS]
  row = lax.broadcasted_iota(jnp.int32, lt.shape, 0)
  big = jnp.int32(ROUTER_ROWS)

  def first_max(v):
    m = jnp.max(v, axis=0, keepdims=True)
    return jnp.min(jnp.where(v == m, row, big), axis=0, keepdims=True)

  gidx = first_max(jnp.where(row < N_GROUPS, lt, NEG))
  lo = N_GROUPS + EPG * gidx
  le = jnp.where((row >= lo) & (row < lo + EPG), lt, NEG)
  i1 = first_max(le)
  i2 = first_max(jnp.where(row == i1, NEG, le))
  e_lo = jnp.minimum(i1, i2) - N_GROUPS
  e_hi = jnp.maximum(i1, i2) - N_GROUPS
  a = e_lo - EPG * gidx
  b = e_hi - EPG * gidx
  pair_base = jnp.where(a == 0, 0, jnp.where(a == 1, 3, 5))
  cls = N_PAIRS * gidx + pair_base + (b - a - 1)

  onehot = jnp.where(row == cls, 1.0, 0.0)
  before = jnp.dot(onehot.astype(BF16), triu_ref[...], preferred_element_type=F32) + carry_s[...]
  rank = jnp.sum(jnp.where(row == cls, before, 0.0), axis=0, keepdims=True)
  before_ref[...] = carry_s[:, 0:128]
  carry_s[...] = carry_s[...] + jnp.sum(onehot, axis=1, keepdims=True)
  cnt_ref[...] = carry_s[...]

  row8 = lax.broadcasted_iota(jnp.int32, info_ref.shape, 0)
  info_ref[...] = jnp.where(row8 == 0, cls.astype(F32), jnp.where(row8 == 1, rank, 0.0))


_ROUTE_IN_SPECS = [
    pl.BlockSpec((1, D), lambda i: (0, 0)),
    pl.BlockSpec((None, 1, D), lambda i: (_seg(i, TM), 0, 0)),
    pl.BlockSpec((None, 1, D), lambda i: (_seg(i, TM), 0, 0)),
    pl.BlockSpec((D, 128), lambda i: (0, 0)),
    pl.BlockSpec((D, 128), lambda i: (0, 0)),
    pl.BlockSpec((1, 128), lambda i: (0, 0)),
    pl.BlockSpec((TM, TM), lambda i: (0, 0)),
]
_ROUTE_OUT_SPECS = (
    pl.BlockSpec((None, 8, TM), lambda i: (i, 0, 0)),
    pl.BlockSpec((ROUTER_ROWS, TM), lambda i: (0, 0)),
    pl.BlockSpec((None, ROUTER_ROWS, 128), lambda i: (i, 0, 0)),
)
_ROUTE_OUT_SHAPE = (
    jax.ShapeDtypeStruct((T // TM, 8, TM), F32),
    jax.ShapeDtypeStruct((ROUTER_ROWS, TM), F32),
    jax.ShapeDtypeStruct((T // TM, ROUTER_ROWS, 128), F32),
)
_ROUTE_SCRATCH = [pltpu.VMEM((ROUTER_ROWS, TM), F32)]


def _router(x, route_args):
  return pl.pallas_call(
      _router_kernel,
      grid=(T // TM,),
      in_specs=[pl.BlockSpec((TM, D), lambda i: (i, 0))] + _ROUTE_IN_SPECS,
      out_specs=_ROUTE_OUT_SPECS,
      out_shape=_ROUTE_OUT_SHAPE,
      scratch_shapes=_ROUTE_SCRATCH,
      compiler_params=_params(("arbitrary",)),
      name="router",
  )(x, *route_args)


def _slot_of(slot_ref, token):
  return slot_ref[token]


def _tile_rows(ref, row):
  return ref.at[pl.ds(pl.multiple_of(row * ROW_SUB, ROW_SUB), ROW_SUB), :]


RUN_CHUNK = 8


def _run_copy(stage, buf, hs_hbm, sem, src, dst, rows):
  return pltpu.make_async_copy(
      stage.at[buf, pl.ds(pl.multiple_of(src * ROW_SUB, ROW_SUB), rows * ROW_SUB), :],
      hs_hbm.at[pl.ds(pl.multiple_of(dst * ROW_SUB, ROW_SUB), rows * ROW_SUB), :],
      sem.at[buf])


def _start_run_copies(hist_ref, lstart_ref, dst_ref, step, stage, buf, hs_hbm, sem):
  shift = RUN_CHUNK.bit_length() - 1
  for c in range(N_CLASSES):
    k = step * N_CLASSES + c
    n_rows, src, dst = hist_ref[k], lstart_ref[k], dst_ref[k]

    def chunk(q, carry, src=src, dst=dst):
      _run_copy(stage, buf, hs_hbm, sem, src + q * RUN_CHUNK, dst + q * RUN_CHUNK, RUN_CHUNK).start()
      return carry
    n_chunks = lax.shift_right_logical(n_rows, jnp.int32(shift))
    lax.fori_loop(0, n_chunks, chunk, 0)
    src, dst = src + n_chunks * RUN_CHUNK, dst + n_chunks * RUN_CHUNK
    size = RUN_CHUNK // 2
    while size:
      take = n_rows & size

      @pl.when(take != 0)
      def _(src=src, dst=dst, size=size):
        _run_copy(stage, buf, hs_hbm, sem, src, dst, size).start()
      src, dst = src + take, dst + take
      size //= 2


def _wait_run_copies(stage, buf, hs_hbm, sem):
  pltpu.make_async_copy(stage.at[buf], hs_hbm.at[pl.ds(0, TS * ROW_SUB), :], sem.at[buf]).wait()


def _scatter_kernel(hist_ref, lstart_ref, dst_ref, pstart_ref, pn_ref, nv_ref,
                    x_ref, g_ref, sh_ref, sc_ref, pos_ref, hs_hbm,
                    stage, zero_s, sem, zsem):
  i = pl.program_id(0)
  n = pl.num_programs(0)
  buf = i % 2

  def pad_copy(c, r):
    return pltpu.make_async_copy(zero_s, _tile_rows(hs_hbm, pstart_ref[c] + r), zsem.at[0])

  def tail_copy(t):
    rows = TMM * ROW_SUB
    return pltpu.make_async_copy(
        stage.at[1, pl.ds(0, rows), :],
        hs_hbm.at[pl.ds(pl.multiple_of(t * rows, rows), rows), :], zsem.at[1])

  @pl.when(i == 0)
  def _():
    stage[...] = jnp.zeros_like(stage)
    zero_s[...] = jnp.zeros_like(zero_s)
    for c in range(N_CLASSES):
      def start(r, carry, c=c):
        pad_copy(c, r).start()
        return carry
      lax.fori_loop(0, pn_ref[c], start, 0)

    def tail_start(t, carry):
      tail_copy(t).start()
      return carry
    lax.fori_loop(nv_ref[0], NT_E, tail_start, 0)
    for c in range(N_CLASSES):
      def wait(r, carry, c=c):
        pad_copy(c, r).wait()
        return carry
      lax.fori_loop(0, pn_ref[c], wait, 0)

    def tail_wait(t, carry):
      tail_copy(t).wait()
      return carry
    lax.fori_loop(nv_ref[0], NT_E, tail_wait, 0)

  h = _norm_mod(x_ref[...], g_ref[...], sh_ref[...], sc_ref[...])
  pos = pos_ref[0:1, :]
  row = lax.broadcasted_iota(jnp.int32, (TS, TS), 0).astype(F32)
  perm = jnp.where(row == pos, 1.0, 0.0).astype(BF16)
  hsort = jnp.dot(perm, h.astype(BF16), preferred_element_type=F32)
  for s in range(ROW_SUB):
    stage[buf, pl.ds(s, TS, stride=ROW_SUB), :] = hsort[:, s * 128:(s + 1) * 128]

  _start_run_copies(hist_ref, lstart_ref, dst_ref, i, stage, buf, hs_hbm, sem)
  pl.when(i > 0)(lambda: _wait_run_copies(stage, 1 - buf, hs_hbm, sem))
  pl.when(i == n - 1)(lambda: _wait_run_copies(stage, buf, hs_hbm, sem))


def _scatter_rows(hist, lstart, dst, pad_start, pad_n, n_valid, x, g, shift, scale, pos):
  assert TS == TM and TS >= TMM
  grid_spec = pltpu.PrefetchScalarGridSpec(
      num_scalar_prefetch=6,
      grid=(T // TS,),
      in_specs=[
          pl.BlockSpec((TS, D), lambda i, *_: (i, 0)),
          pl.BlockSpec((1, D), lambda i, *_: (0, 0)),
          pl.BlockSpec((None, 1, D), lambda i, *_: (_seg(i, TS), 0, 0)),
          pl.BlockSpec((None, 1, D), lambda i, *_: (_seg(i, TS), 0, 0)),
          pl.BlockSpec((None, 8, TS), lambda i, *_: (i, 0, 0)),
      ],
      out_specs=pl.BlockSpec(memory_space=pl.ANY),
      scratch_shapes=[
          pltpu.VMEM((2, TS * ROW_SUB, 128), F32),
          pltpu.VMEM((ROW_SUB, 128), F32),
          pltpu.SemaphoreType.DMA((2,)),
          pltpu.SemaphoreType.DMA((2,)),
      ],
  )
  return pl.pallas_call(
      _scatter_kernel,
      grid_spec=grid_spec,
      out_shape=jax.ShapeDtypeStruct((P_PAD * ROW_SUB, 128), F32),
      compiler_params=_params(("arbitrary",)),
      name="scatter_rows",
  )(hist, lstart, dst, pad_start, pad_n, n_valid, x, g, shift, scale, pos)


def _router_logits(h, wrh_ref, wrl_ref, br_ref):
  h_hi = h.astype(BF16)
  h_lo = (h - h_hi.astype(F32)).astype(BF16)
  dot = functools.partial(jnp.dot, preferred_element_type=F32)
  return (dot(h_hi, wrh_ref[...]) + dot(h_lo, wrh_ref[...]) + dot(h_hi, wrl_ref[...])
          + br_ref[...])


def _expert_kernel(elo_ref, ehi_ref, nv_ref, hs_ref, wrh_ref, br_ref,
                   wg_lo, wg_hi, wu_lo, wu_hi, wd_lo, wd_hi, y_ref):
  i = pl.program_id(0)

  @pl.when(i < nv_ref[0])
  def _():
    h = jnp.concatenate(
        [hs_ref[pl.ds(s, TMM, stride=ROW_SUB), :] for s in range(ROW_SUB)], axis=1)
    hb = h.astype(BF16)
    logits = jnp.dot(hb, wrh_ref[...], preferred_element_type=F32) + br_ref[...]
    lane = lax.broadcasted_iota(jnp.int32, logits.shape, 1)
    elo, ehi = elo_ref[i], ehi_ref[i]

    def pick(idx):
      return jnp.sum(jnp.where(lane == idx, logits, 0.0), axis=-1, keepdims=True)
    l_g, l_lo, l_hi = pick(elo // EPG), pick(N_GROUPS + elo), pick(N_GROUPS + ehi)
    p_top = 1.0 / jnp.sum(jnp.where(lane < N_GROUPS, jnp.exp(logits - l_g), 0.0),
                          axis=-1, keepdims=True)
    wl = jnp.broadcast_to(p_top / (1.0 + jnp.exp(l_hi - l_lo)), (TMM, D_EXPERT))
    wh = jnp.broadcast_to(p_top / (1.0 + jnp.exp(l_lo - l_hi)), (TMM, D_EXPERT))
    dot = functools.partial(jnp.dot, preferred_element_type=F32)
    a_lo = (_silu(dot(hb, wg_lo[...])) * dot(hb, wu_lo[...])) * wl
    a_hi = (_silu(dot(hb, wg_hi[...])) * dot(hb, wu_hi[...])) * wh
    a_lo, a_hi = a_lo.astype(BF16), a_hi.astype(BF16)
    nb = 2 * 128
    for c in range(0, D, nb):
      y = dot(a_lo, wd_lo[:, c:c + nb]) + dot(a_hi, wd_hi[:, c:c + nb])
      for s in range(c // 128, (c + nb) // 128):
        y_ref[pl.ds(s, TMM, stride=ROW_SUB), :] = y[:, s * 128 - c:(s + 1) * 128 - c]

  @pl.when(i >= nv_ref[0])
  def _():
    y_ref[...] = jnp.zeros_like(y_ref)


def _experts(layer, tile_elo, tile_ehi, n_valid, hs, wr_hi, br, w_gate, w_up, w_down):
  up_spec_lo = pl.BlockSpec((None, None, D, D_EXPERT), lambda i, elo, ehi, nv: (layer, elo[i], 0, 0))
  up_spec_hi = pl.BlockSpec((None, None, D, D_EXPERT), lambda i, elo, ehi, nv: (layer, ehi[i], 0, 0))
  dn_spec_lo = pl.BlockSpec((None, None, D_EXPERT, D), lambda i, elo, ehi, nv: (layer, elo[i], 0, 0))
  dn_spec_hi = pl.BlockSpec((None, None, D_EXPERT, D), lambda i, elo, ehi, nv: (layer, ehi[i], 0, 0))
  grid_spec = pltpu.PrefetchScalarGridSpec(
      num_scalar_prefetch=3,
      grid=(NT_E,),
      in_specs=[
          pl.BlockSpec((TMM * ROW_SUB, 128), lambda i, elo, ehi, nv: (i, 0)),
          pl.BlockSpec((D, 128), lambda i, elo, ehi, nv: (0, 0)),
          pl.BlockSpec((1, 128), lambda i, elo, ehi, nv: (0, 0)),
          up_spec_lo, up_spec_hi, up_spec_lo, up_spec_hi, dn_spec_lo, dn_spec_hi,
      ],
      out_specs=pl.BlockSpec((TMM * ROW_SUB, 128), lambda i, elo, ehi, nv: (i, 0)),
  )
  return pl.pallas_call(
      _expert_kernel,
      grid_spec=grid_spec,
      out_shape=jax.ShapeDtypeStruct((P_PAD * ROW_SUB, 128), F32),
      compiler_params=_params(("arbitrary",)),
      name="experts",
  )(tile_elo, tile_ehi, n_valid, hs, wr_hi, br,
    w_gate, w_gate, w_up, w_up, w_down, w_down)


def _gather_copy(slot_ref, step, r, y_hbm, ybuf, buf, sem):
  return pltpu.make_async_copy(
      _tile_rows(y_hbm, _slot_of(slot_ref, step * TC + r)),
      ybuf.at[buf, pl.ds(r * ROW_SUB, ROW_SUB), :], sem.at[buf])


def _combine_kernel(slot_ref, x_ref, gate_ref, nf_ref, y_hbm, *rest, final):
  if final:
    op_ref, os_ref, ybuf, sem = rest
  else:
    o_ref, ybuf, sem = rest
  i = pl.program_id(0)
  n = pl.num_programs(0)
  buf = i % 2

  def start(step, b):
    def body(r2, carry):
      for q in range(2):
        _gather_copy(slot_ref, step, 2 * r2 + q, y_hbm, ybuf, b, sem).start(priority=q)
      return carry
    lax.fori_loop(0, TC // 2, body, 0, unroll=4)

  @pl.when(i == 0)
  def _():
    start(0, 0)

  @pl.when(i + 1 < n)
  def _():
    start(i + 1, 1 - buf)

  def wait(r, carry):
    _gather_copy(slot_ref, i, r, y_hbm, ybuf, buf, sem).wait()
    return carry
  lax.fori_loop(0, TC, wait, 0, unroll=8)

  y = jnp.concatenate(
      [ybuf[buf, pl.ds(s, TC, stride=ROW_SUB), :] for s in range(ROW_SUB)], axis=1)
  x = x_ref[...] + gate_ref[...] * y
  if not final:
    o_ref[...] = x
    return
  x = (x * lax.rsqrt(jnp.mean(x * x, axis=-1, keepdims=True) + EPS)) * nf_ref[...]
  is_prompt = i < T_P // TC

  @pl.when(is_prompt)
  def _():
    op_ref[...] = x

  @pl.when(jnp.logical_not(is_prompt))
  def _():
    os_ref[...] = x


def _combine(slot, x, gate, norm_f, y_sorted, final):
  npt = T_P // TC
  if final:
    out_specs = (pl.BlockSpec((TC, D), lambda i, s: (jnp.minimum(i, npt - 1), 0)),
                 pl.BlockSpec((TC, D), lambda i, s: (jnp.maximum(i - npt, 0), 0)))
    out_shape = (jax.ShapeDtypeStruct((T_P, D), F32), jax.ShapeDtypeStruct((T_S, D), F32))
  else:
    out_specs = pl.BlockSpec((TC, D), lambda i, s: (i, 0))
    out_shape = jax.ShapeDtypeStruct((T, D), F32)
  grid_spec = pltpu.PrefetchScalarGridSpec(
      num_scalar_prefetch=1,
      grid=(T // TC,),
      in_specs=[
          pl.BlockSpec((TC, D), lambda i, s: (i, 0)),
          pl.BlockSpec((None, 1, D), lambda i, s: (_seg(i, TC), 0, 0)),
          pl.BlockSpec((1, D), lambda i, s: (0, 0)),
          pl.BlockSpec(memory_space=pl.ANY),
      ],
      out_specs=out_specs,
      scratch_shapes=[pltpu.VMEM((2, TC * ROW_SUB, 128), F32), pltpu.SemaphoreType.DMA((2,))],
  )
  return pl.pallas_call(
      functools.partial(_combine_kernel, final=final),
      grid_spec=grid_spec,
      out_shape=out_shape,
      compiler_params=_params(("arbitrary",)),
      name="combine_final" if final else "combine",
  )(slot, x, gate, norm_f, y_sorted)


def _class_experts():
  lo, hi = [], []
  for g in range(N_GROUPS):
    for a in range(EPG):
      for b in range(a + 1, EPG):
        lo.append(g * EPG + a)
        hi.append(g * EPG + b)
  return np.asarray(lo, np.int32), np.asarray(hi, np.int32)


def _moe_layer(layer, x, routing, route_args, gate, w_gate, w_up, w_down, norm_f):
  final = layer == DEPTH - 1
  g, shift, scale, wr_hi, _, br, _ = route_args
  info, cnt, before_tiles = routing

  cls = info[:, 0, :].astype(jnp.int32)
  rank = info[:, 1, :].astype(jnp.int32)
  counts = cnt[:N_CLASSES, 0].astype(jnp.int32)
  tiles = (counts + TMM - 1) // TMM
  tile_end = jnp.cumsum(tiles)
  offs = (tile_end - tiles) * TMM
  before = before_tiles[:, :N_CLASSES, 0].astype(jnp.int32)
  hist = jnp.concatenate([before[1:], counts[None, :]], axis=0) - before
  lstart = jnp.cumsum(hist, axis=1) - hist
  dst = offs[None, :] + before
  slot = rank
  pos = rank
  for k in range(N_CLASSES):
    hit = cls == k
    slot = slot + jnp.where(hit, offs[k], 0)
    pos = pos + jnp.where(hit, (lstart[:, k] - before[:, k])[:, None], 0)
  slot = slot.reshape(T)
  pos = jnp.broadcast_to(pos.astype(F32)[:, None, :], (T // TM, 8, TM))
  n_valid = tile_end[-1]
  tile_ids = jnp.minimum(jnp.arange(NT_E, dtype=jnp.int32), n_valid - 1)
  tile_cls = jnp.sum((tile_ids[:, None] >= tile_end[None, :]).astype(jnp.int32), axis=1)
  tile_cls = jnp.minimum(tile_cls, N_CLASSES - 1)
  cls_lo, cls_hi = _class_experts()
  tile_elo = jnp.asarray(cls_lo)[tile_cls]
  tile_ehi = jnp.asarray(cls_hi)[tile_cls]

  n_valid = n_valid.reshape(1)
  hs = _scatter_rows(hist.reshape(-1), lstart.reshape(-1), dst.reshape(-1),
                     offs + counts, tiles * TMM - counts, n_valid, x, g, shift, scale, pos)
  y_sorted = _experts(layer, tile_elo, tile_ehi, n_valid, hs, wr_hi, br, w_gate, w_up, w_down)
  return _combine(slot, x, gate, norm_f, y_sorted, final)


def kernel(x_prompt, x_sample, cache_k, cache_v, state_ret, c, c_ctx, norm1, norm2, w_ada, b_ada, attn_w_qkv, attn_q_norm, attn_k_norm, attn_w_o, pool_w, pool_scale, ret_w_in, ret_decay_logit, ret_norm, ret_w_out, moe_w_router_g, moe_b_router_g, moe_w_router_e, moe_b_router_e, moe_w_gate, moe_w_up, moe_w_down, norm_f):
  x = (x_prompt.reshape(T_P, D), x_sample.reshape(T_S, D))
  cond8 = jnp.concatenate([c_ctx[None, :], c, jnp.zeros((N_SEG - 1 - NB_S, D), F32)], axis=0)
  mods = _adaln(cond8, w_ada, b_ada)
  mods = mods.reshape(DEPTH, N_SEG, 6, 1, D).transpose(0, 2, 1, 3, 4)
  rope = _rope_tables()
  tri = jnp.triu(jnp.ones((TM, TM), BF16), 1)
  pad_r = 128 - N_GROUPS - N_EXPERTS
  norm_f2 = norm_f.reshape(1, D)

  w_gate, w_up, w_down = moe_w_gate.astype(BF16), moe_w_up.astype(BF16), moe_w_down.astype(BF16)

  new_k, new_v, new_s = [], [], []
  for i in range(DEPTH):
    kind, j = i % 3, i // 3
    m = mods[i]
    g1 = norm1[i].reshape(1, D)
    wr = jnp.concatenate([moe_w_router_g[i], moe_w_router_e[i], jnp.zeros((D, pad_r), F32)], axis=1)
    br = jnp.concatenate([moe_b_router_g[i], moe_b_router_e[i], jnp.zeros((pad_r,), F32)]).reshape(1, 128)
    wr_hi = wr.astype(BF16)
    wr_lo = (wr - wr_hi.astype(F32)).astype(BF16)
    route_args = (norm2[i].reshape(1, D), m[3], m[4], wr_hi, wr_lo, br, tri)
    if kind == 0:
      qkv = _nm_matmul(x, g1, m[0], m[1], attn_w_qkv[j].astype(BF16), TM, "qkv_proj")
      qn = attn_q_norm[j].reshape(1, HEAD_DIM)
      kn = attn_k_norm[j].reshape(1, HEAD_DIM)
      o_p, kc, vc = _attn_prompt(qkv, qn, kn)
      ck = cache_k[:, j].reshape(NB_S, PAST, N_KV * HEAD_DIM)
      cv = cache_v[:, j].reshape(NB_S, PAST, N_KV * HEAD_DIM)
      o_s = _attn_latent(qkv, ck, cv, qn, kn, rope)
      new_k.append(kc.reshape(NB_P, SEQ_P, N_KV, HEAD_DIM))
      new_v.append(vc.reshape(NB_P, SEQ_P, N_KV, HEAD_DIM))
      x, *routing = _mm_res_route((o_p, o_s), attn_w_o[j].astype(BF16), x, m[2], route_args, "attn_out")
    elif kind == 1:
      x = _pool_layer(x, g1, m[0], m[1], m[2], pool_w[j].astype(BF16), pool_scale[j].reshape(1, D))
      routing = _router(x, route_args)
    else:
      p = _nm_matmul(x, g1, m[0], m[1], ret_w_in[j].astype(BF16), TM, "ret_proj")
      dl = jnp.broadcast_to(ret_decay_logit[j].T[:, :, None, None], (RET_HEADS, 2, 1, RET_DK))
      ng = ret_norm[j].reshape(1, RET_HEADS * RET_DV)
      y_p, s_new = _retention(p, dl, ng, None, True)
      y_s = _retention(p, dl, ng, state_ret[:, j], False)
      new_s.append(s_new)
      x, *routing = _mm_res_route((y_p, y_s), ret_w_out[j].astype(BF16), x, m[2], route_args, "ret_out")
    x = _moe_layer(i, x, routing, route_args, m[5], w_gate, w_up, w_down, norm_f2)

  y_prompt = x[0].reshape(NB_P, SEQ_P, D)
  y_sample = x[1].reshape(NB_S, SEQ_S, D)
  new_cache_k = jnp.stack(new_k, axis=1)
  new_cache_v = jnp.stack(new_v, axis=1)
  assert len(new_s) == 1
  new_state_ret = new_s[0].reshape(NB_P, 1, 2, RET_HEADS, RET_DK, RET_DV)
  return (y_prompt, y_sample, new_cache_k, new_cache_v, new_state_ret)
```

```python
import functools

import jax
import jax.numpy as jnp
import numpy as np
from jax import lax
from jax.experimental import pallas as pl
from jax.experimental.pallas import tpu as pltpu

F32 = jnp.float32
BF16 = jnp.bfloat16

D = 1024
NB_P, SEQ_P = 32, 256
NB_S, SEQ_S = 4, 2048
T_P = NB_P * SEQ_P
T_S = NB_S * SEQ_S
T = T_P + T_S
DEPTH = 4
GRID_W = 64
HEAD_DIM = 128
N_HEADS = 8
N_KV = 2
KV_GROUP = N_HEADS // N_KV
PAST = 256
ROPE_THETA = 10000.0
POOL_WINDOWS = (2, 4, 8, 16)
POOL_GROUP = D // 4
POOL_HALO = 8
RET_HEADS = 8
RET_DK = 128
RET_DV = 256
RET_CHUNK = 128
RET_HEADS_PER_STEP_PROMPT = 4
RET_HEADS_PER_STEP_LATENT = 2
N_GROUPS = 4
EPG = 4
N_EXPERTS = 16
D_EXPERT = 256
N_PAIRS = 6
N_CLASSES = N_GROUPS * N_PAIRS
ROUTER_ROWS = 32
EPS = 1e-6
LOG2E = 1.4426950408889634
NEG = -1e30
N_SEG = 8

VMEM_LIMIT_BYTES = 52 * 1024 * 1024

TM = 512
TQ = 1024
ATTN_UNIT_Q = 128
ATTN_KEY_CHUNK = 256
ATTN_HEAD_STACK = 2
TP = 256
TMM = 256
TS = 512
TC = 256
ROW_SUB = 8
PACK_SUB = D // 2 // 128
INFO_SUB = PACK_SUB
P_PAD = T + N_CLASSES * TMM
NT_E = P_PAD // TMM


def _params(sem):
  return pltpu.CompilerParams(dimension_semantics=sem, vmem_limit_bytes=VMEM_LIMIT_BYTES)


def _seg(i, tm):
  npt = T_P // tm
  return jnp.where(i < npt, 0, (i - npt) // (SEQ_S // tm) + 1)


def _norm_mod(x, g, shift, scale):
  r = lax.rsqrt(jnp.mean(x * x, axis=-1, keepdims=True) + EPS)
  return ((x * r) * g) * (1.0 + scale) + shift


def _silu(x):
  return x * (1.0 / (1.0 + jnp.exp(-x)))


def _adaln_kernel(c_ref, w_ref, b_ref, o_ref):
  s = _silu(c_ref[...]).astype(BF16)
  o_ref[...] = jnp.dot(s, w_ref[...].astype(BF16), preferred_element_type=F32) + b_ref[...]


def _adaln(cond8, w_ada, b_ada):
  tn = 1536
  return pl.pallas_call(
      _adaln_kernel,
      grid=(DEPTH, 6 * D // tn),
      in_specs=[
          pl.BlockSpec((N_SEG, D), lambda l, j: (0, 0)),
          pl.BlockSpec((None, D, tn), lambda l, j: (l, 0, j)),
          pl.BlockSpec((None, 1, tn), lambda l, j: (l, 0, j)),
      ],
      out_specs=pl.BlockSpec((None, N_SEG, tn), lambda l, j: (l, 0, j)),
      out_shape=jax.ShapeDtypeStruct((DEPTH, N_SEG, 6 * D), F32),
      compiler_params=_params(("arbitrary", "arbitrary")),
      name="adaln",
  )(cond8, w_ada, b_ada.reshape(DEPTH, 1, 6 * D))


def _token_rows(x, tm, width):
  npt = T_P // tm
  if isinstance(x, tuple):
    arrays = x
    latent_map = lambda i: (jnp.maximum(i - npt, 0), 0)
  else:
    arrays = (x, x)
    latent_map = lambda i: (jnp.maximum(i, npt), 0)
  specs = [pl.BlockSpec((tm, width), lambda i: (jnp.minimum(i, npt - 1), 0)),
           pl.BlockSpec((tm, width), latent_map)]
  return arrays, specs


def _pick_rows(p_ref, s_ref, tm):
  return jnp.where(pl.program_id(0) < T_P // tm, p_ref[...], s_ref[...])


def _nm_matmul_kernel(xp_ref, xs_ref, g_ref, sh_ref, sc_ref, w_ref, o_ref, *, tm, n_chunk):
  x = _pick_rows(xp_ref, xs_ref, tm)
  h = _norm_mod(x, g_ref[...], sh_ref[...], sc_ref[...]).astype(BF16)
  n = w_ref.shape[1]
  for c in range(0, n, n_chunk):
    o_ref[:, c:c + n_chunk] = jnp.dot(
        h, w_ref[:, c:c + n_chunk], preferred_element_type=F32).astype(o_ref.dtype)


def _nm_matmul(x, g, shift, scale, w, tm, name):
  n = w.shape[1]
  x_arrays, x_specs = _token_rows(x, tm, D)
  return pl.pallas_call(
      functools.partial(_nm_matmul_kernel, tm=tm, n_chunk=512),
      grid=(T // tm,),
      in_specs=x_specs + [
          pl.BlockSpec((1, D), lambda i: (0, 0)),
          pl.BlockSpec((None, 1, D), lambda i: (_seg(i, tm), 0, 0)),
          pl.BlockSpec((None, 1, D), lambda i: (_seg(i, tm), 0, 0)),
          pl.BlockSpec((D, n), lambda i: (0, 0)),
      ],
      out_specs=pl.BlockSpec((tm, n), lambda i: (i, 0)),
      out_shape=jax.ShapeDtypeStruct((T, n), BF16),
      compiler_params=_params(("arbitrary",)),
      name=name,
  )(*x_arrays, g, shift, scale, w)


def _mm_res_route_kernel(ap_ref, as_ref, xp_ref, xs_ref, w_ref, gate_ref, *rest):
  n_route_in = len(_ROUTE_IN_SPECS)
  route_in, (o_ref, *route_out) = rest[:n_route_in], rest[n_route_in:]
  a = _pick_rows(ap_ref, as_ref, TM)
  x = _pick_rows(xp_ref, xs_ref, TM)
  x_new = x + gate_ref[...] * jnp.dot(a, w_ref[...], preferred_element_type=F32)
  o_ref[...] = x_new
  _route_tile(x_new, *route_in, *route_out)


def _mm_res_route(a, w, x, gate, route_args, name):
  k = w.shape[0]
  a_arrays, a_specs = _token_rows(a, TM, k)
  x_arrays, x_specs = _token_rows(x, TM, D)
  return pl.pallas_call(
      _mm_res_route_kernel,
      grid=(T // TM,),
      in_specs=a_specs + x_specs + [
          pl.BlockSpec((k, D), lambda i: (0, 0)),
          pl.BlockSpec((None, 1, D), lambda i: (_seg(i, TM), 0, 0)),
      ] + _ROUTE_IN_SPECS,
      out_specs=(pl.BlockSpec((TM, D), lambda i: (i, 0)),) + _ROUTE_OUT_SPECS,
      out_shape=(jax.ShapeDtypeStruct((T, D), F32),) + _ROUTE_OUT_SHAPE,
      scratch_shapes=_ROUTE_SCRATCH,
      compiler_params=_params(("arbitrary",)),
      name=name,
  )(*a_arrays, *x_arrays, w, gate, *route_args)


def _rope(x, c, a, b):
  return x * c + pltpu.roll(x, 96, 1) * a + pltpu.roll(x, 32, 1) * b


def _head_norm(x, w):
  return (x * lax.rsqrt(jnp.mean(x * x, axis=-1, keepdims=True) + EPS)) * w


def _attn_latent_kernel(q_ref, kn_ref, vn_ref, ck_ref, cv_ref, qw_ref, kw_ref,
                        cq_ref, aq_ref, bq_ref, ckk_ref, akk_ref, bkk_ref,
                        o_ref, k_s, vt_s, *st_refs, tq):
  @pl.when(pl.program_id(2) == 0)
  def _():
    k = _rope(_head_norm(kn_ref[...].astype(F32), kw_ref[...]), ckk_ref[...], akk_ref[...], bkk_ref[...])
    k_s[0:PAST, :] = ck_ref[...].astype(BF16)
    k_s[PAST:, :] = k.astype(BF16)
    vt_s[:, 0:PAST] = cv_ref[...].T.astype(BF16)
    vt_s[:, PAST:] = vn_ref[...].astype(F32).T.astype(BF16)

  qa = q_ref[...].astype(F32)
  lk = k_s.shape[0]

  def prep_q(h):
    qh = _head_norm(qa[:, h * HEAD_DIM:(h + 1) * HEAD_DIM], qw_ref[...])
    qh = _rope(qh, cq_ref[...], aq_ref[...], bq_ref[...])
    return (qh * (HEAD_DIM ** -0.5 * LOG2E)).astype(BF16)

  nq = ATTN_HEAD_STACK * ATTN_UNIT_Q
  chunks = [(c0, min(c0 + ATTN_KEY_CHUNK, lk)) for c0 in range(0, lk, ATTN_KEY_CHUNK)]
  qh_all = [prep_q(h) for h in range(KV_GROUP)]
  units = [(h0, r0) for r0 in range(0, tq, ATTN_UNIT_Q) for h0 in range(0, KV_GROUP, ATTN_HEAD_STACK)]

  def score_chunk(u, c0, c1, m):
    h0, r0 = units[u]
    qp = jnp.concatenate([qh_all[h0 + d][r0:r0 + ATTN_UNIT_Q] for d in range(ATTN_HEAD_STACK)], axis=0)
    st = lax.dot_general(k_s[c0:c1, :], qp, (((1,), (1,)), ((), ())),
                         preferred_element_type=F32)
    st_refs[u % 2][c0:c1, :] = st
    return jnp.maximum(m, jnp.max(st, axis=0, keepdims=True))

  def value_chunk(u, c0, c1, m, l, acc):
    pt = jnp.exp2(st_refs[u % 2][c0:c1, :] - m)
    return (l + jnp.sum(pt, axis=0, keepdims=True),
            acc + jnp.dot(vt_s[:, c0:c1], pt.astype(BF16), preferred_element_type=F32))

  def finish(u, l, acc):
    h0, r0 = units[u]
    o = acc / l
    for d in range(ATTN_HEAD_STACK):
      h = h0 + d
      o_ref[r0:r0 + ATTN_UNIT_Q, h * HEAD_DIM:(h + 1) * HEAD_DIM] = (
          o[:, d * ATTN_UNIT_Q:(d + 1) * ATTN_UNIT_Q].T.astype(BF16))

  m_prev = None
  for u in range(len(units) + 1):
    m = jnp.full((1, nq), NEG, F32)
    l = jnp.zeros((1, nq), F32)
    acc = jnp.zeros((HEAD_DIM, nq), F32)
    for c0, c1 in chunks:
      if u < len(units):
        m = score_chunk(u, c0, c1, m)
      if u > 0:
        l, acc = value_chunk(u - 1, c0, c1, m_prev, l, acc)
    if u > 0:
      finish(u - 1, l, acc)
    m_prev = m


def _attn_scratch(lk):
  return ([pltpu.VMEM((lk, HEAD_DIM), BF16), pltpu.VMEM((HEAD_DIM, lk), BF16)]
          + [pltpu.VMEM((lk, ATTN_HEAD_STACK * ATTN_UNIT_Q), F32) for _ in range(2)])


def _rope_tables():
  rows = SEQ_S // GRID_W
  t_row = jnp.broadcast_to(jnp.arange(rows)[:, None], (rows, GRID_W)).reshape(-1)
  t_col = jnp.broadcast_to(jnp.arange(GRID_W)[None, :], (rows, GRID_W)).reshape(-1)
  nf = HEAD_DIM // 4
  inv = ROPE_THETA ** (-jnp.arange(nf, dtype=F32) / nf)
  ang_r = t_row.astype(F32)[:, None] * inv[None, :]
  ang_c = t_col.astype(F32)[:, None] * inv[None, :]
  cr, sr, cc, sc = jnp.cos(ang_r), jnp.sin(ang_r), jnp.cos(ang_c), jnp.sin(ang_c)
  z = jnp.zeros_like(sr)
  c = jnp.concatenate([cr, cr, cc, cc], axis=1)
  a = jnp.concatenate([-sr, z, -sc, z], axis=1)
  b = jnp.concatenate([z, sr, z, sc], axis=1)
  return c, a, b


def _attn_prompt_kernel(qkv_ref, qw_ref, kw_ref, o_ref, ko_ref, vo_ref):
  nq = N_HEADS * HEAD_DIM
  nk = N_KV * HEAD_DIM
  for kh in range(N_KV):
    kcols = slice(nq + kh * HEAD_DIM, nq + (kh + 1) * HEAD_DIM)
    vcols = slice(nq + nk + kh * HEAD_DIM, nq + nk + (kh + 1) * HEAD_DIM)
    k = _head_norm(qkv_ref[:, kcols].astype(F32), kw_ref[...])
    v = qkv_ref[:, vcols].astype(F32)
    ko_ref[pl.ds(kh, SEQ_P, stride=N_KV), :] = k
    vo_ref[pl.ds(kh, SEQ_P, stride=N_KV), :] = v
    kb = k.astype(BF16)
    vt = v.T.astype(BF16)
    for h in range(kh * KV_GROUP, (kh + 1) * KV_GROUP):
      hcols = slice(h * HEAD_DIM, (h + 1) * HEAD_DIM)
      qh = _head_norm(qkv_ref[:, hcols].astype(F32), qw_ref[...])
      qh = (qh * (HEAD_DIM ** -0.5 * LOG2E)).astype(BF16)
      st = lax.dot_general(kb, qh, (((1,), (1,)), ((), ())), preferred_element_type=F32)
      pt = jnp.exp2(st - jnp.max(st, axis=0, keepdims=True))
      l = jnp.sum(pt, axis=0, keepdims=True)
      acc = jnp.dot(vt, pt.astype(BF16), preferred_element_type=F32)
      o_ref[:, hcols] = (acc / l).T.astype(BF16)


def _attn_prompt(qkv, q_norm, k_norm):
  width = (N_HEADS + 2 * N_KV) * HEAD_DIM
  out_shapes = (
      jax.ShapeDtypeStruct((T_P, N_HEADS * HEAD_DIM), BF16),
      jax.ShapeDtypeStruct((NB_P, SEQ_P * N_KV, HEAD_DIM), F32),
      jax.ShapeDtypeStruct((NB_P, SEQ_P * N_KV, HEAD_DIM), F32),
  )
  return pl.pallas_call(
      _attn_prompt_kernel,
      grid=(NB_P,),
      in_specs=[
          pl.BlockSpec((SEQ_P, width), lambda b: (b, 0)),
          pl.BlockSpec((1, HEAD_DIM), lambda b: (0, 0)),
          pl.BlockSpec((1, HEAD_DIM), lambda b: (0, 0)),
      ],
      out_specs=(
          pl.BlockSpec((SEQ_P, N_HEADS * HEAD_DIM), lambda b: (b, 0)),
          pl.BlockSpec((None, SEQ_P * N_KV, HEAD_DIM), lambda b: (b, 0, 0)),
          pl.BlockSpec((None, SEQ_P * N_KV, HEAD_DIM), lambda b: (b, 0, 0)),
      ),
      out_shape=out_shapes,
      compiler_params=_params(("arbitrary",)),
      name="attn_prompt",
  )(qkv, q_norm, k_norm)


def _attn_latent(qkv, cache_k, cache_v, q_norm, k_norm, rope):
  kcol = N_HEADS
  vcol = kcol + N_KV
  nq = SEQ_S // TQ
  row0 = T_P // TQ
  seq0 = T_P // SEQ_S
  lk = PAST + SEQ_S
  c, a, b = rope
  tab_q = pl.BlockSpec((TQ, HEAD_DIM), lambda bb, h, i: (i, 0))
  tab_k = pl.BlockSpec((SEQ_S, HEAD_DIM), lambda bb, h, i: (0, 0))
  return pl.pallas_call(
      functools.partial(_attn_latent_kernel, tq=TQ),
      grid=(NB_S, N_KV, nq),
      in_specs=[
          pl.BlockSpec((TQ, KV_GROUP * HEAD_DIM), lambda bb, h, i: (row0 + bb * nq + i, h)),
          pl.BlockSpec((SEQ_S, HEAD_DIM), lambda bb, h, i: (seq0 + bb, kcol + h)),
          pl.BlockSpec((SEQ_S, HEAD_DIM), lambda bb, h, i: (seq0 + bb, vcol + h)),
          pl.BlockSpec((None, PAST, HEAD_DIM), lambda bb, h, i: (bb, 0, h)),
          pl.BlockSpec((None, PAST, HEAD_DIM), lambda bb, h, i: (bb, 0, h)),
          pl.BlockSpec((1, HEAD_DIM), lambda bb, h, i: (0, 0)),
          pl.BlockSpec((1, HEAD_DIM), lambda bb, h, i: (0, 0)),
          tab_q, tab_q, tab_q, tab_k, tab_k, tab_k,
      ],
      out_specs=pl.BlockSpec((TQ, KV_GROUP * HEAD_DIM), lambda bb, h, i: (bb * nq + i, h)),
      out_shape=jax.ShapeDtypeStruct((T_S, N_HEADS * HEAD_DIM), BF16),
      scratch_shapes=_attn_scratch(lk),
      compiler_params=_params(("arbitrary", "arbitrary", "arbitrary")),
      name="attn_latent",
  )(qkv, qkv, qkv, cache_k, cache_v, q_norm, k_norm, c, a, b, c, a, b)


def _pool_kernel(x_ref, xp_ref, xn_ref, g_ref, sh_ref, sc_ref, gate_ref, w_ref, ps_ref, o_ref):
  t = pl.program_id(0)
  npt = T_P // TP
  tiles_s = SEQ_S // TP
  is_p = t < npt
  pos = jnp.where(is_p, 0, (t - npt) % tiles_s)
  ntile = jnp.where(is_p, SEQ_P // TP, tiles_s)
  seq_len = ntile * TP
  keep_prev = jnp.where(pos == 0, 0.0, 1.0)
  keep_next = jnp.where(pos == ntile - 1, 0.0, 1.0)

  g, sh, sc = g_ref[...], sh_ref[...], sc_ref[...]
  x = x_ref[...]
  h = _norm_mod(x, g, sh, sc)
  hp = _norm_mod(xp_ref[...], g, sh, sc) * keep_prev
  hn = _norm_mod(xn_ref[...], g, sh, sc) * keep_next
  ext = jnp.concatenate([hp, h, hn], axis=0)
  n_ext = TP + 2 * POOL_HALO
  tseq = pos * TP + lax.broadcasted_iota(jnp.int32, (TP, 128), 0)

  outs = []
  for gi, win in enumerate(POOL_WINDOWS):
    lo, hi = gi * POOL_GROUP, (gi + 1) * POOL_GROUP
    acc = ext[:, lo:hi]
    span = 1
    while span < win:
      acc = acc + pltpu.roll(acc, n_ext - span, 0)
      span *= 2
    start = POOL_HALO - win // 2
    if start:
      acc = pltpu.roll(acc, n_ext - start, 0)
    ssum = acc[0:TP]
    cnt = (jnp.minimum(tseq + win // 2, seq_len) - jnp.maximum(tseq - win // 2, 0)).astype(F32)
    cnt = jnp.concatenate([cnt] * (POOL_GROUP // 128), axis=1)
    dlt = (ssum / cnt - h[:, lo:hi]).astype(BF16)
    outs.append(jnp.dot(dlt, w_ref[gi], preferred_element_type=F32))
  y = jnp.concatenate(outs, axis=1) * ps_ref[...]
  o_ref[...] = x + gate_ref[...] * y


def _pool_layer(x, g, shift, scale, gate, w, pscale):
  hb = TP // POOL_HALO
  last = T // POOL_HALO - 1
  seg = lambda i: _seg(i, TP)
  return pl.pallas_call(
      _pool_kernel,
      grid=(T // TP,),
      in_specs=[
          pl.BlockSpec((TP, D), lambda i: (i, 0)),
          pl.BlockSpec((POOL_HALO, D), lambda i: (jnp.maximum(i * hb - 1, 0), 0)),
          pl.BlockSpec((POOL_HALO, D), lambda i: (jnp.minimum((i + 1) * hb, last), 0)),
          pl.BlockSpec((1, D), lambda i: (0, 0)),
          pl.BlockSpec((None, 1, D), lambda i: (seg(i), 0, 0)),
          pl.BlockSpec((None, 1, D), lambda i: (seg(i), 0, 0)),
          pl.BlockSpec((None, 1, D), lambda i: (seg(i), 0, 0)),
          pl.BlockSpec((4, POOL_GROUP, POOL_GROUP), lambda i: (0, 0, 0)),
          pl.BlockSpec((1, D), lambda i: (0, 0)),
      ],
      out_specs=pl.BlockSpec((TP, D), lambda i: (i, 0)),
      out_shape=jax.ShapeDtypeStruct((T, D), F32),
      compiler_params=_params(("arbitrary",)),
      name="pool",
  )(x, x, x, g, shift, scale, gate, w, pscale)


def _ret_kernel(*refs, seq_len, has_s0, nh):
  it = iter(refs)
  q_ref, k_ref, v_ref, gate_ref, dl_ref, ng_ref = [next(it) for _ in range(6)]
  s0_ref = next(it) if has_s0 else None
  y_ref = next(it)
  so_ref = None if has_s0 else next(it)
  u_s = next(it)
  dec_s, cdec_s = next(it), next(it)
  c = RET_CHUNK
  n = seq_len // c
  kscale = RET_DK ** -0.5
  nt = (((1,), (1,)), ((), ()))
  tn = (((0,), (0,)), ((), ()))

  @pl.when(pl.program_id(1) == 0)
  def _():
    ri = lax.broadcasted_iota(jnp.int32, (c, c), 0).astype(F32)
    ci = lax.broadcasted_iota(jnp.int32, (c, c), 1).astype(F32)
    ri2 = lax.broadcasted_iota(jnp.int32, (c, RET_DV), 0).astype(F32)
    for hh in range(nh):
      lg = -jnp.log1p(jnp.exp(-dl_ref[hh]))
      lgf, lgb = lg[0], lg[1]
      lgf2 = jnp.concatenate([lgf, lgf], axis=1)
      lgb2 = jnp.concatenate([lgb, lgb], axis=1)
      dec_s[hh, 0] = jnp.concatenate(
          [jnp.exp((c - 1.0 - ri) * lgf), jnp.exp(ri * lgb)], axis=1) * kscale
      intra = (jnp.where(ri >= ci, jnp.exp((ri - ci) * lgf), 0.0)
               + jnp.where(ri <= ci, jnp.exp((ci - ri) * lgb), 0.0)) * kscale
      dec_s[hh, 1] = jnp.concatenate([intra, intra], axis=1)
      dec_s[hh, 2] = jnp.exp((ri2 + 1.0) * lgf2)
      dec_s[hh, 3] = jnp.exp((c - ri2) * lgb2)
      cdec_s[hh, 0] = jnp.exp(c * lgf2)
      cdec_s[hh, 1] = jnp.exp(c * lgb2)

  for hh in range(nh):
    qcols = slice(hh * RET_DK, (hh + 1) * RET_DK)
    vcols = slice(hh * RET_DV, (hh + 1) * RET_DV)
    kd2 = dec_s[hh, 0]

    for j in range(n):
      rows = slice(j * c, (j + 1) * c)
      kc = k_ref[rows, qcols].astype(F32)
      k2 = (jnp.concatenate([kc, kc], axis=1) * kd2).astype(BF16)
      u = lax.dot_general(k2, v_ref[rows, vcols], tn, preferred_element_type=F32)
      u_s[0, hh, j] = u[0:RET_DK]
      u_s[1, hh, j] = u[RET_DK:]

    for d, order in ((0, range(n)), (1, reversed(range(n)))):
      cdec = cdec_s[hh, d]
      st = s0_ref[d, hh] if has_s0 else jnp.zeros((RET_DK, RET_DV), F32)
      for j in order:
        u = u_s[d, hh, j]
        u_s[d, hh, j] = st
        st = st * cdec + u
      if not has_s0:
        so_ref[d, hh] = st

    intra = dec_s[hh, 1][:, :c]
    qd_f = dec_s[hh, 2]
    qd_b = dec_s[hh, 3]
    for j in range(n):
      rows = slice(j * c, (j + 1) * c)
      qc = q_ref[rows, qcols]
      vc = v_ref[rows, vcols]
      s = lax.dot_general(qc, k_ref[rows, qcols], nt, preferred_element_type=F32) * intra
      st2 = jnp.concatenate([u_s[0, hh, j], u_s[1, hh, j]], axis=1).astype(BF16)
      inter = jnp.dot(qc, st2, preferred_element_type=F32)
      o = (jnp.dot(s.astype(BF16), vc, preferred_element_type=F32)
           + inter[:, :RET_DV] * qd_f + inter[:, RET_DV:] * qd_b)
      mu = jnp.mean(o, axis=-1, keepdims=True)
      dv = o - mu
      var = jnp.mean(dv * dv, axis=-1, keepdims=True)
      on = (dv * lax.rsqrt(var + EPS)) * ng_ref[:, vcols]
      y_ref[rows, vcols] = (on * _silu(gate_ref[rows, vcols].astype(F32))).astype(BF16)


def _retention(p, dl, ng, state0, prompt):
  seq_len = SEQ_P if prompt else SEQ_S
  nb = NB_P if prompt else NB_S
  nh = RET_HEADS_PER_STEP_PROMPT if prompt else RET_HEADS_PER_STEP_LATENT
  row0 = 0 if prompt else T_P // SEQ_S
  hb = RET_HEADS // nh
  kcol = hb
  vcol = (2 * RET_HEADS * RET_DK) // (nh * RET_DV)
  gcol = vcol + hb
  in_specs = [
      pl.BlockSpec((seq_len, nh * RET_DK), lambda h, b: (row0 + b, h)),
      pl.BlockSpec((seq_len, nh * RET_DK), lambda h, b: (row0 + b, kcol + h)),
      pl.BlockSpec((seq_len, nh * RET_DV), lambda h, b: (row0 + b, vcol + h)),
      pl.BlockSpec((seq_len, nh * RET_DV), lambda h, b: (row0 + b, gcol + h)),
      pl.BlockSpec((nh, 2, 1, RET_DK), lambda h, b: (h, 0, 0, 0)),
      pl.BlockSpec((1, nh * RET_DV), lambda h, b: (0, h)),
  ]
  args = [p, p, p, p, dl, ng]
  y_spec = pl.BlockSpec((seq_len, nh * RET_DV), lambda h, b: (b, h))
  y_shape = jax.ShapeDtypeStruct((nb * seq_len, RET_HEADS * RET_DV), BF16)
  state_spec = pl.BlockSpec((None, 2, nh, RET_DK, RET_DV), lambda h, b: (b, 0, h, 0, 0))
  if prompt:
    out_specs = (y_spec, state_spec)
    out_shape = (y_shape, jax.ShapeDtypeStruct((NB_P, 2, RET_HEADS, RET_DK, RET_DV), F32))
  else:
    in_specs.append(state_spec)
    args.append(state0)
    out_specs = y_spec
    out_shape = y_shape
  return pl.pallas_call(
      functools.partial(_ret_kernel, seq_len=seq_len, has_s0=not prompt, nh=nh),
      grid=(hb, nb),
      in_specs=in_specs,
      out_specs=out_specs,
      out_shape=out_shape,
      scratch_shapes=[
          pltpu.VMEM((2, nh, seq_len // RET_CHUNK, RET_DK, RET_DV), F32),
          pltpu.VMEM((nh, 4, RET_CHUNK, RET_DV), F32),
          pltpu.VMEM((nh, 2, 1, RET_DV), F32),
      ],
      compiler_params=_params(("arbitrary", "arbitrary")),
      name="ret_prompt" if prompt else "ret_latent",
  )(*args)


def _router_kernel(x_ref, *route_refs):
  _route_tile(x_ref[...], *route_refs)


def _route_tile(x, g_ref, sh_ref, sc_ref, wrh_ref, wrl_ref, br_ref, triu_ref,
                info_ref, cnt_ref, before_ref, carry_s):
  @pl.when(pl.program_id(0) == 0)
  def _():
    carry_s[...] = jnp.zeros_like(carry_s)

  h = _norm_mod(x, g_ref[...], sh_ref[...], sc_ref[...])
  logits = _router_logits(h, wrh_ref, wrl_ref, br_ref)
  lt = logits.T[0:ROUTER_ROWS]
  row = lax.broadcasted_iota(jnp.int32, lt.shape, 0)
  big = jnp.int32(ROUTER_ROWS)

  def first_max(v):
    m = jnp.max(v, axis=0, keepdims=True)
    return jnp.min(jnp.where(v == m, row, big), axis=0, keepdims=True)

  gidx = first_max(jnp.where(row < N_GROUPS, lt, NEG))
  lo = N_GROUPS + EPG * gidx
  le = jnp.where((row >= lo) & (row < lo + EPG), lt, NEG)
  i1 = first_max(le)
  i2 = first_max(jnp.where(row == i1, NEG, le))
  e_lo = jnp.minimum(i1, i2) - N_GROUPS
  e_hi = jnp.maximum(i1, i2) - N_GROUPS
  a = e_lo - EPG * gidx
  b = e_hi - EPG * gidx
  pair_base = jnp.where(a == 0, 0, jnp.where(a == 1, 3, 5))
  cls = N_PAIRS * gidx + pair_base + (b - a - 1)

  onehot = jnp.where(row == cls, 1.0, 0.0)
  before = jnp.dot(onehot.astype(BF16), triu_ref[...], preferred_element_type=F32) + carry_s[...]
  rank = jnp.sum(jnp.where(row == cls, before, 0.0), axis=0, keepdims=True)
  before_ref[...] = carry_s[:, 0:128]
  carry_s[...] = carry_s[...] + jnp.sum(onehot, axis=1, keepdims=True)
  cnt_ref[...] = carry_s[...]

  row8 = lax.broadcasted_iota(jnp.int32, info_ref.shape, 0)
  info_ref[...] = jnp.where(row8 == 0, cls.astype(F32), jnp.where(row8 == 1, rank, 0.0))


_ROUTE_IN_SPECS = [
    pl.BlockSpec((1, D), lambda i: (0, 0)),
    pl.BlockSpec((None, 1, D), lambda i: (_seg(i, TM), 0, 0)),
    pl.BlockSpec((None, 1, D), lambda i: (_seg(i, TM), 0, 0)),
    pl.BlockSpec((D, 128), lambda i: (0, 0)),
    pl.BlockSpec((D, 128), lambda i: (0, 0)),
    pl.BlockSpec((1, 128), lambda i: (0, 0)),
    pl.BlockSpec((TM, TM), lambda i: (0, 0)),
]
_ROUTE_OUT_SPECS = (
    pl.BlockSpec((None, 8, TM), lambda i: (i, 0, 0)),
    pl.BlockSpec((ROUTER_ROWS, TM), lambda i: (0, 0)),
    pl.BlockSpec((None, ROUTER_ROWS, 128), lambda i: (i, 0, 0)),
)
_ROUTE_OUT_SHAPE = (
    jax.ShapeDtypeStruct((T // TM, 8, TM), F32),
    jax.ShapeDtypeStruct((ROUTER_ROWS, TM), F32),
    jax.ShapeDtypeStruct((T // TM, ROUTER_ROWS, 128), F32),
)
_ROUTE_SCRATCH = [pltpu.VMEM((ROUTER_ROWS, TM), F32)]


def _router(x, route_args):
  return pl.pallas_call(
      _router_kernel,
      grid=(T // TM,),
      in_specs=[pl.BlockSpec((TM, D), lambda i: (i, 0))] + _ROUTE_IN_SPECS,
      out_specs=_ROUTE_OUT_SPECS,
      out_shape=_ROUTE_OUT_SHAPE,
      scratch_shapes=_ROUTE_SCRATCH,
      compiler_params=_params(("arbitrary",)),
      name="router",
  )(x, *route_args)


def _slot_of(slot_ref, token):
  return slot_ref[token]


def _tile_rows(ref, row):
  return ref.at[pl.ds(pl.multiple_of(row * ROW_SUB, ROW_SUB), ROW_SUB), :]


RUN_CHUNK = 8


def _run_copy(stage, buf, hs_hbm, sem, src, dst, rows):
  return pltpu.make_async_copy(
      stage.at[buf, pl.ds(pl.multiple_of(src * ROW_SUB, ROW_SUB), rows * ROW_SUB), :],
      hs_hbm.at[pl.ds(pl.multiple_of(dst * ROW_SUB, ROW_SUB), rows * ROW_SUB), :],
      sem.at[buf])


def _start_run_copies(hist_ref, lstart_ref, dst_ref, step, stage, buf, hs_hbm, sem):
  shift = RUN_CHUNK.bit_length() - 1
  for c in range(N_CLASSES):
    k = step * N_CLASSES + c
    n_rows, src, dst = hist_ref[k], lstart_ref[k], dst_ref[k]

    def chunk(q, carry, src=src, dst=dst):
      _run_copy(stage, buf, hs_hbm, sem, src + q * RUN_CHUNK, dst + q * RUN_CHUNK, RUN_CHUNK).start()
      return carry
    n_chunks = lax.shift_right_logical(n_rows, jnp.int32(shift))
    lax.fori_loop(0, n_chunks, chunk, 0)
    src, dst = src + n_chunks * RUN_CHUNK, dst + n_chunks * RUN_CHUNK
    size = RUN_CHUNK // 2
    while size:
      take = n_rows & size

      @pl.when(take != 0)
      def _(src=src, dst=dst, size=size):
        _run_copy(stage, buf, hs_hbm, sem, src, dst, size).start()
      src, dst = src + take, dst + take
      size //= 2


def _wait_run_copies(stage, buf, hs_hbm, sem):
  pltpu.make_async_copy(stage.at[buf], hs_hbm.at[pl.ds(0, TS * ROW_SUB), :], sem.at[buf]).wait()


def _scatter_kernel(hist_ref, lstart_ref, dst_ref, pstart_ref, pn_ref, nv_ref,
                    x_ref, g_ref, sh_ref, sc_ref, pos_ref, hs_hbm,
                    stage, sem, zsem):
  i = pl.program_id(0)
  n = pl.num_programs(0)
  buf = i % 2

  def zero_rows(dst, rows):
    return pltpu.make_async_copy(
        stage.at[1, pl.ds(0, rows * ROW_SUB), :],
        hs_hbm.at[pl.ds(pl.multiple_of(dst * ROW_SUB, ROW_SUB), rows * ROW_SUB), :], zsem.at[0])

  def pad_copies(apply):
    shift = RUN_CHUNK.bit_length() - 1
    for c in range(N_CLASSES):
      n_rows, dst = pn_ref[c], pstart_ref[c]
      n_chunks = lax.shift_right_logical(n_rows, jnp.int32(shift))

      def chunk(q, carry, dst=dst):
        apply(zero_rows(dst + q * RUN_CHUNK, RUN_CHUNK))
        return carry
      lax.fori_loop(0, n_chunks, chunk, 0)
      dst = dst + n_chunks * RUN_CHUNK
      size = RUN_CHUNK // 2
      while size:
        take = n_rows & size

        @pl.when(take != 0)
        def _(dst=dst, size=size):
          apply(zero_rows(dst, size))
        dst = dst + take
        size //= 2

  def tail_copy(t):
    rows = TMM * ROW_SUB
    return pltpu.make_async_copy(
        stage.at[1, pl.ds(0, rows), :],
        hs_hbm.at[pl.ds(pl.multiple_of(t * rows, rows), rows), :], zsem.at[1])

  @pl.when(i == 0)
  def _():
    stage[...] = jnp.zeros_like(stage)
    pad_copies(lambda cp: cp.start())

    def tail_start(t, carry):
      tail_copy(t).start()
      return carry
    lax.fori_loop(nv_ref[0], NT_E, tail_start, 0)
    pad_copies(lambda cp: cp.wait())

    def tail_wait(t, carry):
      tail_copy(t).wait()
      return carry
    lax.fori_loop(nv_ref[0], NT_E, tail_wait, 0)

  h = _norm_mod(x_ref[...], g_ref[...], sh_ref[...], sc_ref[...])
  pos = pos_ref[0:1, :]
  row = lax.broadcasted_iota(jnp.int32, (TS, TS), 0).astype(F32)
  perm = jnp.where(row == pos, 1.0, 0.0).astype(BF16)
  hsort = jnp.dot(perm, h.astype(BF16), preferred_element_type=F32)
  for s in range(ROW_SUB):
    stage[buf, pl.ds(s, TS, stride=ROW_SUB), :] = hsort[:, s * 128:(s + 1) * 128]

  _start_run_copies(hist_ref, lstart_ref, dst_ref, i, stage, buf, hs_hbm, sem)
  pl.when(i > 0)(lambda: _wait_run_copies(stage, 1 - buf, hs_hbm, sem))
  pl.when(i == n - 1)(lambda: _wait_run_copies(stage, buf, hs_hbm, sem))


def _scatter_rows(hist, lstart, dst, pad_start, pad_n, n_valid, x, g, shift, scale, pos):
  assert TS == TM and TS >= TMM
  grid_spec = pltpu.PrefetchScalarGridSpec(
      num_scalar_prefetch=6,
      grid=(T // TS,),
      in_specs=[
          pl.BlockSpec((TS, D), lambda i, *_: (i, 0)),
          pl.BlockSpec((1, D), lambda i, *_: (0, 0)),
          pl.BlockSpec((None, 1, D), lambda i, *_: (_seg(i, TS), 0, 0)),
          pl.BlockSpec((None, 1, D), lambda i, *_: (_seg(i, TS), 0, 0)),
          pl.BlockSpec((None, 8, TS), lambda i, *_: (i, 0, 0)),
      ],
      out_specs=pl.BlockSpec(memory_space=pl.ANY),
      scratch_shapes=[
          pltpu.VMEM((2, TS * ROW_SUB, 128), F32),
          pltpu.SemaphoreType.DMA((2,)),
          pltpu.SemaphoreType.DMA((2,)),
      ],
  )
  return pl.pallas_call(
      _scatter_kernel,
      grid_spec=grid_spec,
      out_shape=jax.ShapeDtypeStruct((P_PAD * ROW_SUB, 128), F32),
      compiler_params=_params(("arbitrary",)),
      name="scatter_rows",
  )(hist, lstart, dst, pad_start, pad_n, n_valid, x, g, shift, scale, pos)


def _router_logits(h, wrh_ref, wrl_ref, br_ref):
  h_hi = h.astype(BF16)
  h_lo = (h - h_hi.astype(F32)).astype(BF16)
  dot = functools.partial(jnp.dot, preferred_element_type=F32)
  return (dot(h_hi, wrh_ref[...]) + dot(h_lo, wrh_ref[...]) + dot(h_hi, wrl_ref[...])
          + br_ref[...])


def _expert_kernel(elo_ref, ehi_ref, nv_ref, hs_ref, wrh_ref, br_ref,
                   wg_lo, wg_hi, wu_lo, wu_hi, wd_lo, wd_hi, y_ref):
  i = pl.program_id(0)

  @pl.when(i < nv_ref[0])
  def _():
    h = jnp.concatenate(
        [hs_ref[pl.ds(s, TMM, stride=ROW_SUB), :] for s in range(ROW_SUB)], axis=1)
    hb = h.astype(BF16)
    logits = jnp.dot(hb, wrh_ref[...], preferred_element_type=F32) + br_ref[...]
    lane = lax.broadcasted_iota(jnp.int32, logits.shape, 1)
    elo, ehi = elo_ref[i], ehi_ref[i]

    def pick(idx):
      return jnp.sum(jnp.where(lane == idx, logits, 0.0), axis=-1, keepdims=True)
    l_g, l_lo, l_hi = pick(elo // EPG), pick(N_GROUPS + elo), pick(N_GROUPS + ehi)
    p_top = 1.0 / jnp.sum(jnp.where(lane < N_GROUPS, jnp.exp(logits - l_g), 0.0),
                          axis=-1, keepdims=True)
    wl = jnp.broadcast_to(p_top / (1.0 + jnp.exp(l_hi - l_lo)), (TMM, D_EXPERT))
    wh = jnp.broadcast_to(p_top / (1.0 + jnp.exp(l_lo - l_hi)), (TMM, D_EXPERT))
    dot = functools.partial(jnp.dot, preferred_element_type=F32)
    a_lo = (_silu(dot(hb, wg_lo[...])) * dot(hb, wu_lo[...])) * wl
    a_hi = (_silu(dot(hb, wg_hi[...])) * dot(hb, wu_hi[...])) * wh
    a_lo, a_hi = a_lo.astype(BF16), a_hi.astype(BF16)
    nb = 2 * 128
    for c in range(0, D, nb):
      y = dot(a_lo, wd_lo[:, c:c + nb]) + dot(a_hi, wd_hi[:, c:c + nb])
      for s in range(c // 128, (c + nb) // 128):
        y_ref[pl.ds(s, TMM, stride=ROW_SUB), :] = y[:, s * 128 - c:(s + 1) * 128 - c]

  @pl.when(i >= nv_ref[0])
  def _():
    y_ref[...] = jnp.zeros_like(y_ref)


def _experts(layer, tile_elo, tile_ehi, n_valid, hs, wr_hi, br, w_gate, w_up, w_down):
  up_spec_lo = pl.BlockSpec((None, None, D, D_EXPERT), lambda i, elo, ehi, nv: (layer, elo[i], 0, 0))
  up_spec_hi = pl.BlockSpec((None, None, D, D_EXPERT), lambda i, elo, ehi, nv: (layer, ehi[i], 0, 0))
  dn_spec_lo = pl.BlockSpec((None, None, D_EXPERT, D), lambda i, elo, ehi, nv: (layer, elo[i], 0, 0))
  dn_spec_hi = pl.BlockSpec((None, None, D_EXPERT, D), lambda i, elo, ehi, nv: (layer, ehi[i], 0, 0))
  grid_spec = pltpu.PrefetchScalarGridSpec(
      num_scalar_prefetch=3,
      grid=(NT_E,),
      in_specs=[
          pl.BlockSpec((TMM * ROW_SUB, 128), lambda i, elo, ehi, nv: (i, 0)),
          pl.BlockSpec((D, 128), lambda i, elo, ehi, nv: (0, 0)),
          pl.BlockSpec((1, 128), lambda i, elo, ehi, nv: (0, 0)),
          up_spec_lo, up_spec_hi, up_spec_lo, up_spec_hi, dn_spec_lo, dn_spec_hi,
      ],
      out_specs=pl.BlockSpec((TMM * ROW_SUB, 128), lambda i, elo, ehi, nv: (i, 0)),
  )
  return pl.pallas_call(
      _expert_kernel,
      grid_spec=grid_spec,
      out_shape=jax.ShapeDtypeStruct((P_PAD * ROW_SUB, 128), F32),
      compiler_params=_params(("arbitrary",)),
      name="experts",
  )(tile_elo, tile_ehi, n_valid, hs, wr_hi, br,
    w_gate, w_gate, w_up, w_up, w_down, w_down)


def _gather_copy(slot_ref, step, r, y_hbm, ybuf, buf, sem):
  return pltpu.make_async_copy(
      _tile_rows(y_hbm, _slot_of(slot_ref, step * TC + r)),
      ybuf.at[buf, pl.ds(r * ROW_SUB, ROW_SUB), :], sem.at[buf])


def _combine_kernel(slot_ref, x_ref, gate_ref, nf_ref, y_hbm, *rest, final):
  if final:
    op_ref, os_ref, ybuf, sem = rest
  else:
    o_ref, ybuf, sem = rest
  i = pl.program_id(0)
  n = pl.num_programs(0)
  buf = i % 2

  def start(step, b):
    def body(r2, carry):
      for q in range(2):
        _gather_copy(slot_ref, step, 2 * r2 + q, y_hbm, ybuf, b, sem).start(priority=q)
      return carry
    lax.fori_loop(0, TC // 2, body, 0, unroll=4)

  @pl.when(i == 0)
  def _():
    start(0, 0)

  @pl.when(i + 1 < n)
  def _():
    start(i + 1, 1 - buf)

  def wait(r, carry):
    _gather_copy(slot_ref, i, r, y_hbm, ybuf, buf, sem).wait()
    return carry
  lax.fori_loop(0, TC, wait, 0, unroll=8)

  y = jnp.concatenate(
      [ybuf[buf, pl.ds(s, TC, stride=ROW_SUB), :] for s in range(ROW_SUB)], axis=1)
  x = x_ref[...] + gate_ref[...] * y
  if not final:
    o_ref[...] = x
    return
  x = (x * lax.rsqrt(jnp.mean(x * x, axis=-1, keepdims=True) + EPS)) * nf_ref[...]
  is_prompt = i < T_P // TC

  @pl.when(is_prompt)
  def _():
    op_ref[...] = x

  @pl.when(jnp.logical_not(is_prompt))
  def _():
    os_ref[...] = x


def _combine(slot, x, gate, norm_f, y_sorted, final):
  npt = T_P // TC
  if final:
    out_specs = (pl.BlockSpec((TC, D), lambda i, s: (jnp.minimum(i, npt - 1), 0)),
                 pl.BlockSpec((TC, D), lambda i, s: (jnp.maximum(i - npt, 0), 0)))
    out_shape = (jax.ShapeDtypeStruct((T_P, D), F32), jax.ShapeDtypeStruct((T_S, D), F32))
  else:
    out_specs = pl.BlockSpec((TC, D), lambda i, s: (i, 0))
    out_shape = jax.ShapeDtypeStruct((T, D), F32)
  grid_spec = pltpu.PrefetchScalarGridSpec(
      num_scalar_prefetch=1,
      grid=(T // TC,),
      in_specs=[
          pl.BlockSpec((TC, D), lambda i, s: (i, 0)),
          pl.BlockSpec((None, 1, D), lambda i, s: (_seg(i, TC), 0, 0)),
          pl.BlockSpec((1, D), lambda i, s: (0, 0)),
          pl.BlockSpec(memory_space=pl.ANY),
      ],
      out_specs=out_specs,
      scratch_shapes=[pltpu.VMEM((2, TC * ROW_SUB, 128), F32), pltpu.SemaphoreType.DMA((2,))],
  )
  return pl.pallas_call(
      functools.partial(_combine_kernel, final=final),
      grid_spec=grid_spec,
      out_shape=out_shape,
      compiler_params=_params(("arbitrary",)),
      name="combine_final" if final else "combine",
  )(slot, x, gate, norm_f, y_sorted)


def _class_experts():
  lo, hi = [], []
  for g in range(N_GROUPS):
    for a in range(EPG):
      for b in range(a + 1, EPG):
        lo.append(g * EPG + a)
        hi.append(g * EPG + b)
  return np.asarray(lo, np.int32), np.asarray(hi, np.int32)


def _moe_layer(layer, x, routing, route_args, gate, w_gate, w_up, w_down, norm_f):
  final = layer == DEPTH - 1
  g, shift, scale, wr_hi, _, br, _ = route_args
  info, cnt, before_tiles = routing

  cls = info[:, 0, :].astype(jnp.int32)
  rank = info[:, 1, :].astype(jnp.int32)
  counts = cnt[:N_CLASSES, 0].astype(jnp.int32)
  tiles = (counts + TMM - 1) // TMM
  tile_end = jnp.cumsum(tiles)
  offs = (tile_end - tiles) * TMM
  before = before_tiles[:, :N_CLASSES, 0].astype(jnp.int32)
  hist = jnp.concatenate([before[1:], counts[None, :]], axis=0) - before
  lstart = jnp.cumsum(hist, axis=1) - hist
  dst = offs[None, :] + before
  slot = rank
  pos = rank
  for k in range(N_CLASSES):
    hit = cls == k
    slot = slot + jnp.where(hit, offs[k], 0)
    pos = pos + jnp.where(hit, (lstart[:, k] - before[:, k])[:, None], 0)
  slot = slot.reshape(T)
  pos = jnp.broadcast_to(pos.astype(F32)[:, None, :], (T // TM, 8, TM))
  n_valid = tile_end[-1]
  tile_ids = jnp.minimum(jnp.arange(NT_E, dtype=jnp.int32), n_valid - 1)
  tile_cls = jnp.sum((tile_ids[:, None] >= tile_end[None, :]).astype(jnp.int32), axis=1)
  tile_cls = jnp.minimum(tile_cls, N_CLASSES - 1)
  cls_lo, cls_hi = _class_experts()
  tile_elo = jnp.asarray(cls_lo)[tile_cls]
  tile_ehi = jnp.asarray(cls_hi)[tile_cls]

  n_valid = n_valid.reshape(1)
  hs = _scatter_rows(hist.reshape(-1), lstart.reshape(-1), dst.reshape(-1),
                     offs + counts, tiles * TMM - counts, n_valid, x, g, shift, scale, pos)
  y_sorted = _experts(layer, tile_elo, tile_ehi, n_valid, hs, wr_hi, br, w_gate, w_up, w_down)
  return _combine(slot, x, gate, norm_f, y_sorted, final)


def kernel(x_prompt, x_sample, cache_k, cache_v, state_ret, c, c_ctx, norm1, norm2, w_ada, b_ada, attn_w_qkv, attn_q_norm, attn_k_norm, attn_w_o, pool_w, pool_scale, ret_w_in, ret_decay_logit, ret_norm, ret_w_out, moe_w_router_g, moe_b_router_g, moe_w_router_e, moe_b_router_e, moe_w_gate, moe_w_up, moe_w_down, norm_f):
  x = (x_prompt.reshape(T_P, D), x_sample.reshape(T_S, D))
  cond8 = jnp.concatenate([c_ctx[None, :], c, jnp.zeros((N_SEG - 1 - NB_S, D), F32)], axis=0)
  mods = _adaln(cond8, w_ada, b_ada)
  mods = mods.reshape(DEPTH, N_SEG, 6, 1, D).transpose(0, 2, 1, 3, 4)
  rope = _rope_tables()
  tri = jnp.triu(jnp.ones((TM, TM), BF16), 1)
  pad_r = 128 - N_GROUPS - N_EXPERTS
  norm_f2 = norm_f.reshape(1, D)

  w_gate, w_up, w_down = moe_w_gate.astype(BF16), moe_w_up.astype(BF16), moe_w_down.astype(BF16)

  new_k, new_v, new_s = [], [], []
  for i in range(DEPTH):
    kind, j = i % 3, i // 3
    m = mods[i]
    g1 = norm1[i].reshape(1, D)
    wr = jnp.concatenate([moe_w_router_g[i], moe_w_router_e[i], jnp.zeros((D, pad_r), F32)], axis=1)
    br = jnp.concatenate([moe_b_router_g[i], moe_b_router_e[i], jnp.zeros((pad_r,), F32)]).reshape(1, 128)
    wr_hi = wr.astype(BF16)
    wr_lo = (wr - wr_hi.astype(F32)).astype(BF16)
    route_args = (norm2[i].reshape(1, D), m[3], m[4], wr_hi, wr_lo, br, tri)
    if kind == 0:
      qkv = _nm_matmul(x, g1, m[0], m[1], attn_w_qkv[j].astype(BF16), TM, "qkv_proj")
      qn = attn_q_norm[j].reshape(1, HEAD_DIM)
      kn = attn_k_norm[j].reshape(1, HEAD_DIM)
      o_p, kc, vc = _attn_prompt(qkv, qn, kn)
      ck = cache_k[:, j].reshape(NB_S, PAST, N_KV * HEAD_DIM)
      cv = cache_v[:, j].reshape(NB_S, PAST, N_KV * HEAD_DIM)
      o_s = _attn_latent(qkv, ck, cv, qn, kn, rope)
      new_k.append(kc.reshape(NB_P, SEQ_P, N_KV, HEAD_DIM))
      new_v.append(vc.reshape(NB_P, SEQ_P, N_KV, HEAD_DIM))
      x, *routing = _mm_res_route((o_p, o_s), attn_w_o[j].astype(BF16), x, m[2], route_args, "attn_out")
    elif kind == 1:
      x = _pool_layer(x, g1, m[0], m[1], m[2], pool_w[j].astype(BF16), pool_scale[j].reshape(1, D))
      routing = _router(x, route_args)
    else:
      p = _nm_matmul(x, g1, m[0], m[1], ret_w_in[j].astype(BF16), TM, "ret_proj")
      dl = jnp.broadcast_to(ret_decay_logit[j].T[:, :, None, None], (RET_HEADS, 2, 1, RET_DK))
      ng = ret_norm[j].reshape(1, RET_HEADS * RET_DV)
      y_p, s_new = _retention(p, dl, ng, None, True)
      y_s = _retention(p, dl, ng, state_ret[:, j], False)
      new_s.append(s_new)
      x, *routing = _mm_res_route((y_p, y_s), ret_w_out[j].astype(BF16), x, m[2], route_args, "ret_out")
    x = _moe_layer(i, x, routing, route_args, m[5], w_gate, w_up, w_down, norm_f2)

  y_prompt = x[0].reshape(NB_P, SEQ_P, D)
  y_sample = x[1].reshape(NB_S, SEQ_S, D)
  new_cache_k = jnp.stack(new_k, axis=1)
  new_cache_v = jnp.stack(new_v, axis=1)
  assert len(new_s) == 1
  new_state_ret = new_s[0].reshape(NB_P, 1, 2, RET_HEADS, RET_DK, RET_DV)
  return (y_prompt, y_sample, new_cache_k, new_cache_v, new_state_ret)
```
